```python
import jax
import jax.numpy as jnp
from jax import lax


D_MODEL = 1024
BATCH = 32
SEQ = 2048
DEPTH = 4

CHUNK = 64
HEAD_DIM = 64
HEADS_PER_GROUP = 4
GROUP_WIDTH = HEADS_PER_GROUP * HEAD_DIM
N_GROUPS = 4
D_MIX = N_GROUPS * GROUP_WIDTH
Q_BLOCK = 128
ROPE_THETA = 10000.0
NORM_EPS = 1e-6
NEG_INF = -1e30
KV_RANK = 128
IDX_HEADS = 4
IDX_DIM = 32
TOPK_MAX = 256
LEFT_CHUNKS = 8
MAX_REL = 128
N_EXPERTS = 32
TOP_K = 4
D_FF = D_MODEL
SWIGLU_ALPHA = 1.702
SWIGLU_LIMIT = 7.0
MOE_BLOCK = 512

A_COLS = 3 * GROUP_WIDTH + HEADS_PER_GROUP
B_COLS = GROUP_WIDTH + KV_RANK + IDX_HEADS * IDX_DIM + IDX_DIM + IDX_HEADS
C_COLS = 3 * GROUP_WIDTH
D_COLS = 3 * GROUP_WIDTH
IN_COLS = A_COLS + B_COLS + C_COLS + D_COLS

kernel_name = 'hybrid_chunk_causal_moe_trunk'


def rmsnorm(x, g):
    xf = x.astype(jnp.float32)
    y = xf * lax.rsqrt(jnp.mean(xf * xf, axis=-1, keepdims=True) + NORM_EPS)
    return (y * g.astype(jnp.float32)).astype(x.dtype)


def rope(x, pos):
    half = x.shape[-1] // 2
    inv = ROPE_THETA ** (-jnp.arange(half, dtype=jnp.float32) / half)
    ang = pos.astype(jnp.float32)[:, None] * inv[None, :]
    cos = jnp.cos(ang)[None, :, None, :]
    sin = jnp.sin(ang)[None, :, None, :]
    xf = x.astype(jnp.float32)
    x1, x2 = xf[..., :half], xf[..., half:]
    return jnp.concatenate([x1 * cos - x2 * sin, x1 * sin + x2 * cos], axis=-1).astype(x.dtype)


def merge_blocks(out):
    nb, b, blk, h, d = out.shape
    return out.transpose(1, 0, 2, 3, 4).reshape(b, nb * blk, h, d)


def forgetting_attention(q, k, v, log_f):
    S, Dh = q.shape[1], q.shape[-1]
    F = jnp.cumsum(log_f.astype(jnp.float32), axis=1).transpose(0, 2, 1)
    pos = jnp.arange(S)

    def block(i):
        q0 = i * Q_BLOCK
        qb = lax.dynamic_slice_in_dim(q, q0, Q_BLOCK, axis=1)
        Fq = lax.dynamic_slice_in_dim(F, q0, Q_BLOCK, axis=2)
        s = jnp.einsum('bqhd,bkhd->bhqk', qb, k).astype(jnp.float32) * (Dh ** -0.5)
        s = s + Fq[..., :, None] - F[..., None, :]
        qpos = q0 + jnp.arange(Q_BLOCK)
        s = jnp.where((pos[None, :] <= qpos[:, None])[None, None], s, NEG_INF)
        p = jax.nn.softmax(s, axis=-1)
        return jnp.einsum('bhqk,bkhd->bqhd', p.astype(v.dtype), v)

    return merge_blocks(lax.map(block, jnp.arange(S // Q_BLOCK)))


def dsa_attention(q, k, v, q_idx, k_idx, w_idx):
    S, Dh = q.shape[1], q.shape[-1]
    n_sel = min(TOPK_MAX, S // 4)
    pos = jnp.arange(S)
    limit = (pos // CHUNK + 1) * CHUNK

    def block(i):
        q0 = i * Q_BLOCK
        qib = lax.dynamic_slice_in_dim(q_idx, q0, Q_BLOCK, axis=1)
        wib = lax.dynamic_slice_in_dim(w_idx, q0, Q_BLOCK, axis=1).astype(jnp.float32)
        lim = lax.dynamic_slice_in_dim(limit, q0, Q_BLOCK)
        rel = jax.nn.relu(jnp.einsum('bqhd,bkd->bqhk', qib, k_idx).astype(jnp.float32) * (IDX_DIM ** -0.5))
        score = jnp.einsum('bqh,bqhk->bqk', wib, rel)
        score = jnp.where((pos[None, :] < lim[:, None])[None], score, NEG_INF)
        _, idx = lax.top_k(score, n_sel)
        valid = idx < lim[None, :, None]
        k_sel = jax.vmap(lambda kb, ib: kb[ib])(k, idx)
        v_sel = jax.vmap(lambda vb, ib: vb[ib])(v, idx)
        qb = lax.dynamic_slice_in_dim(q, q0, Q_BLOCK, axis=1)
        s = jnp.einsum('bqhd,bqnd->bhqn', qb, k_sel).astype(jnp.float32) * (Dh ** -0.5)
        s = jnp.where(valid[:, None], s, NEG_INF)
        p = jax.nn.softmax(s, axis=-1)
        return jnp.einsum('bhqn,bqnd->bqhd', p.astype(v.dtype), v_sel)

    return merge_blocks(lax.map(block, jnp.arange(S // Q_BLOCK)))


def stick_breaking_attention(q, k, v):
    S, Dh = q.shape[1], q.shape[-1]
    pos = jnp.arange(S)

    def block(i):
        q0 = i * Q_BLOCK
        qb = lax.dynamic_slice_in_dim(q, q0, Q_BLOCK, axis=1)
        z = jnp.einsum('bqhd,bkhd->bhqk', qb, k).astype(jnp.float32) * (Dh ** -0.5)
        qpos = q0 + jnp.arange(Q_BLOCK)
        mask = (pos[None, :] < qpos[:, None])[None, None]
        log_1m = jnp.where(mask, jax.nn.log_sigmoid(-z), 0.0)
        after = lax.cumsum(log_1m, axis=3, reverse=True) - log_1m
        w = jnp.where(mask, jnp.exp(jax.nn.log_sigmoid(z) + after), 0.0)
        return jnp.einsum('bhqk,bkhd->bqhd', w.astype(v.dtype), v)

    return merge_blocks(lax.map(block, jnp.arange(S // Q_BLOCK)))


def chunked_relbias_attention(q, k, v, rel_table):
    S, Dh = q.shape[1], q.shape[-1]
    pad = LEFT_CHUNKS * CHUNK
    band = pad + CHUNK
    kp = jnp.pad(k, ((0, 0), (pad, 0), (0, 0), (0, 0)))
    vp = jnp.pad(v, ((0, 0), (pad, 0), (0, 0), (0, 0)))
    j = jnp.arange(band)
    rel = j[None, :] - pad - jnp.arange(CHUNK)[:, None]
    bias = rel_table[:, jnp.clip(rel, -MAX_REL, MAX_REL) + MAX_REL].astype(jnp.float32)

    def chunk(ci):
        c0 = ci * CHUNK
        qb = lax.dynamic_slice_in_dim(q, c0, CHUNK, axis=1)
        kb = lax.dynamic_slice_in_dim(kp, c0, band, axis=1)
        vb = lax.dynamic_slice_in_dim(vp, c0, band, axis=1)
        s = jnp.einsum('bqhd,bkhd->bhqk', qb, kb).astype(jnp.float32) * (Dh ** -0.5) + bias[None]
        valid = (c0 - pad + j) >= 0
        s = jnp.where(valid[None, None, None, :], s, NEG_INF)
        p = jax.nn.softmax(s, axis=-1)
        return jnp.einsum('bhqk,bkhd->bqhd', p.astype(vb.dtype), vb)

    return merge_blocks(lax.map(chunk, jnp.arange(S // CHUNK)))


def hybrid_mixer(h, w_in, b_forget, kv_norm_g, w_uk, w_uv, rel_table, w_o):
    B, S, _ = h.shape
    heads = lambda t: t.reshape(B, S, HEADS_PER_GROUP, HEAD_DIM)
    pos = jnp.arange(S)
    proj = h @ w_in
    pa, pb, pc, pd = jnp.split(proj, [A_COLS, A_COLS + B_COLS, A_COLS + B_COLS + C_COLS], axis=-1)
    GW = GROUP_WIDTH
    qa, ka, va, fa = jnp.split(pa, [GW, 2 * GW, 3 * GW], axis=-1)
    log_f = jax.nn.log_sigmoid(fa.astype(jnp.float32) + b_forget.astype(jnp.float32))
    ya = forgetting_attention(heads(qa), heads(ka), heads(va), log_f)
    ob = [GW, GW + KV_RANK, GW + KV_RANK + IDX_HEADS * IDX_DIM, GW + KV_RANK + IDX_HEADS * IDX_DIM + IDX_DIM]
    qb, lat, qi, ki, wi = jnp.split(pb, ob, axis=-1)
    lat = rmsnorm(lat, kv_norm_g)
    kb = rope((lat @ w_uk)[:, :, None, :], pos)[:, :, 0, :]
    vb = lat @ w_uv
    qb = rope(heads(qb), pos)
    qi = rope(qi.reshape(B, S, IDX_HEADS, IDX_DIM), pos)
    ki = rope(ki[:, :, None, :], pos)[:, :, 0, :]
    wi = wi * (IDX_HEADS ** -0.5)
    yb = dsa_attention(qb, kb, vb, qi, ki, wi)
    qc, kc, vc = jnp.split(pc, [GW, 2 * GW], axis=-1)
    yc = stick_breaking_attention(heads(qc), heads(kc), heads(vc))
    qd, kd, vd = jnp.split(pd, [GW, 2 * GW], axis=-1)
    yd = chunked_relbias_attention(heads(qd), heads(kd), heads(vd), rel_table)
    y = jnp.concatenate([ya.reshape(B, S, GW), yb.reshape(B, S, GW), yc.reshape(B, S, GW), yd.reshape(B, S, GW)], axis=-1)
    return y @ w_o


def clamped_swiglu(gu):
    glu, lin = jnp.split(gu, 2, axis=-1)
    glu = jnp.minimum(glu, SWIGLU_LIMIT)
    lin = jnp.clip(lin, -SWIGLU_LIMIT, SWIGLU_LIMIT)
    return glu * jax.nn.sigmoid(SWIGLU_ALPHA * glu) * (lin + 1.0)


def moe_ffn(h, w_router, b_router, w_gu, b_gu, w_dn, b_dn):
    B, S, D = h.shape
    T = B * S
    A = T * TOP_K
    hf = h.reshape(T, D)
    logits = (hf @ w_router).astype(jnp.float32) + b_router.astype(jnp.float32)
    top_vals, top_idx = lax.top_k(logits, TOP_K)
    gates = jax.nn.softmax(top_vals, axis=-1)
    flat_e = top_idx.reshape(A)
    flat_tok = jnp.arange(A, dtype=jnp.int32) // TOP_K
    flat_g = gates.reshape(A)
    order = jnp.argsort(flat_e, stable=True)
    se = flat_e[order]
    counts = jnp.bincount(flat_e, length=N_EXPERTS)
    start = jnp.cumsum(counts) - counts
    pcounts = (counts + MOE_BLOCK - 1) // MOE_BLOCK * MOE_BLOCK
    pend = jnp.cumsum(pcounts)
    pstart = pend - pcounts
    dest = pstart[se] + jnp.arange(A) - start[se]
    n_blocks = -(-A // MOE_BLOCK) + N_EXPERTS
    P = n_blocks * MOE_BLOCK
    row_tok = jnp.full((P,), T, jnp.int32).at[dest].set(flat_tok[order])
    row_g = jnp.zeros((P,), jnp.float32).at[dest].set(flat_g[order])
    blk_e = jnp.minimum(jnp.searchsorted(pend, jnp.arange(n_blocks) * MOE_BLOCK, side='right'), N_EXPERTS - 1)
    h_pad = jnp.concatenate([hf, jnp.zeros((1, D), hf.dtype)], axis=0)

    def step(acc, inp):
        rows, g, e = inp
        xb = h_pad[rows]
        y = clamped_swiglu(xb @ w_gu[e] + b_gu[e]) @ w_dn[e] + b_dn[e]
        return acc.at[rows].add((g[:, None] * y.astype(jnp.float32)).astype(acc.dtype)), None

    acc0 = jnp.zeros((T + 1, D), h.dtype)
    acc, _ = lax.scan(step, acc0, (row_tok.reshape(n_blocks, MOE_BLOCK), row_g.reshape(n_blocks, MOE_BLOCK), blk_e))
    return acc[:T].reshape(B, S, D)


def setup_inputs(seed: int = 0) -> dict:
    key = jax.random.key(seed)
    ks = jax.random.split(key, 20)
    nrm = lambda k, shape, s: jax.random.normal(k, shape, jnp.float32) * s
    return {
        'x': nrm(ks[0], (BATCH, SEQ, D_MODEL), 1.0),
        'c': nrm(ks[1], (BATCH, D_MODEL), 1.0),
        'w_ada': nrm(ks[2], (DEPTH, D_MODEL, 6 * D_MODEL), 0.5 * D_MODEL ** -0.5),
        'b_ada': nrm(ks[3], (DEPTH, 6 * D_MODEL), 0.02),
        'norm1_g': 1.0 + nrm(ks[4], (DEPTH, D_MODEL), 0.05),
        'norm2_g': 1.0 + nrm(ks[5], (DEPTH, D_MODEL), 0.05),
        'w_in': nrm(ks[6], (DEPTH, D_MODEL, IN_COLS), D_MODEL ** -0.5),
        'b_forget': 2.0 + nrm(ks[7], (DEPTH, HEADS_PER_GROUP), 0.5),
        'kv_norm_g': 1.0 + nrm(ks[8], (DEPTH, KV_RANK), 0.05),
        'w_uk': nrm(ks[9], (DEPTH, KV_RANK, HEAD_DIM), KV_RANK ** -0.5),
        'w_uv': nrm(ks[10], (DEPTH, KV_RANK, HEAD_DIM), KV_RANK ** -0.5),
        'rel_bias': nrm(ks[11], (DEPTH, HEADS_PER_GROUP, 2 * MAX_REL + 1), 0.5),
        'w_o': nrm(ks[12], (DEPTH, D_MIX, D_MODEL), D_MIX ** -0.5),
        'w_router': nrm(ks[13], (DEPTH, D_MODEL, N_EXPERTS), D_MODEL ** -0.5),
        'b_router': nrm(ks[14], (DEPTH, N_EXPERTS), 0.01),
        'w_gu': nrm(ks[15], (DEPTH, N_EXPERTS, D_MODEL, 2 * D_FF), D_MODEL ** -0.5),
        'b_gu': nrm(ks[16], (DEPTH, N_EXPERTS, 2 * D_FF), 0.01),
        'w_dn': nrm(ks[17], (DEPTH, N_EXPERTS, D_FF, D_MODEL), D_FF ** -0.5),
        'b_dn': nrm(ks[18], (DEPTH, N_EXPERTS, D_MODEL), 0.01),
        'final_g': 1.0 + nrm(ks[19], (D_MODEL,), 0.05),
    }


def reference(x, c, w_ada, b_ada, norm1_g, norm2_g, w_in, b_forget, kv_norm_g, w_uk, w_uv, rel_bias, w_o,
              w_router, b_router, w_gu, b_gu, w_dn, b_dn, final_g):
    c_act = jax.nn.silu(c)
    for l in range(DEPTH):
        mod = c_act @ w_ada[l] + b_ada[l]
        sh1, sc1, g1, sh2, sc2, g2 = [m[:, None, :] for m in jnp.split(mod, 6, axis=-1)]
        h = rmsnorm(x, norm1_g[l]) * (1.0 + sc1) + sh1
        x = x + g1 * hybrid_mixer(h, w_in[l], b_forget[l], kv_norm_g[l], w_uk[l], w_uv[l], rel_bias[l], w_o[l])
        h = rmsnorm(x, norm2_g[l]) * (1.0 + sc2) + sh2
        x = x + g2 * moe_ffn(h, w_router[l], b_router[l], w_gu[l], b_gu[l], w_dn[l], b_dn[l])
    return rmsnorm(x, final_g)
```

```python
import functools

import jax
import jax.numpy as jnp
from jax import lax
from jax.experimental import pallas as pl
from jax.experimental.pallas import tpu as pltpu

F32 = jnp.float32
BF16 = jnp.bfloat16

HEAD_DIM = 64
HEADS = 4
GROUP_WIDTH = HEADS * HEAD_DIM
CHUNK = 64
ROPE_THETA = 10000.0
NORM_EPS = 1e-6
NEG_INF = -1e30
KV_RANK = 128
IDX_HEADS = 4
IDX_DIM = 32
TOPK_MAX = 256
LEFT_CHUNKS = 8
MAX_REL = 128
N_EXPERTS = 32
TOP_K = 4
SWIGLU_ALPHA = 1.702
SWIGLU_LIMIT = 7.0

LANES = 128
VMEM_LIMIT = 48 * 1024 * 1024

COL_A = 0
COL_C = 3 * GROUP_WIDTH
COL_D = 6 * GROUP_WIDTH
COL_QB = 9 * GROUP_WIDTH
COL_LAT = COL_QB + GROUP_WIDTH
COL_QI = COL_LAT + KV_RANK
COL_MISC = COL_QI + IDX_HEADS * IDX_DIM
N_COLS = COL_MISC + LANES
MISC_WI = IDX_DIM
MISC_FA = IDX_DIM + IDX_HEADS


def _cparams(sem):
    return pltpu.CompilerParams(dimension_semantics=sem, vmem_limit_bytes=VMEM_LIMIT)


def _split3(x):
    hi = x.astype(BF16)
    r = x - hi.astype(F32)
    mid = r.astype(BF16)
    lo = (r - mid.astype(F32)).astype(BF16)
    return hi, mid, lo


def _dot(a, b):
    return jnp.dot(a, b, preferred_element_type=F32)


def _dot_nt(a, b):
    return lax.dot_general(a, b, (((1,), (1,)), ((), ())), preferred_element_type=F32)


def _mod_body(c_ref, w_ref, b_ref, o_ref):
    c = c_ref[...]
    act = (c * jax.nn.sigmoid(c)).astype(BF16)
    o_ref[0] = _dot(act, w_ref[0].astype(BF16)) + b_ref[0]


def _modulation(c, w_ada, b_ada):
    depth, d, n = w_ada.shape
    b = c.shape[0]
    tn = 1024
    return pl.pallas_call(
        _mod_body,
        out_shape=jax.ShapeDtypeStruct((depth, b, n), F32),
        grid=(depth, n // tn),
        in_specs=[
            pl.BlockSpec((b, d), lambda l, j: (0, 0)),
            pl.BlockSpec((1, d, tn), lambda l, j: (l, 0, j)),
            pl.BlockSpec((1, 1, tn), lambda l, j: (l, 0, j)),
        ],
        out_specs=pl.BlockSpec((1, b, tn), lambda l, j: (l, 0, j)),
        compiler_params=_cparams(("arbitrary", "arbitrary")),
        name="adaln_mod",
    )(c, w_ada, b_ada.reshape(depth, 1, n))


def _rope_tables(s):
    pos = jnp.arange(s, dtype=F32)[:, None]
    lane = jnp.arange(LANES)[None, :]

    def tables(dim, active):
        half = dim // 2
        j = lane % dim
        inv = ROPE_THETA ** (-(j % half).astype(F32) / half)
        ang = pos * inv
        cos = jnp.where(active, jnp.cos(ang), 1.0)
        sin = jnp.where(active, jnp.sin(ang), 0.0)
        first = j < half
        return [cos, jnp.where(first, -sin, 0.0), jnp.where(first, 0.0, sin)]

    everywhere = lane >= 0
    t = (tables(HEAD_DIM, everywhere) + tables(HEAD_DIM, lane < HEAD_DIM)
         + tables(IDX_DIM, everywhere) + tables(IDX_DIM, lane < IDX_DIM))
    return jnp.stack(t).astype(F32)


def _rope(x, tab, k, half):
    cos, sin_a, sin_b = tab[3 * k], tab[3 * k + 1], tab[3 * k + 2]
    return x * cos + pltpu.roll(x, LANES - half, 1) * sin_a + pltpu.roll(x, half, 1) * sin_b


def _instage_body(x_ref, sc_ref, sh_ref, g_ref, w_ref, wkv_ref, kvg_ref, bf_ref, tri_ref, tab_ref,
                  fq_o, fk_o, fv_o, dq_o, dkv_o, dqi_o, dkit_o, dw_o,
                  cq_o, ck_o, cv_o, eq_o, ek_o, ev_o, carry_ref):
    tm = x_ref.shape[1]
    x = x_ref[0]
    y = x * lax.rsqrt(jnp.mean(x * x, axis=-1, keepdims=True) + NORM_EPS) * g_ref[...]
    h = (y * (1.0 + sc_ref[0]) + sh_ref[0]).astype(BF16)

    def proj(c0, n):
        return _dot(h, w_ref[:, c0:c0 + n])

    def heads_out(val, refs, scale=None):
        for i, ref in enumerate(refs):
            for hd in range(HEADS):
                c0 = i * GROUP_WIDTH + hd * HEAD_DIM
                piece = val[:, c0:c0 + HEAD_DIM]
                if i == 0 and scale is not None:
                    piece = piece * scale
                ref[0, hd] = piece.astype(ref.dtype)

    scale = HEAD_DIM ** -0.5
    heads_out(proj(COL_C, 3 * GROUP_WIDTH), (cq_o, ck_o, cv_o), scale)
    heads_out(proj(COL_D, 3 * GROUP_WIDTH), (eq_o, ek_o, ev_o), scale)

    misc = _rope(proj(COL_MISC, LANES), tab_ref, 3, IDX_DIM // 2)
    lane = lax.broadcasted_iota(jnp.int32, (tm, LANES), 1)
    ki = jnp.where(lane < IDX_DIM, misc, 0.0)
    kit = ki + pltpu.roll(ki, IDX_DIM, 1) + pltpu.roll(ki, 2 * IDX_DIM, 1) + pltpu.roll(ki, 3 * IDX_DIM, 1)
    dkit_o[0] = kit.astype(BF16)
    dw_o[0] = misc * ((IDX_HEADS ** -0.5) * (IDX_DIM ** -0.5))
    qi = _rope(proj(COL_QI, LANES), tab_ref, 2, IDX_DIM // 2)
    for hd in range(IDX_HEADS):
        in_head = (lane >= hd * IDX_DIM) & (lane < (hd + 1) * IDX_DIM)
        dqi_o[0, hd] = jnp.where(in_head, qi, 0.0).astype(BF16)
    lat = proj(COL_LAT, KV_RANK)
    lat = lat * lax.rsqrt(jnp.mean(lat * lat, axis=-1, keepdims=True) + NORM_EPS) * kvg_ref[...]
    kv = _rope(_dot(lat.astype(BF16), wkv_ref[...]), tab_ref, 1, HEAD_DIM // 2)
    dkv_o[0] = kv.astype(BF16)
    qb = proj(COL_QB, GROUP_WIDTH)
    zeros64 = jnp.zeros((tm, HEAD_DIM), F32)
    for half_i in range(2):
        r = _rope(qb[:, half_i * LANES:(half_i + 1) * LANES], tab_ref, 0, HEAD_DIM // 2) * scale
        for j in range(2):
            piece = jnp.concatenate([r[:, j * HEAD_DIM:(j + 1) * HEAD_DIM], zeros64], axis=-1)
            dq_o[0, 2 * half_i + j] = piece.astype(BF16)

    @pl.when(pl.program_id(1) == 0)
    def _():
        carry_ref[...] = jnp.zeros_like(carry_ref)

    z = misc + bf_ref[...]
    log_f = jnp.minimum(z, 0.0) - jnp.log1p(jnp.exp(-jnp.abs(z)))
    tri = tri_ref[...]
    f_cum = carry_ref[...] + sum(_dot(tri, p) for p in _split3(log_f))
    carry_ref[...] = f_cum[tm - 1:tm, :]
    pa = proj(COL_A, 3 * GROUP_WIDTH)
    lane64 = lax.broadcasted_iota(jnp.int32, (tm, HEAD_DIM), 1)
    for hd in range(HEADS):
        fh = f_cum[:, MISC_FA + hd:MISC_FA + hd + 1]
        f3 = [jnp.broadcast_to(p.astype(F32), (tm, HEAD_DIM)) for p in _split3(fh)]
        one = jnp.where(lane64 < 6, 1.0, 0.0)
        q_ext = jnp.where(lane64 == 0, f3[0], jnp.where(lane64 == 1, f3[1], jnp.where(lane64 == 2, f3[2], one)))
        k_ext = jnp.where(lane64 == 3, -f3[0], jnp.where(lane64 == 4, -f3[1], jnp.where(lane64 == 5, -f3[2], one)))
        q = pa[:, hd * HEAD_DIM:(hd + 1) * HEAD_DIM] * scale
        k = pa[:, GROUP_WIDTH + hd * HEAD_DIM:GROUP_WIDTH + (hd + 1) * HEAD_DIM]
        fq_o[0, hd] = jnp.concatenate([q, q_ext], axis=-1).astype(BF16)
        fk_o[0, hd] = jnp.concatenate([k, k_ext], axis=-1).astype(BF16)
        fv_o[0, hd] = pa[:, 2 * GROUP_WIDTH + hd * HEAD_DIM:2 * GROUP_WIDTH + (hd + 1) * HEAD_DIM].astype(BF16)


def _reorder_w_in(w_in):
    gw = GROUP_WIDTH
    a0 = 0
    b0 = 3 * gw + HEADS
    c0 = b0 + gw + KV_RANK + IDX_HEADS * IDX_DIM + IDX_DIM + IDX_HEADS
    d0 = c0 + 3 * gw
    lat0 = b0 + gw
    qi0 = lat0 + KV_RANK
    ki0 = qi0 + IDX_HEADS * IDX_DIM
    wi0 = ki0 + IDX_DIM
    pad = jnp.zeros(w_in.shape[:-1] + (LANES - IDX_DIM - IDX_HEADS - HEADS,), w_in.dtype)
    w = jnp.concatenate([
        w_in[..., a0:a0 + 3 * gw], w_in[..., c0:c0 + 3 * gw], w_in[..., d0:d0 + 3 * gw],
        w_in[..., b0:b0 + gw], w_in[..., lat0:lat0 + KV_RANK], w_in[..., qi0:qi0 + IDX_HEADS * IDX_DIM],
        w_in[..., ki0:ki0 + IDX_DIM], w_in[..., wi0:wi0 + IDX_HEADS], w_in[..., 3 * gw:3 * gw + HEADS], pad,
    ], axis=-1)
    return w.astype(BF16)


def _input_stage(x, sc, sh, g, w, wkv, kvg, bf, tri, tab, tm):
    b, s, d = x.shape
    hq = lambda width, dt: jax.ShapeDtypeStruct((b, HEADS, s, width), dt)
    flat = lambda width, dt: jax.ShapeDtypeStruct((b, s, width), dt)
    out_shape = (
        hq(LANES, BF16), hq(LANES, BF16), hq(HEAD_DIM, BF16),
        hq(LANES, BF16), flat(LANES, BF16), hq(LANES, BF16), flat(LANES, BF16), flat(LANES, F32),
        hq(HEAD_DIM, BF16), hq(HEAD_DIM, BF16), hq(HEAD_DIM, BF16),
        hq(HEAD_DIM, BF16), hq(HEAD_DIM, BF16), hq(HEAD_DIM, BF16),
    )
    hspec = lambda width: pl.BlockSpec((1, HEADS, tm, width), lambda i, j: (i, 0, j, 0))
    fspec = pl.BlockSpec((1, tm, LANES), lambda i, j: (i, j, 0))
    const2 = lambda shape: pl.BlockSpec(shape, lambda i, j: (0, 0))
    out_specs = (
        hspec(LANES), hspec(LANES), hspec(HEAD_DIM),
        hspec(LANES), fspec, hspec(LANES), fspec, fspec,
        hspec(HEAD_DIM), hspec(HEAD_DIM), hspec(HEAD_DIM),
        hspec(HEAD_DIM), hspec(HEAD_DIM), hspec(HEAD_DIM),
    )
    return pl.pallas_call(
        _instage_body,
        out_shape=out_shape,
        grid=(b, s // tm),
        in_specs=[
            pl.BlockSpec((1, tm, d), lambda i, j: (i, j, 0)),
            pl.BlockSpec((1, 1, d), lambda i, j: (i, 0, 0)),
            pl.BlockSpec((1, 1, d), lambda i, j: (i, 0, 0)),
            const2((1, d)),
            const2((d, N_COLS)),
            const2((KV_RANK, LANES)),
            const2((1, KV_RANK)),
            const2((1, LANES)),
            const2((tm, tm)),
            pl.BlockSpec((12, tm, LANES), lambda i, j: (0, j, 0)),
        ],
        out_specs=out_specs,
        scratch_shapes=[pltpu.VMEM((1, LANES), F32)],
        compiler_params=_cparams(("arbitrary", "arbitrary")),
        name="mixer_input_stage",
    )(x, sc, sh, g, w, wkv, kvg, bf, tri, tab)


def _online_softmax_step(s, v, m_ref, l_ref, acc_ref):
    m_prev = m_ref[...]
    m_new = jnp.maximum(m_prev, jnp.max(s, axis=-1, keepdims=True))
    alpha = jnp.exp(m_prev - m_new)
    p = jnp.exp(s - m_new)
    l_ref[...] = alpha * l_ref[...] + jnp.sum(p, axis=-1, keepdims=True)
    acc_ref[...] = alpha * acc_ref[...] + _dot(p.astype(BF16), v)
    m_ref[...] = m_new


def _fox_body(q_ref, k_ref, v_ref, o_ref, m_ref, l_ref, acc_ref):
    qi = pl.program_id(1)
    tq = q_ref.shape[2]
    row = lax.broadcasted_iota(jnp.int32, (tq, tq), 0)
    col = lax.broadcasted_iota(jnp.int32, (tq, tq), 1)
    outs = []
    for hd in range(HEADS):
        q = q_ref[0, hd]
        m_ref[...] = jnp.full_like(m_ref, NEG_INF)
        l_ref[...] = jnp.zeros_like(l_ref)
        acc_ref[...] = jnp.zeros_like(acc_ref)

        def block(kb, masked):
            k0 = pl.multiple_of(kb * tq, tq)
            s = _dot_nt(q, k_ref[0, hd, pl.ds(k0, tq), :])
            if masked:
                s = jnp.where(col <= row, s, NEG_INF)
            _online_softmax_step(s, v_ref[0, hd, pl.ds(k0, tq), :], m_ref, l_ref, acc_ref)

        def loop_body(kb, carry):
            block(kb, False)
            return carry

        lax.fori_loop(0, qi, loop_body, 0)
        block(qi, True)
        outs.append(acc_ref[...] / l_ref[...])
    o_ref[0] = jnp.concatenate(outs, axis=-1).astype(o_ref.dtype)


def _fox_attention(q, k, v, tq):
    b, _, s, _ = q.shape
    return pl.pallas_call(
        _fox_body,
        out_shape=jax.ShapeDtypeStruct((b, s, GROUP_WIDTH), BF16),
        grid=(b, s // tq),
        in_specs=[
            pl.BlockSpec((1, HEADS, tq, LANES), lambda i, j: (i, 0, j, 0)),
            pl.BlockSpec((1, HEADS, s, LANES), lambda i, j: (i, 0, 0, 0)),
            pl.BlockSpec((1, HEADS, s, HEAD_DIM), lambda i, j: (i, 0, 0, 0)),
        ],
        out_specs=pl.BlockSpec((1, tq, GROUP_WIDTH), lambda i, j: (i, j, 0)),
        scratch_shapes=[pltpu.VMEM((tq, 1), F32), pltpu.VMEM((tq, 1), F32), pltpu.VMEM((tq, HEAD_DIM), F32)],
        compiler_params=_cparams(("arbitrary", "arbitrary")),
        name="forgetting_attention",
    )(q, k, v)


def _sb_body(q_ref, k_ref, v_ref, tri_ref, o_ref, r_ref, acc_ref):
    qi = pl.program_id(1)
    tq = q_ref.shape[2]
    row = lax.broadcasted_iota(jnp.int32, (tq, tq), 0)
    col = lax.broadcasted_iota(jnp.int32, (tq, tq), 1)
    tri = tri_ref[...]
    outs = []
    for hd in range(HEADS):
        q = q_ref[0, hd]
        r_ref[...] = jnp.zeros_like(r_ref)
        acc_ref[...] = jnp.zeros_like(acc_ref)

        def block(kb, masked):
            k0 = pl.multiple_of(kb * tq, tq)
            z = _dot_nt(q, k_ref[0, hd, pl.ds(k0, tq), :])
            log_1m = -(jnp.maximum(z, 0.0) + jnp.log1p(jnp.exp(-jnp.abs(z))))
            if masked:
                log_1m = jnp.where(col < row, log_1m, 0.0)
            hi, mid, _ = _split3(log_1m)
            after = r_ref[...] + _dot(hi, tri) + _dot(mid, tri)
            w = jnp.exp(log_1m + z + after)
            if masked:
                w = jnp.where(col < row, w, 0.0)
            acc_ref[...] += _dot(w.astype(BF16), v_ref[0, hd, pl.ds(k0, tq), :])
            r_ref[...] += jnp.sum(log_1m, axis=-1, keepdims=True)

        block(qi, True)

        def loop_body(n, carry):
            block(qi - 1 - n, False)
            return carry

        lax.fori_loop(0, qi, loop_body, 0)
        outs.append(acc_ref[...])
    o_ref[0] = jnp.concatenate(outs, axis=-1).astype(o_ref.dtype)


def _sb_attention(q, k, v, tri, tq):
    b, _, s, _ = q.shape
    return pl.pallas_call(
        _sb_body,
        out_shape=jax.ShapeDtypeStruct((b, s, GROUP_WIDTH), BF16),
        grid=(b, s // tq),
        in_specs=[
            pl.BlockSpec((1, HEADS, tq, HEAD_DIM), lambda i, j: (i, 0, j, 0)),
            pl.BlockSpec((1, HEADS, s, HEAD_DIM), lambda i, j: (i, 0, 0, 0)),
            pl.BlockSpec((1, HEADS, s, HEAD_DIM), lambda i, j: (i, 0, 0, 0)),
            pl.BlockSpec((tq, tq), lambda i, j: (0, 0)),
        ],
        out_specs=pl.BlockSpec((1, tq, GROUP_WIDTH), lambda i, j: (i, j, 0)),
        scratch_shapes=[pltpu.VMEM((tq, 1), F32), pltpu.VMEM((tq, HEAD_DIM), F32)],
        compiler_params=_cparams(("arbitrary", "arbitrary")),
        name="stick_breaking_attention",
    )(q, k, v, tri)


def _chunk_bias_table(rel_table, tq):
    left = LEFT_CHUNKS * CHUNK
    t = jnp.arange(tq)[:, None]
    j = jnp.arange(left + tq)[None, :]
    rel = j - left - t
    c0 = (t // CHUNK) * CHUNK
    visible = (j - left >= c0 - left) & (j - left < c0 + CHUNK)
    bias = rel_table[:, jnp.clip(rel, -MAX_REL, MAX_REL) + MAX_REL].astype(F32)
    return jnp.where(visible[None], bias, NEG_INF)


def _chunk_body(nb, q_ref, *refs):
    k_refs, v_refs = refs[:nb], refs[nb:2 * nb]
    bias_ref, o_ref = refs[2 * nb], refs[2 * nb + 1]
    qi = pl.program_id(1)
    outs = []
    for hd in range(HEADS):
        q = q_ref[0, hd]
        s_parts = []
        for n in range(nb):
            s = _dot_nt(q, k_refs[n][0, hd])
            s_parts.append(jnp.where(qi - (nb - 1) + n >= 0, s, NEG_INF))
        s = jnp.concatenate(s_parts, axis=-1) + bias_ref[hd]
        m = jnp.max(s, axis=-1, keepdims=True)
        p = jnp.exp(s - m)
        l = jnp.sum(p, axis=-1, keepdims=True)
        p = p.astype(BF16)
        tq = q.shape[0]
        acc = sum(_dot(p[:, n * tq:(n + 1) * tq], v_refs[n][0, hd]) for n in range(nb))
        outs.append(acc / l)
    o_ref[0] = jnp.concatenate(outs, axis=-1).astype(o_ref.dtype)


def _chunk_attention(q, k, v, bias, tq):
    b, _, s, _ = q.shape
    left = LEFT_CHUNKS * CHUNK
    assert left % tq == 0
    nb = left // tq + 1
    kv_specs = [
        pl.BlockSpec((1, HEADS, tq, HEAD_DIM), functools.partial(
            lambda i, j, n: (i, 0, jnp.maximum(j - (nb - 1) + n, 0), 0), n=n))
        for n in range(nb)
    ]
    return pl.pallas_call(
        functools.partial(_chunk_body, nb),
        out_shape=jax.ShapeDtypeStruct((b, s, GROUP_WIDTH), BF16),
        grid=(b, s // tq),
        in_specs=[pl.BlockSpec((1, HEADS, tq, HEAD_DIM), lambda i, j: (i, 0, j, 0))] + kv_specs + kv_specs
        + [pl.BlockSpec((HEADS, tq, nb * tq), lambda i, j: (0, 0, 0))],
        out_specs=pl.BlockSpec((1, tq, GROUP_WIDTH), lambda i, j: (i, j, 0)),
        compiler_params=_cparams(("arbitrary", "arbitrary")),
        name="chunked_relbias_attention",
    )(q, *([k] * nb), *([v] * nb), bias)


INT_MIN = -2 ** 31


def _lane_block_sum(x):
    return sum(x[:, c:c + LANES] for c in range(0, x.shape[1], LANES))


def _dsa_body(n_sel, q_ref, kv_ref, qi_ref, kit_ref, w_ref, tri_ref, o_ref,
              key_ref, bias_ref, m_ref, l_ref, acc_ref):
    i = pl.program_id(1)
    tq = q_ref.shape[2]
    nkb = i + 1
    row = lax.broadcasted_iota(jnp.int32, (tq, tq), 0)
    col = lax.broadcasted_iota(jnp.int32, (tq, tq), 1)
    vis_diag = col < (row // CHUNK + 1) * CHUNK
    w = w_ref[0]
    qi4 = qi_ref[0].reshape(IDX_HEADS * tq, LANES)

    def score_block(kb, masked):
        k0 = pl.multiple_of(kb * tq, tq)
        sc4 = _dot_nt(qi4, kit_ref[0, pl.ds(k0, tq), :])
        score = sum(w[:, MISC_WI + hd:MISC_WI + hd + 1] * jnp.maximum(sc4[hd * tq:(hd + 1) * tq], 0.0)
                    for hd in range(IDX_HEADS))
        score = jnp.where(score == 0.0, 0.0, score)
        bits = pltpu.bitcast(score, jnp.int32)
        key = jnp.where(bits < 0, bits ^ 0x7FFFFFFF, bits)
        if masked:
            key = jnp.where(vis_diag, key, INT_MIN)
        key_ref[kb] = key

    def score_loop(kb, carry):
        score_block(kb, False)
        return carry

    lax.fori_loop(0, i, score_loop, 0)
    score_block(i, True)

    def count(pred):
        def body(kb, acc):
            return acc + _lane_block_sum(jnp.where(pred(key_ref[kb]), 1.0, 0.0))
        acc = lax.fori_loop(0, nkb, body, jnp.zeros((tq, LANES), F32))
        return jnp.sum(acc, axis=-1, keepdims=True)

    def bit_step(it, ans_u):
        cand_u = ans_u | lax.shift_left(jnp.int32(1), 31 - it)
        cand = cand_u ^ INT_MIN
        return jnp.where(count(lambda k: k >= cand) >= n_sel, cand_u, ans_u)

    thr = lax.fori_loop(0, 32, bit_step, jnp.zeros((tq, 1), jnp.int32)) ^ INT_MIN

    n_gt = count(lambda k: k > thr)
    need = jnp.where(thr != INT_MIN, n_sel - n_gt, 0.0)
    tri = tri_ref[...]

    def select_block(kb, seen):
        key = key_ref[kb]
        eq = key == thr
        eq_f = jnp.where(eq, 1.0, 0.0)
        earlier = seen + _dot(eq_f.astype(BF16), tri)
        tie_bias = jnp.where(eq, jnp.where(earlier < need, 0.0, NEG_INF), NEG_INF)
        bias_ref[kb] = jnp.where(key > thr, 0.0, tie_bias)
        return seen + jnp.sum(eq_f, axis=-1, keepdims=True)

    lax.fori_loop(0, nkb, select_block, jnp.zeros((tq, 1), F32))

    outs = []
    for hd in range(HEADS):
        q = q_ref[0, hd]
        m_ref[...] = jnp.full_like(m_ref, NEG_INF)
        l_ref[...] = jnp.zeros_like(l_ref)
        acc_ref[...] = jnp.zeros_like(acc_ref)

        def attend(kb, carry):
            k0 = pl.multiple_of(kb * tq, tq)
            kvb = kv_ref[0, pl.ds(k0, tq), :]
            _online_softmax_step(_dot_nt(q, kvb) + bias_ref[kb], kvb, m_ref, l_ref, acc_ref)
            return carry

        lax.fori_loop(0, nkb, attend, 0)
        outs.append((acc_ref[...] / l_ref[...])[:, HEAD_DIM:])
    o_ref[0] = jnp.concatenate(outs, axis=-1).astype(o_ref.dtype)


def _dsa_attention(q, kv, qi, kit, w, tri, tq):
    b, _, s, _ = q.shape
    n_sel = min(TOPK_MAX, s // 4)
    return pl.pallas_call(
        functools.partial(_dsa_body, n_sel),
        out_shape=jax.ShapeDtypeStruct((b, s, GROUP_WIDTH), BF16),
        grid=(b, s // tq),
        in_specs=[
            pl.BlockSpec((1, HEADS, tq, LANES), lambda i, j: (i, 0, j, 0)),
            pl.BlockSpec((1, s, LANES), lambda i, j: (i, 0, 0)),
            pl.BlockSpec((1, IDX_HEADS, tq, LANES), lambda i, j: (i, 0, j, 0)),
            pl.BlockSpec((1, s, LANES), lambda i, j: (i, 0, 0)),
            pl.BlockSpec((1, tq, LANES), lambda i, j: (i, j, 0)),
            pl.BlockSpec((tq, tq), lambda i, j: (0, 0)),
        ],
        out_specs=pl.BlockSpec((1, tq, GROUP_WIDTH), lambda i, j: (i, j, 0)),
        scratch_shapes=[
            pltpu.VMEM((s // tq, tq, tq), jnp.int32), pltpu.VMEM((s // tq, tq, tq), F32),
            pltpu.VMEM((tq, 1), F32), pltpu.VMEM((tq, 1), F32), pltpu.VMEM((tq, LANES), F32),
        ],
        compiler_params=_cparams(("arbitrary", "arbitrary")),
        name="sparse_indexer_attention",
    )(q, kv, qi, kit, w, tri)


TOK_TILE = 256
ROUTE_ROWS = 2 * TOP_K


def _outstage_body(ya_ref, yb_ref, yc_ref, yd_ref, wo_ref, x_ref, g1_ref, sc_ref, sh_ref, ng_ref,
                   wr_ref, br_ref, tri_ref, xo_ref, h_ref, route_ref, gate_ref, cnt_ref, carry_ref):
    tm = x_ref.shape[0]
    gw = GROUP_WIDTH
    mix = sum(_dot(r[...], wo_ref[n * gw:(n + 1) * gw, :]) for n, r in enumerate((ya_ref, yb_ref, yc_ref, yd_ref)))
    x = x_ref[...] + g1_ref[0] * mix
    xo_ref[...] = x
    y = x * lax.rsqrt(jnp.mean(x * x, axis=-1, keepdims=True) + NORM_EPS) * ng_ref[...]
    h = y * (1.0 + sc_ref[0]) + sh_ref[0]
    _store_rows(h_ref, h)

    @pl.when(pl.program_id(0) == 0)
    def _():
        carry_ref[...] = jnp.zeros_like(carry_ref)

    logits = _dot_nt(wr_ref[...], h.astype(BF16)) + br_ref[...]
    e_iota = lax.broadcasted_iota(jnp.int32, (N_EXPERTS, tm), 0).astype(F32)
    vals, ids = [], []
    for _ in range(TOP_K):
        top = jnp.max(logits, axis=0, keepdims=True)
        idx = jnp.min(jnp.where(logits == top, e_iota, float(N_EXPERTS)), axis=0, keepdims=True)
        logits = jnp.where(e_iota == idx, -jnp.inf, logits)
        vals.append(top)
        ids.append(idx)
    exps = [jnp.exp(v - vals[0]) for v in vals]
    denom = sum(exps)
    onehot = sum(jnp.where(e_iota == idx, 1.0, 0.0) for idx in ids)
    before = carry_ref[...] + _dot(onehot.astype(BF16), tri_ref[...])
    ranks = [jnp.sum(jnp.where(e_iota == idx, before, 0.0), axis=0, keepdims=True) for idx in ids]
    carry_ref[...] += jnp.sum(onehot, axis=1, keepdims=True)
    route_ref[0] = jnp.concatenate(ids + ranks, axis=0).astype(jnp.int32)
    gate_ref[0] = jnp.concatenate([e / denom for e in exps] + [jnp.zeros_like(denom)] * TOP_K, axis=0)
    cnt_ref[...] = jnp.broadcast_to(carry_ref[...], cnt_ref.shape)


def _output_stage(ys, wo, x, g1, sc2, sh2, ng, wr_t, br, tri, tiles_per_batch):
    t, d = x.shape
    tm = TOK_TILE
    nt = t // tm
    row = lambda width: pl.BlockSpec((tm, width), lambda i: (i, 0))
    const = lambda shape: pl.BlockSpec(shape, lambda i: (0,) * len(shape))
    per_batch = pl.BlockSpec((1, 1, d), lambda i: (i // tiles_per_batch, 0, 0))
    return pl.pallas_call(
        _outstage_body,
        out_shape=(
            jax.ShapeDtypeStruct((t, d), F32), jax.ShapeDtypeStruct((t, d // LANES, LANES), F32),
            jax.ShapeDtypeStruct((nt, ROUTE_ROWS, tm), jnp.int32), jax.ShapeDtypeStruct((nt, ROUTE_ROWS, tm), F32),
            jax.ShapeDtypeStruct((N_EXPERTS, LANES), F32),
        ),
        grid=(nt,),
        in_specs=[row(GROUP_WIDTH)] * 4 + [const((d, d)), row(d), per_batch, per_batch, per_batch, const((1, d)),
                                           const((N_EXPERTS, d)), const((N_EXPERTS, 1)), const((tm, tm))],
        out_specs=(row(d), pl.BlockSpec((tm, d // LANES, LANES), lambda i: (i, 0, 0)),
                   pl.BlockSpec((1, ROUTE_ROWS, tm), lambda i: (i, 0, 0)),
                   pl.BlockSpec((1, ROUTE_ROWS, tm), lambda i: (i, 0, 0)),
                   const((N_EXPERTS, LANES))),
        scratch_shapes=[pltpu.VMEM((N_EXPERTS, 1), F32)],
        compiler_params=_cparams(("arbitrary",)),
        name="mixer_output_stage_router",
    )(*ys, wo, x, g1, sc2, sh2, ng, wr_t, br, tri)


EXPERT_TILE = 512


def _store_rows(ref, val):
    for c in range(val.shape[1] // LANES):
        ref[:, c, :] = val[:, c * LANES:(c + 1) * LANES]


def _load_rows(ref, r0=0, n=None):
    n = ref.shape[0] if n is None else n
    return jnp.concatenate([ref[r0:r0 + n, c, :] for c in range(ref.shape[1])], axis=-1)


def _sorted_row(route_smem, pstart_ref, tm, r):
    return pstart_ref[route_smem[r]] + route_smem[TOP_K * tm + r]


def _dispatch_body(pstart_ref, pend_ref, route_hbm, h_ref, xs_hbm, route_smem, zero_ref, rsem, zsem, sem):
    tm = h_ref.shape[0]
    i = pl.program_id(0)

    @pl.when(i == 0)
    def _():
        zero_ref[...] = jnp.zeros_like(zero_ref)

        def fill(e):
            return pltpu.make_async_copy(zero_ref, xs_hbm.at[pl.ds(pl.multiple_of(pend_ref[e] - EXPERT_TILE, EXPERT_TILE), EXPERT_TILE)], zsem)

        def start(e, c):
            @pl.when(pend_ref[e] > pstart_ref[e])
            def _():
                fill(e).start()
            return c

        def wait(e, c):
            @pl.when(pend_ref[e] > pstart_ref[e])
            def _():
                fill(e).wait()
            return c

        lax.fori_loop(0, N_EXPERTS, start, 0)
        lax.fori_loop(0, N_EXPERTS, wait, 0)

    cp = pltpu.make_async_copy(route_hbm.at[i], route_smem, rsem)
    cp.start()
    cp.wait()

    def row_copy(r):
        t = r & (tm - 1)
        dst = _sorted_row(route_smem, pstart_ref, tm, r)
        return pltpu.make_async_copy(h_ref.at[t], xs_hbm.at[dst], sem)

    def start(r, c):
        row_copy(r).start()
        return c

    def wait(r, c):
        row_copy(r).wait()
        return c

    lax.fori_loop(0, TOP_K * tm, start, 0)
    lax.fori_loop(0, TOP_K * tm, wait, 0)


def _dispatch(pstart, pend, route, h, n_rows):
    t, dc, _ = h.shape
    tm = TOK_TILE
    return pl.pallas_call(
        _dispatch_body,
        out_shape=jax.ShapeDtypeStruct((n_rows, dc, LANES), F32),
        grid_spec=pltpu.PrefetchScalarGridSpec(
            num_scalar_prefetch=2,
            grid=(t // tm,),
            in_specs=[pl.BlockSpec(memory_space=pl.ANY), pl.BlockSpec((tm, dc, LANES), lambda i, ps, pe: (i, 0, 0))],
            out_specs=pl.BlockSpec(memory_space=pl.ANY),
            scratch_shapes=[
                pltpu.SMEM((ROUTE_ROWS * tm,), jnp.int32), pltpu.VMEM((EXPERT_TILE, dc, LANES), F32),
                pltpu.SemaphoreType.DMA, pltpu.SemaphoreType.DMA, pltpu.SemaphoreType.DMA,
            ],
        ),
        compiler_params=_cparams(("arbitrary",)),
        name="moe_dispatch",
    )(pstart, pend, route, h)


def _expert_body(tile_e_ref, n_used_ref, x_ref, wgu_ref, bgu_ref, wdn_ref, bdn_ref, y_ref):
    @pl.when(pl.program_id(0) < n_used_ref[0])
    def _():
        f = wdn_ref.shape[1]
        gu = _dot(_load_rows(x_ref).astype(BF16), wgu_ref[0]) + bgu_ref[0]
        glu = jnp.minimum(gu[:, :f], SWIGLU_LIMIT)
        lin = jnp.clip(gu[:, f:], -SWIGLU_LIMIT, SWIGLU_LIMIT)
        act = glu * jax.nn.sigmoid(SWIGLU_ALPHA * glu) * (lin + 1.0)
        _store_rows(y_ref, _dot(act.astype(BF16), wdn_ref[0]) + bdn_ref[0])


def _experts(tile_e, n_used, xs, wgu, bgu, wdn, bdn):
    p, dc, _ = xs.shape
    d = dc * LANES
    f = wdn.shape[1]
    tm = EXPERT_TILE
    used = lambda i, te, nu: jnp.minimum(i, nu[0] - 1)
    return pl.pallas_call(
        _expert_body,
        out_shape=jax.ShapeDtypeStruct((p, dc, LANES), F32),
        grid_spec=pltpu.PrefetchScalarGridSpec(
            num_scalar_prefetch=2,
            grid=(p // tm,),
            in_specs=[
                pl.BlockSpec((tm, dc, LANES), lambda i, te, nu: (used(i, te, nu), 0, 0)),
                pl.BlockSpec((1, d, 2 * f), lambda i, te, nu: (te[i], 0, 0)),
                pl.BlockSpec((1, 1, 2 * f), lambda i, te, nu: (te[i], 0, 0)),
                pl.BlockSpec((1, f, d), lambda i, te, nu: (te[i], 0, 0)),
                pl.BlockSpec((1, 1, d), lambda i, te, nu: (te[i], 0, 0)),
            ],
            out_specs=pl.BlockSpec((tm, dc, LANES), lambda i, te, nu: (used(i, te, nu), 0, 0)),
        ),
        compiler_params=_cparams(("arbitrary",)),
        name="moe_experts",
    )(tile_e, n_used, xs, wgu, bgu, wdn, bdn)


def _combine_body(final, pstart_ref, route_hbm, ys_hbm, x_ref, g2_ref, gate_ref, fg_ref, o_ref,
                  route_smem, buf_ref, rsem, sem):
    tm = x_ref.shape[0]
    i = pl.program_id(0)
    cp = pltpu.make_async_copy(route_hbm.at[i], route_smem, rsem)
    cp.start()
    cp.wait()

    def row_copy(r):
        src = _sorted_row(route_smem, pstart_ref, tm, r)
        return pltpu.make_async_copy(ys_hbm.at[src], buf_ref.at[r], sem)

    def start(r, c):
        row_copy(r).start()
        return c

    def wait(r, c):
        row_copy(r).wait()
        return c

    lax.fori_loop(0, TOP_K * tm, start, 0)
    lax.fori_loop(0, TOP_K * tm, wait, 0)
    gates = gate_ref[...]
    moe = sum(gates[:, k:k + 1] * _load_rows(buf_ref, k * tm, tm) for k in range(TOP_K))
    x = x_ref[...] + g2_ref[0] * moe
    if final:
        x = x * lax.rsqrt(jnp.mean(x * x, axis=-1, keepdims=True) + NORM_EPS) * fg_ref[...]
    o_ref[...] = x


def _combine(pstart, route, ys, x, g2, gates_col, final_g, tiles_per_batch, final):
    t, d = x.shape
    tm = TOK_TILE
    return pl.pallas_call(
        functools.partial(_combine_body, final),
        out_shape=jax.ShapeDtypeStruct((t, d), F32),
        grid_spec=pltpu.PrefetchScalarGridSpec(
            num_scalar_prefetch=1,
            grid=(t // tm,),
            in_specs=[
                pl.BlockSpec(memory_space=pl.ANY), pl.BlockSpec(memory_space=pl.ANY),
                pl.BlockSpec((tm, d), lambda i, ps: (i, 0)),
                pl.BlockSpec((1, 1, d), lambda i, ps: (i // tiles_per_batch, 0, 0)),
                pl.BlockSpec((tm, ROUTE_ROWS), lambda i, ps: (i, 0)),
                pl.BlockSpec((1, d), lambda i, ps: (0, 0)),
            ],
            out_specs=pl.BlockSpec((tm, d), lambda i, ps: (i, 0)),
            scratch_shapes=[
                pltpu.SMEM((ROUTE_ROWS * tm,), jnp.int32), pltpu.VMEM((TOP_K * tm, d // LANES, LANES), F32),
                pltpu.SemaphoreType.DMA, pltpu.SemaphoreType.DMA,
            ],
        ),
        compiler_params=_cparams(("arbitrary",)),
        name="moe_combine",
    )(pstart, route, ys, x, g2, gates_col, final_g)


def _moe_plan(counts):
    counts = counts.astype(jnp.int32)
    tiles = (counts + EXPERT_TILE - 1) // EXPERT_TILE
    tile_end = jnp.cumsum(tiles)
    pend = tile_end * EXPERT_TILE
    pstart = pend - tiles * EXPERT_TILE
    return pstart, pend, tile_end


ATT_TILE = 256


def _constants(s):
    r = jnp.arange(ATT_TILE)[:, None]
    c = jnp.arange(ATT_TILE)[None, :]
    return dict(
        tab=_rope_tables(s),
        tri_le=(c <= r).astype(BF16),
        tri_gt=(r > c).astype(BF16),
        tri_lt=(r < c).astype(BF16),
    )


def _prep_layer(p, l):
    bf = jnp.zeros((1, LANES), F32).at[0, MISC_FA:MISC_FA + HEADS].set(p["b_forget"][l].astype(F32))
    return dict(
        w_in=_reorder_w_in(p["w_in"][l]),
        wkv=jnp.concatenate([p["w_uk"][l], p["w_uv"][l]], axis=-1).astype(BF16),
        kvg=p["kv_norm_g"][l].reshape(1, KV_RANK).astype(F32),
        bf=bf,
        g1=p["norm1_g"][l].reshape(1, -1).astype(F32),
        chunk_bias=_chunk_bias_table(p["rel_bias"][l], ATT_TILE),
    )


def _mixers(x, sc1, sh1, lp, cst):
    tq = ATT_TILE
    (fq, fk, fv, dq, dkv, dqi, dkit, dw, cq, ck, cv, eq, ek, ev) = _input_stage(
        x, sc1, sh1, lp["g1"], lp["w_in"], lp["wkv"], lp["kvg"], lp["bf"], cst["tri_le"], cst["tab"], tq)
    ya = _fox_attention(fq, fk, fv, tq)
    yb = _dsa_attention(dq, dkv, dqi, dkit, dw, cst["tri_lt"], tq)
    yc = _sb_attention(cq, ck, cv, cst["tri_gt"], tq)
    yd = _chunk_attention(eq, ek, ev, lp["chunk_bias"], tq)
    return ya, yb, yc, yd


def _moe(p, l, x, h, route, gate, counts, g2, tiles_per_batch, final):
    t, d = x.shape
    nt = t // TOK_TILE
    n_tiles = t * TOP_K // EXPERT_TILE + N_EXPERTS
    pstart, pend, tile_end = _moe_plan(counts)
    n_used = tile_end[-1:]
    tile_e = jnp.searchsorted(tile_end, jnp.minimum(jnp.arange(n_tiles), n_used - 1), side="right").astype(jnp.int32)
    route = route.reshape(nt, ROUTE_ROWS * TOK_TILE)
    xs = _dispatch(pstart, pend, route, h, n_tiles * EXPERT_TILE)
    ys = _experts(tile_e, n_used, xs,
                  p["w_gu"][l].astype(BF16), p["b_gu"][l][:, None, :].astype(F32),
                  p["w_dn"][l].astype(BF16), p["b_dn"][l][:, None, :].astype(F32))
    gates_col = gate.transpose(0, 2, 1).reshape(t, ROUTE_ROWS)
    return _combine(pstart, route, ys, x, g2, gates_col, p["final_g"].reshape(1, d).astype(F32),
                    tiles_per_batch, final)


def kernel(x, c, w_ada, b_ada, norm1_g, norm2_g, w_in, b_forget, kv_norm_g, w_uk, w_uv, rel_bias, w_o,
           w_router, b_router, w_gu, b_gu, w_dn, b_dn, final_g):
    p = dict(w_in=w_in, b_forget=b_forget, kv_norm_g=kv_norm_g, w_uk=w_uk, w_uv=w_uv, rel_bias=rel_bias,
             norm1_g=norm1_g, w_gu=w_gu, b_gu=b_gu, w_dn=w_dn, b_dn=b_dn, final_g=final_g)
    b, s, d = x.shape
    t = b * s
    depth = w_ada.shape[0]
    tiles_per_batch = s // TOK_TILE
    mod = _modulation(c, w_ada, b_ada)
    cst = _constants(s)
    for l in range(depth):
        sh1, sc1, g1, sh2, sc2, g2 = [m[:, None, :] for m in jnp.split(mod[l], 6, axis=-1)]
        ys = _mixers(x, sc1, sh1, _prep_layer(p, l), cst)
        x2, h, route, gate, cnt = _output_stage(
            [y.reshape(t, GROUP_WIDTH) for y in ys], w_o[l].astype(BF16), x.reshape(t, d), g1, sc2, sh2,
            norm2_g[l].reshape(1, d).astype(F32), w_router[l].T.astype(BF16),
            b_router[l].reshape(N_EXPERTS, 1).astype(F32), cst["tri_lt"], tiles_per_batch)
        x = _moe(p, l, x2, h, route, gate, cnt[:, 0], g2, tiles_per_batch, l == depth - 1).reshape(b, s, d)
    return x
```

```python
import functools
import math

import jax
import jax.numpy as jnp
from jax import lax
from jax.experimental import pallas as pl
from jax.experimental.pallas import tpu as pltpu

F32 = jnp.float32
BF16 = jnp.bfloat16

HEAD_DIM = 64
HEADS = 4
GROUP_WIDTH = HEADS * HEAD_DIM
CHUNK = 64
ROPE_THETA = 10000.0
NORM_EPS = 1e-6
NEG_INF = -1e30
KV_RANK = 128
IDX_HEADS = 4
IDX_DIM = 32
TOPK_MAX = 256
LEFT_CHUNKS = 8
MAX_REL = 128
N_EXPERTS = 32
TOP_K = 4
SWIGLU_ALPHA = 1.702
SWIGLU_LIMIT = 7.0
LOG2E = math.log2(math.e)
INT_MIN = -2 ** 31

LANES = 128
SUBLANES = 8
VMEM_LIMIT = 48 * 1024 * 1024

ATT_TILE = 256
TOK_TILE = 512
EXPERT_TILE = 512
ROUTE_ROWS = 2 * TOP_K
DMA_UNROLL = 8

COL_A = 0
COL_C = 3 * GROUP_WIDTH
COL_D = 6 * GROUP_WIDTH
COL_QB = 9 * GROUP_WIDTH
COL_LAT = COL_QB + GROUP_WIDTH
COL_QI = COL_LAT + KV_RANK
COL_MISC = COL_QI + IDX_HEADS * IDX_DIM
N_COLS = COL_MISC + LANES
MISC_WI = IDX_DIM
MISC_FA = IDX_DIM + IDX_HEADS
ROW_VA = 0
ROW_VC = GROUP_WIDTH
ROW_WI = 2 * GROUP_WIDTH
N_ROWS_T = ROW_WI + SUBLANES


def _cparams(sem):
    return pltpu.CompilerParams(dimension_semantics=sem, vmem_limit_bytes=VMEM_LIMIT)


def _split3(x):
    hi = x.astype(BF16)
    r = x - hi.astype(F32)
    mid = r.astype(BF16)
    lo = (r - mid.astype(F32)).astype(BF16)
    return hi, mid, lo


def _dot(a, b):
    return jnp.dot(a, b, preferred_element_type=F32)


def _dot_nt(a, b):
    return lax.dot_general(a, b, (((1,), (1,)), ((), ())), preferred_element_type=F32)


def _mod_body(c_ref, w_ref, b_ref, o_ref):
    c = c_ref[...]
    act = (c * jax.nn.sigmoid(c)).astype(BF16)
    o_ref[0] = _dot(act, w_ref[0].astype(BF16)) + b_ref[0]


def _modulation(c, w_ada, b_ada):
    depth, d, n = w_ada.shape
    b = c.shape[0]
    tn = 1024
    return pl.pallas_call(
        _mod_body,
        out_shape=jax.ShapeDtypeStruct((depth, b, n), F32),
        grid=(depth, n // tn),
        in_specs=[
            pl.BlockSpec((b, d), lambda l, j: (0, 0)),
            pl.BlockSpec((1, d, tn), lambda l, j: (l, 0, j)),
            pl.BlockSpec((1, 1, tn), lambda l, j: (l, 0, j)),
        ],
        out_specs=pl.BlockSpec((1, b, tn), lambda l, j: (l, 0, j)),
        compiler_params=_cparams(("arbitrary", "arbitrary")),
        name="adaln_mod",
    )(c, w_ada, b_ada.reshape(depth, 1, n))


def _rope_tables(s):
    pos = jnp.arange(s, dtype=F32)[:, None]
    lane = jnp.arange(LANES)[None, :]

    def tables(dim, active):
        half = dim // 2
        j = lane % dim
        inv = ROPE_THETA ** (-(j % half).astype(F32) / half)
        ang = pos * inv
        cos = jnp.where(active, jnp.cos(ang), 1.0)
        sin = jnp.where(active, jnp.sin(ang), 0.0)
        first = j < half
        return [cos, jnp.where(first, -sin, 0.0), jnp.where(first, 0.0, sin)]

    everywhere = lane >= 0
    t = (tables(HEAD_DIM, everywhere) + tables(HEAD_DIM, lane < HEAD_DIM)
         + tables(IDX_DIM, everywhere) + tables(IDX_DIM, lane < IDX_DIM))
    return jnp.stack(t).astype(F32)


def _rope(x, tab, k, half):
    cos, sin_a, sin_b = tab[3 * k], tab[3 * k + 1], tab[3 * k + 2]
    return x * cos + pltpu.roll(x, LANES - half, 1) * sin_a + pltpu.roll(x, half, 1) * sin_b


def _instage_body(x_ref, sc_ref, sh_ref, g_ref, w_ref, wt_ref, wkv_ref, wuvt_ref, kvg_ref, bf_ref, tri_ref, tab_ref,
                  fq_o, fk_o, fvt_o, dq_o, dkv_o, dvt_o, dqi_o, dkit_o, dwt_o,
                  cq_o, ck_o, cvt_o, eq_o, ek_o, ev_o, carry_ref):
    tm = x_ref.shape[1]
    x = x_ref[0]
    y = x * lax.rsqrt(jnp.mean(x * x, axis=-1, keepdims=True) + NORM_EPS) * g_ref[...]
    h = (y * (1.0 + sc_ref[0]) + sh_ref[0]).astype(BF16)

    def proj(c0, n):
        return _dot(h, w_ref[:, c0:c0 + n])

    def proj_t(r0, n):
        return _dot_nt(wt_ref[r0:r0 + n, :], h)

    def heads_out(val, refs, scale):
        for i, ref in enumerate(refs):
            for hd in range(HEADS):
                c0 = i * GROUP_WIDTH + hd * HEAD_DIM
                piece = val[:, c0:c0 + HEAD_DIM]
                if i == 0:
                    piece = piece * scale
                ref[0, hd] = piece.astype(ref.dtype)

    def heads_out_t(val, ref):
        for hd in range(HEADS):
            ref[0, hd] = val[hd * HEAD_DIM:(hd + 1) * HEAD_DIM, :].astype(ref.dtype)

    scale = HEAD_DIM ** -0.5
    heads_out(proj(COL_C, 2 * GROUP_WIDTH), (cq_o, ck_o), scale)
    heads_out_t(proj_t(ROW_VC, GROUP_WIDTH), cvt_o)
    heads_out(proj(COL_D, 3 * GROUP_WIDTH), (eq_o, ek_o, ev_o), scale)

    misc = _rope(proj(COL_MISC, LANES), tab_ref, 3, IDX_DIM // 2)
    lane = lax.broadcasted_iota(jnp.int32, (tm, LANES), 1)
    ki = jnp.where(lane < IDX_DIM, misc, 0.0)
    kit = ki + pltpu.roll(ki, IDX_DIM, 1) + pltpu.roll(ki, 2 * IDX_DIM, 1) + pltpu.roll(ki, 3 * IDX_DIM, 1)
    dkit_o[0] = kit.astype(BF16)
    dwt_o[0] = proj_t(ROW_WI, SUBLANES) * ((IDX_HEADS ** -0.5) * (IDX_DIM ** -0.5))
    qi = _rope(proj(COL_QI, LANES), tab_ref, 2, IDX_DIM // 2)
    for hd in range(IDX_HEADS):
        in_head = (lane >= hd * IDX_DIM) & (lane < (hd + 1) * IDX_DIM)
        dqi_o[0, hd] = jnp.where(in_head, qi, 0.0).astype(BF16)
    lat = proj(COL_LAT, KV_RANK)
    lat = (lat * lax.rsqrt(jnp.mean(lat * lat, axis=-1, keepdims=True) + NORM_EPS) * kvg_ref[...]).astype(BF16)
    dkv_o[0] = _rope(_dot(lat, wkv_ref[...]), tab_ref, 1, HEAD_DIM // 2).astype(BF16)
    dvt_o[0] = _dot_nt(wuvt_ref[...], lat).astype(BF16)
    qb = proj(COL_QB, GROUP_WIDTH)
    zeros64 = jnp.zeros((tm, HEAD_DIM), F32)
    for half_i in range(2):
        r = _rope(qb[:, half_i * LANES:(half_i + 1) * LANES], tab_ref, 0, HEAD_DIM // 2) * (scale * LOG2E)
        for j in range(2):
            piece = jnp.concatenate([r[:, j * HEAD_DIM:(j + 1) * HEAD_DIM], zeros64], axis=-1)
            dq_o[0, 2 * half_i + j] = piece.astype(BF16)

    @pl.when(pl.program_id(1) == 0)
    def _():
        carry_ref[...] = jnp.zeros_like(carry_ref)

    z = misc + bf_ref[...]
    log_f = jnp.minimum(z, 0.0) - jnp.log1p(jnp.exp(-jnp.abs(z)))
    tri = tri_ref[...]
    f_cum = carry_ref[...] + sum(_dot(tri, p) for p in _split3(log_f))
    carry_ref[...] = f_cum[tm - 1:tm, :]
    f_cum = f_cum * LOG2E
    pa = proj(COL_A, 2 * GROUP_WIDTH)
    heads_out_t(proj_t(ROW_VA, GROUP_WIDTH), fvt_o)
    lane64 = lax.broadcasted_iota(jnp.int32, (tm, HEAD_DIM), 1)
    for hd in range(HEADS):
        fh = f_cum[:, MISC_FA + hd:MISC_FA + hd + 1]
        f3 = [jnp.broadcast_to(p.astype(F32), (tm, HEAD_DIM)) for p in _split3(fh)]
        one = jnp.where(lane64 < 6, 1.0, 0.0)
        q_ext = jnp.where(lane64 == 0, f3[0], jnp.where(lane64 == 1, f3[1], jnp.where(lane64 == 2, f3[2], one)))
        k_ext = jnp.where(lane64 == 3, -f3[0], jnp.where(lane64 == 4, -f3[1], jnp.where(lane64 == 5, -f3[2], one)))
        q = pa[:, hd * HEAD_DIM:(hd + 1) * HEAD_DIM] * (scale * LOG2E)
        k = pa[:, GROUP_WIDTH + hd * HEAD_DIM:GROUP_WIDTH + (hd + 1) * HEAD_DIM]
        fq_o[0, hd] = jnp.concatenate([q, q_ext], axis=-1).astype(BF16)
        fk_o[0, hd] = jnp.concatenate([k, k_ext], axis=-1).astype(BF16)


def _reorder_w_in(w_in):
    gw = GROUP_WIDTH
    a0 = 0
    b0 = 3 * gw + HEADS
    c0 = b0 + gw + KV_RANK + IDX_HEADS * IDX_DIM + IDX_DIM + IDX_HEADS
    d0 = c0 + 3 * gw
    lat0 = b0 + gw
    qi0 = lat0 + KV_RANK
    ki0 = qi0 + IDX_HEADS * IDX_DIM
    wi0 = ki0 + IDX_DIM
    pad = jnp.zeros(w_in.shape[:-1] + (LANES - IDX_DIM - IDX_HEADS - HEADS,), w_in.dtype)
    w = jnp.concatenate([
        w_in[:, a0:a0 + 3 * gw], w_in[:, c0:c0 + 3 * gw], w_in[:, d0:d0 + 3 * gw],
        w_in[:, b0:b0 + gw], w_in[:, lat0:lat0 + KV_RANK], w_in[:, qi0:qi0 + IDX_HEADS * IDX_DIM],
        w_in[:, ki0:ki0 + IDX_DIM], w_in[:, wi0:wi0 + IDX_HEADS], w_in[:, 3 * gw:3 * gw + HEADS], pad,
    ], axis=-1)
    wt = jnp.concatenate([
        w_in[:, a0 + 2 * gw:a0 + 3 * gw], w_in[:, c0 + 2 * gw:c0 + 3 * gw], w_in[:, wi0:wi0 + IDX_HEADS],
        jnp.zeros((w_in.shape[0], SUBLANES - IDX_HEADS), w_in.dtype),
    ], axis=-1).T
    return w.astype(BF16), wt.astype(BF16)


def _input_stage(x, sc, sh, g, w, wt, wkv, wuvt, kvg, bf, tri, tab, tm):
    b, s, d = x.shape
    hq = lambda width: jax.ShapeDtypeStruct((b, HEADS, s, width), BF16)
    hqt = jax.ShapeDtypeStruct((b, HEADS, HEAD_DIM, s), BF16)
    flat = jax.ShapeDtypeStruct((b, s, LANES), BF16)
    out_shape = (
        hq(LANES), hq(LANES), hqt,
        hq(LANES), flat, jax.ShapeDtypeStruct((b, HEAD_DIM, s), BF16),
        hq(LANES), flat, jax.ShapeDtypeStruct((b, SUBLANES, s), F32),
        hq(HEAD_DIM), hq(HEAD_DIM), hqt,
        hq(HEAD_DIM), hq(HEAD_DIM), hq(HEAD_DIM),
    )
    hspec = lambda width: pl.BlockSpec((1, HEADS, tm, width), lambda i, j: (i, 0, j, 0))
    htspec = pl.BlockSpec((1, HEADS, HEAD_DIM, tm), lambda i, j: (i, 0, 0, j))
    fspec = pl.BlockSpec((1, tm, LANES), lambda i, j: (i, j, 0))
    tspec = lambda rows: pl.BlockSpec((1, rows, tm), lambda i, j: (i, 0, j))
    const2 = lambda shape: pl.BlockSpec(shape, lambda i, j: (0, 0))
    out_specs = (
        hspec(LANES), hspec(LANES), htspec,
        hspec(LANES), fspec, tspec(HEAD_DIM),
        hspec(LANES), fspec, tspec(SUBLANES),
        hspec(HEAD_DIM), hspec(HEAD_DIM), htspec,
        hspec(HEAD_DIM), hspec(HEAD_DIM), hspec(HEAD_DIM),
    )
    return pl.pallas_call(
        _instage_body,
        out_shape=out_shape,
        grid=(b, s // tm),
        in_specs=[
            pl.BlockSpec((1, tm, d), lambda i, j: (i, j, 0)),
            pl.BlockSpec((1, 1, d), lambda i, j: (i, 0, 0)),
            pl.BlockSpec((1, 1, d), lambda i, j: (i, 0, 0)),
            const2((1, d)),
            const2((d, N_COLS)),
            const2((N_ROWS_T, d)),
            const2((KV_RANK, LANES)),
            const2((HEAD_DIM, KV_RANK)),
            const2((1, KV_RANK)),
            const2((1, LANES)),
            const2((tm, tm)),
            pl.BlockSpec((12, tm, LANES), lambda i, j: (0, j, 0)),
        ],
        out_specs=out_specs,
        scratch_shapes=[pltpu.VMEM((1, LANES), F32)],
        compiler_params=_cparams(("arbitrary", "arbitrary")),
        name="mixer_input_stage",
    )(x, sc, sh, g, w, wt, wkv, wuvt, kvg, bf, tri, tab)


def _key_query_iota(t):
    return lax.broadcasted_iota(jnp.int32, (t, t), 0), lax.broadcasted_iota(jnp.int32, (t, t), 1)


def _softmax2_step(s, vt, m, l, acc):
    m_new = jnp.maximum(m, jnp.max(s, axis=0, keepdims=True))
    alpha = jnp.exp2(m - m_new)
    p = jnp.exp2(s - m_new)
    l_new = alpha * l + jnp.sum(p, axis=0, keepdims=True)
    acc_new = alpha * acc + _dot(vt, p.astype(BF16))
    return m_new, l_new, acc_new


def _skewed(n, stages):
    vals = [None] * n
    for step in range(n + len(stages) - 1):
        for j, stage in enumerate(stages):
            i = step - j
            if 0 <= i < n:
                vals[i] = stage(i, vals[i])


def _heads_to_rows(accs):
    return jnp.concatenate([a.T for a in accs], axis=-1)


def _fox_body(q_ref, k_ref, vt_ref, o_ref, m_ref, l_ref, acc_ref):
    qi = pl.program_id(1)
    tq = q_ref.shape[2]
    key, qry = _key_query_iota(tq)
    m_ref[...] = jnp.full_like(m_ref, NEG_INF)
    l_ref[...] = jnp.zeros_like(l_ref)
    acc_ref[...] = jnp.zeros_like(acc_ref)

    def block(kb, masked):
        k0 = pl.multiple_of(kb * tq, tq)

        def scores(hd, _):
            s = _dot_nt(k_ref[0, hd, pl.ds(k0, tq), :], q_ref[0, hd])
            return jnp.where(key <= qry, s, NEG_INF) if masked else s

        def update(hd, s):
            m_ref[hd], l_ref[hd], acc_ref[hd] = _softmax2_step(
                s, vt_ref[0, hd, :, pl.ds(k0, tq)], m_ref[hd], l_ref[hd], acc_ref[hd])

        _skewed(HEADS, [scores, update])

    def loop_body(kb, carry):
        block(kb, False)
        return carry

    lax.fori_loop(0, qi, loop_body, 0)
    block(qi, True)
    o_ref[0] = _heads_to_rows([acc_ref[hd] / l_ref[hd] for hd in range(HEADS)]).astype(o_ref.dtype)


def _fox_attention(q, k, vt, tq):
    b, _, s, _ = q.shape
    return pl.pallas_call(
        _fox_body,
        out_shape=jax.ShapeDtypeStruct((b, s, GROUP_WIDTH), BF16),
        grid=(b, s // tq),
        in_specs=[
            pl.BlockSpec((1, HEADS, tq, LANES), lambda i, j: (i, 0, j, 0)),
            pl.BlockSpec((1, HEADS, s, LANES), lambda i, j: (i, 0, 0, 0)),
            pl.BlockSpec((1, HEADS, HEAD_DIM, s), lambda i, j: (i, 0, 0, 0)),
        ],
        out_specs=pl.BlockSpec((1, tq, GROUP_WIDTH), lambda i, j: (i, j, 0)),
        scratch_shapes=[pltpu.VMEM((HEADS, 1, tq), F32), pltpu.VMEM((HEADS, 1, tq), F32),
                        pltpu.VMEM((HEADS, HEAD_DIM, tq), F32)],
        compiler_params=_cparams(("arbitrary", "arbitrary")),
        name="forgetting_attention",
    )(q, k, vt)


def _sb_body(q_ref, k_ref, vt_ref, tri_ref, o_ref, r_ref, acc_ref):
    qi = pl.program_id(1)
    tq = q_ref.shape[2]
    key, qry = _key_query_iota(tq)
    tri = tri_ref[...]
    r_ref[...] = jnp.zeros_like(r_ref)
    acc_ref[...] = jnp.zeros_like(acc_ref)

    def block(kb, masked):
        k0 = pl.multiple_of(kb * tq, tq)

        def logits(hd, _):
            return _dot_nt(k_ref[0, hd, pl.ds(k0, tq), :], q_ref[0, hd])

        def later_sums(hd, z):
            log_1m = -(jnp.maximum(z, 0.0) + jnp.log1p(jnp.exp(-jnp.abs(z))))
            if masked:
                log_1m = jnp.where(key < qry, log_1m, 0.0)
            hi, mid, _ = _split3(log_1m)
            after = r_ref[hd] + _dot(tri, hi) + _dot(tri, mid)
            return z, log_1m, after

        def accumulate(hd, val):
            z, log_1m, after = val
            w = jnp.exp(log_1m + z + after)
            if masked:
                w = jnp.where(key < qry, w, 0.0)
            acc_ref[hd] += _dot(vt_ref[0, hd, :, pl.ds(k0, tq)], w.astype(BF16))
            r_ref[hd] += jnp.sum(log_1m, axis=0, keepdims=True)

        _skewed(HEADS, [logits, later_sums, accumulate])

    block(qi, True)

    def loop_body(n, carry):
        block(qi - 1 - n, False)
        return carry

    lax.fori_loop(0, qi, loop_body, 0)
    o_ref[0] = _heads_to_rows([acc_ref[hd] for hd in range(HEADS)]).astype(o_ref.dtype)


def _sb_attention(q, k, vt, tri, tq):
    b, _, s, _ = q.shape
    return pl.pallas_call(
        _sb_body,
        out_shape=jax.ShapeDtypeStruct((b, s, GROUP_WIDTH), BF16),
        grid=(b, s // tq),
        in_specs=[
            pl.BlockSpec((1, HEADS, tq, HEAD_DIM), lambda i, j: (i, 0, j, 0)),
            pl.BlockSpec((1, HEADS, s, HEAD_DIM), lambda i, j: (i, 0, 0, 0)),
            pl.BlockSpec((1, HEADS, HEAD_DIM, s), lambda i, j: (i, 0, 0, 0)),
            pl.BlockSpec((tq, tq), lambda i, j: (0, 0)),
        ],
        out_specs=pl.BlockSpec((1, tq, GROUP_WIDTH), lambda i, j: (i, j, 0)),
        scratch_shapes=[pltpu.VMEM((HEADS, 1, tq), F32), pltpu.VMEM((HEADS, HEAD_DIM, tq), F32)],
        compiler_params=_cparams(("arbitrary", "arbitrary")),
        name="stick_breaking_attention",
    )(q, k, vt, tri)


def _chunk_bias_table(rel_table, tq):
    left = LEFT_CHUNKS * CHUNK
    width = left + tq
    d = jnp.arange(width + tq - 1) - (tq - 1) - left
    diag = rel_table[:, jnp.clip(d, -MAX_REL, MAX_REL) + MAX_REL].astype(F32)
    bias = jnp.stack([diag[:, tq - 1 - t:tq - 1 - t + width] for t in range(tq)], axis=1)
    t = jnp.arange(tq)[:, None]
    j = jnp.arange(width)[None, :]
    c0 = (t // CHUNK) * CHUNK
    visible = (j >= c0) & (j - left < c0 + CHUNK)
    return jnp.where(visible[None], bias, NEG_INF)


def _chunk_body(nb, q_ref, *refs):
    k_refs, v_refs = refs[:nb], refs[nb:2 * nb]
    bias_ref, o_ref = refs[2 * nb], refs[2 * nb + 1]
    qi = pl.program_id(1)
    outs = []
    for hd in range(HEADS):
        q = q_ref[0, hd]
        s_parts = []
        for n in range(nb):
            s = _dot_nt(q, k_refs[n][0, hd])
            s_parts.append(jnp.where(qi - (nb - 1) + n >= 0, s, NEG_INF))
        s = jnp.concatenate(s_parts, axis=-1) + bias_ref[hd]
        m = jnp.max(s, axis=-1, keepdims=True)
        p = jnp.exp(s - m)
        l = jnp.sum(p, axis=-1, keepdims=True)
        p = p.astype(BF16)
        tq = q.shape[0]
        acc = sum(_dot(p[:, n * tq:(n + 1) * tq], v_refs[n][0, hd]) for n in range(nb))
        outs.append(acc / l)
    o_ref[0] = jnp.concatenate(outs, axis=-1).astype(o_ref.dtype)


def _chunk_attention(q, k, v, bias, tq):
    b, _, s, _ = q.shape
    left = LEFT_CHUNKS * CHUNK
    assert left % tq == 0
    nb = left // tq + 1
    kv_specs = [
        pl.BlockSpec((1, HEADS, tq, HEAD_DIM), functools.partial(
            lambda i, j, n: (i, 0, jnp.maximum(j - (nb - 1) + n, 0), 0), n=n))
        for n in range(nb)
    ]
    return pl.pallas_call(
        functools.partial(_chunk_body, nb),
        out_shape=jax.ShapeDtypeStruct((b, s, GROUP_WIDTH), BF16),
        grid=(b, s // tq),
        in_specs=[pl.BlockSpec((1, HEADS, tq, HEAD_DIM), lambda i, j: (i, 0, j, 0))] + kv_specs + kv_specs
        + [pl.BlockSpec((HEADS, tq, nb * tq), lambda i, j: (0, 0, 0))],
        out_specs=pl.BlockSpec((1, tq, GROUP_WIDTH), lambda i, j: (i, j, 0)),
        compiler_params=_cparams(("arbitrary", "arbitrary")),
        name="chunked_relbias_attention",
    )(q, *([k] * nb), *([v] * nb), bias)


def _dsa_body(n_sel, q_ref, kv_ref, vt_ref, qi_ref, kit_ref, wt_ref, tri_ref, o_ref,
              key_ref, bias_ref, m_ref, l_ref, acc_ref):
    i = pl.program_id(1)
    tq = q_ref.shape[2]
    nkb = i + 1
    key_pos, qry_pos = _key_query_iota(tq)
    vis_diag = key_pos < (qry_pos // CHUNK + 1) * CHUNK
    wt = wt_ref[0]

    def score_block(kb, masked):
        k0 = pl.multiple_of(kb * tq, tq)
        kit = kit_ref[0, pl.ds(k0, tq), :]
        dots = [_dot_nt(kit, qi_ref[0, hd]) for hd in range(IDX_HEADS)]
        score = sum(wt[hd:hd + 1, :] * jnp.maximum(dots[hd], 0.0) for hd in range(IDX_HEADS))
        score = jnp.where(score == 0.0, 0.0, score)
        bits = pltpu.bitcast(score, jnp.int32)
        okey = jnp.where(bits < 0, bits ^ 0x7FFFFFFF, bits)
        if masked:
            okey = jnp.where(vis_diag, okey, INT_MIN)
        key_ref[kb] = okey

    def score_loop(kb, carry):
        score_block(kb, False)
        return carry

    lax.fori_loop(0, i, score_loop, 0)
    score_block(i, True)

    def count(pred):
        def body(kb, acc):
            ind = jnp.where(pred(key_ref[kb]), 1.0, 0.0)
            return acc + ind.reshape(tq // SUBLANES, SUBLANES, tq).sum(axis=0)
        acc = lax.fori_loop(0, nkb, body, jnp.zeros((SUBLANES, tq), F32))
        return jnp.sum(acc, axis=0, keepdims=True)

    def bit_step(it, ans_u):
        cand_u = ans_u | lax.shift_left(jnp.int32(1), 31 - it)
        cand = cand_u ^ INT_MIN
        return jnp.where(count(lambda k: k >= cand) >= n_sel, cand_u, ans_u)

    thr = lax.fori_loop(0, 32, bit_step, jnp.zeros((1, tq), jnp.int32)) ^ INT_MIN

    n_gt = count(lambda k: k > thr)
    need = jnp.where(thr != INT_MIN, n_sel - n_gt, 0.0)
    tri = tri_ref[...]

    def select_block(kb, seen):
        okey = key_ref[kb]
        eq = okey == thr
        eq_f = jnp.where(eq, 1.0, 0.0)
        earlier = seen + _dot(tri, eq_f.astype(BF16))
        tie_bias = jnp.where(eq, jnp.where(earlier < need, 0.0, NEG_INF), NEG_INF)
        bias_ref[kb] = jnp.where(okey > thr, 0.0, tie_bias)
        return seen + jnp.sum(eq_f, axis=0, keepdims=True)

    lax.fori_loop(0, nkb, select_block, jnp.zeros((1, tq), F32))

    m_ref[...] = jnp.full_like(m_ref, NEG_INF)
    l_ref[...] = jnp.zeros_like(l_ref)
    acc_ref[...] = jnp.zeros_like(acc_ref)

    def attend(kb, carry):
        k0 = pl.multiple_of(kb * tq, tq)

        def scores(hd, _):
            return _dot_nt(kv_ref[0, pl.ds(k0, tq), :], q_ref[0, hd]) + bias_ref[kb]

        def update(hd, s):
            m_ref[hd], l_ref[hd], acc_ref[hd] = _softmax2_step(
                s, vt_ref[0, :, pl.ds(k0, tq)], m_ref[hd], l_ref[hd], acc_ref[hd])

        _skewed(HEADS, [scores, update])
        return carry

    lax.fori_loop(0, nkb, attend, 0)
    o_ref[0] = _heads_to_rows([acc_ref[hd] / l_ref[hd] for hd in range(HEADS)]).astype(o_ref.dtype)


def _dsa_attention(q, kv, vt, qi, kit, wt, tri, tq):
    b, _, s, _ = q.shape
    n_sel = min(TOPK_MAX, s // 4)
    return pl.pallas_call(
        functools.partial(_dsa_body, n_sel),
        out_shape=jax.ShapeDtypeStruct((b, s, GROUP_WIDTH), BF16),
        grid=(b, s // tq),
        in_specs=[
            pl.BlockSpec((1, HEADS, tq, LANES), lambda i, j: (i, 0, j, 0)),
            pl.BlockSpec((1, s, LANES), lambda i, j: (i, 0, 0)),
            pl.BlockSpec((1, HEAD_DIM, s), lambda i, j: (i, 0, 0)),
            pl.BlockSpec((1, IDX_HEADS, tq, LANES), lambda i, j: (i, 0, j, 0)),
            pl.BlockSpec((1, s, LANES), lambda i, j: (i, 0, 0)),
            pl.BlockSpec((1, SUBLANES, tq), lambda i, j: (i, 0, j)),
            pl.BlockSpec((tq, tq), lambda i, j: (0, 0)),
        ],
        out_specs=pl.BlockSpec((1, tq, GROUP_WIDTH), lambda i, j: (i, j, 0)),
        scratch_shapes=[
            pltpu.VMEM((s // tq, tq, tq), jnp.int32), pltpu.VMEM((s // tq, tq, tq), F32),
            pltpu.VMEM((HEADS, 1, tq), F32), pltpu.VMEM((HEADS, 1, tq), F32),
            pltpu.VMEM((HEADS, HEAD_DIM, tq), F32),
        ],
        compiler_params=_cparams(("arbitrary", "arbitrary")),
        name="sparse_indexer_attention",
    )(q, kv, vt, qi, kit, wt, tri)


def _store_rows(ref, val):
    for c in range(val.shape[1] // LANES):
        ref[:, c, :] = val[:, c * LANES:(c + 1) * LANES]


def _load_rows(ref, r0=0, n=None):
    n = ref.shape[0] if n is None else n
    return jnp.concatenate([ref[r0:r0 + n, c, :] for c in range(ref.shape[1])], axis=-1)


def _outstage_body(ya_ref, yb_ref, yc_ref, yd_ref, wo_ref, x_ref, g1_ref, sc_ref, sh_ref, ng_ref,
                   wr_ref, br_ref, tri_ref, xo_ref, h_ref, route_ref, gate_ref, cnt_ref, carry_ref):
    tm = x_ref.shape[0]
    gw = GROUP_WIDTH
    mix = sum(_dot(r[...], wo_ref[n * gw:(n + 1) * gw, :]) for n, r in enumerate((ya_ref, yb_ref, yc_ref, yd_ref)))
    x = x_ref[...] + g1_ref[0] * mix
    xo_ref[...] = x
    y = x * lax.rsqrt(jnp.mean(x * x, axis=-1, keepdims=True) + NORM_EPS) * ng_ref[...]
    h = y * (1.0 + sc_ref[0]) + sh_ref[0]
    _store_rows(h_ref, h)

    @pl.when(pl.program_id(0) == 0)
    def _():
        carry_ref[...] = jnp.zeros_like(carry_ref)

    logits = _dot_nt(wr_ref[...], h.astype(BF16)) + br_ref[...]
    e_iota = lax.broadcasted_iota(jnp.int32, (N_EXPERTS, tm), 0).astype(F32)
    vals, ids = [], []
    for _ in range(TOP_K):
        top = jnp.max(logits, axis=0, keepdims=True)
        idx = jnp.min(jnp.where(logits == top, e_iota, float(N_EXPERTS)), axis=0, keepdims=True)
        logits = jnp.where(e_iota == idx, -jnp.inf, logits)
        vals.append(top)
        ids.append(idx)
    exps = [jnp.exp(v - vals[0]) for v in vals]
    denom = sum(exps)
    onehot = sum(jnp.where(e_iota == idx, 1.0, 0.0) for idx in ids)
    before = carry_ref[...] + _dot(onehot.astype(BF16), tri_ref[...])
    ranks = [jnp.sum(jnp.where(e_iota == idx, before, 0.0), axis=0, keepdims=True) for idx in ids]
    carry_ref[...] += jnp.sum(onehot, axis=1, keepdims=True)
    route_ref[0] = jnp.concatenate(ids + ranks, axis=0).astype(jnp.int32)
    gate_ref[0] = jnp.concatenate([e / denom for e in exps] + [jnp.zeros_like(denom)] * TOP_K, axis=0)
    cnt_ref[...] = jnp.broadcast_to(carry_ref[...], cnt_ref.shape)


def _output_stage(ys, wo, x, g1, sc2, sh2, ng, wr_t, br, tri, tiles_per_batch):
    t, d = x.shape
    tm = TOK_TILE
    nt = t // tm
    row = lambda width: pl.BlockSpec((tm, width), lambda i: (i, 0))
    const = lambda shape: pl.BlockSpec(shape, lambda i: (0,) * len(shape))
    per_batch = pl.BlockSpec((1, 1, d), lambda i: (i // tiles_per_batch, 0, 0))
    return pl.pallas_call(
        _outstage_body,
        out_shape=(
            jax.ShapeDtypeStruct((t, d), F32), jax.ShapeDtypeStruct((t, d // LANES, LANES), F32),
            jax.ShapeDtypeStruct((nt, ROUTE_ROWS, tm), jnp.int32), jax.ShapeDtypeStruct((nt, ROUTE_ROWS, tm), F32),
            jax.ShapeDtypeStruct((N_EXPERTS, LANES), F32),
        ),
        grid=(nt,),
        in_specs=[row(GROUP_WIDTH)] * 4 + [const((d, d)), row(d), per_batch, per_batch, per_batch, const((1, d)),
                                           const((N_EXPERTS, d)), const((N_EXPERTS, 1)), const((tm, tm))],
        out_specs=(row(d), pl.BlockSpec((tm, d // LANES, LANES), lambda i: (i, 0, 0)),
                   pl.BlockSpec((1, ROUTE_ROWS, tm), lambda i: (i, 0, 0)),
                   pl.BlockSpec((1, ROUTE_ROWS, tm), lambda i: (i, 0, 0)),
                   const((N_EXPERTS, LANES))),
        scratch_shapes=[pltpu.VMEM((N_EXPERTS, 1), F32)],
        compiler_params=_cparams(("arbitrary",)),
        name="mixer_output_stage_router",
    )(*ys, wo, x, g1, sc2, sh2, ng, wr_t, br, tri)


def _route_fetch(route_hbm, route_smem, rsem):
    i = pl.program_id(0)
    slot = lax.rem(i, 2)

    def fetch(step, sl):
        return pltpu.make_async_copy(route_hbm.at[step], route_smem.at[sl], rsem.at[sl])

    @pl.when(i == 0)
    def _():
        fetch(0, 0).start()

    fetch(i, slot).wait()

    @pl.when(i + 1 < pl.num_programs(0))
    def _():
        fetch(i + 1, 1 - slot).start()

    return slot


def _dispatch_body(pstart_ref, pend_ref, route_hbm, h_ref, xs_hbm, route_smem, zero_ref, rsem, zsem, sem):
    tm = h_ref.shape[0]
    i = pl.program_id(0)

    @pl.when(i == 0)
    def _():
        zero_ref[...] = jnp.zeros_like(zero_ref)

        def fill(e):
            start = pl.multiple_of(pend_ref[e] - EXPERT_TILE, EXPERT_TILE)
            return pltpu.make_async_copy(zero_ref, xs_hbm.at[pl.ds(start, EXPERT_TILE)], zsem)

        def start(e, c):
            @pl.when(pend_ref[e] > pstart_ref[e])
            def _():
                fill(e).start()
            return c

        def wait(e, c):
            @pl.when(pend_ref[e] > pstart_ref[e])
            def _():
                fill(e).wait()
            return c

        lax.fori_loop(0, N_EXPERTS, start, 0)
        lax.fori_loop(0, N_EXPERTS, wait, 0)

    slot = _route_fetch(route_hbm, route_smem, rsem)

    def start(r, c):
        pltpu.make_async_copy(h_ref.at[r & (tm - 1)], xs_hbm.at[route_smem[slot, r]], sem).start()
        return c

    lax.fori_loop(0, TOP_K * tm, start, 0, unroll=DMA_UNROLL)
    for _ in range(TOP_K):
        pltpu.make_async_copy(h_ref, xs_hbm.at[pl.ds(0, tm)], sem).wait()


def _dispatch(pstart, pend, route, h, n_rows):
    t, dc, _ = h.shape
    tm = TOK_TILE
    return pl.pallas_call(
        _dispatch_body,
        out_shape=jax.ShapeDtypeStruct((n_rows, dc, LANES), F32),
        grid_spec=pltpu.PrefetchScalarGridSpec(
            num_scalar_prefetch=2,
            grid=(t // tm,),
            in_specs=[pl.BlockSpec(memory_space=pl.ANY), pl.BlockSpec((tm, dc, LANES), lambda i, ps, pe: (i, 0, 0))],
            out_specs=pl.BlockSpec(memory_space=pl.ANY),
            scratch_shapes=[
                pltpu.SMEM((2, TOP_K * tm), jnp.int32), pltpu.VMEM((EXPERT_TILE, dc, LANES), F32),
                pltpu.SemaphoreType.DMA((2,)), pltpu.SemaphoreType.DMA, pltpu.SemaphoreType.DMA,
            ],
        ),
        compiler_params=_cparams(("arbitrary",)),
        name="moe_dispatch",
    )(pstart, pend, route, h)


def _expert_body(tile_e_ref, n_used_ref, x_ref, wgu_ref, bgu_ref, wdn_ref, bdn_ref, y_ref):
    @pl.when(pl.program_id(0) < n_used_ref[0])
    def _():
        f = wdn_ref.shape[1]
        gu = _dot(_load_rows(x_ref).astype(BF16), wgu_ref[0]) + bgu_ref[0]
        glu = jnp.minimum(gu[:, :f], SWIGLU_LIMIT)
        lin = jnp.clip(gu[:, f:], -SWIGLU_LIMIT, SWIGLU_LIMIT)
        act = glu * jax.nn.sigmoid(SWIGLU_ALPHA * glu) * (lin + 1.0)
        _store_rows(y_ref, _dot(act.astype(BF16), wdn_ref[0]) + bdn_ref[0])


def _experts(tile_e, n_used, xs, wgu, bgu, wdn, bdn):
    p, dc, _ = xs.shape
    d = dc * LANES
    f = wdn.shape[1]
    tm = EXPERT_TILE
    used = lambda i, te, nu: jnp.minimum(i, nu[0] - 1)
    return pl.pallas_call(
        _expert_body,
        out_shape=jax.ShapeDtypeStruct((p, dc, LANES), F32),
        grid_spec=pltpu.PrefetchScalarGridSpec(
            num_scalar_prefetch=2,
            grid=(p // tm,),
            in_specs=[
                pl.BlockSpec((tm, dc, LANES), lambda i, te, nu: (used(i, te, nu), 0, 0)),
                pl.BlockSpec((1, d, 2 * f), lambda i, te, nu: (te[i], 0, 0)),
                pl.BlockSpec((1, 1, 2 * f), lambda i, te, nu: (te[i], 0, 0)),
                pl.BlockSpec((1, f, d), lambda i, te, nu: (te[i], 0, 0)),
                pl.BlockSpec((1, 1, d), lambda i, te, nu: (te[i], 0, 0)),
            ],
            out_specs=pl.BlockSpec((tm, dc, LANES), lambda i, te, nu: (used(i, te, nu), 0, 0)),
        ),
        compiler_params=_cparams(("arbitrary",)),
        name="moe_experts",
    )(tile_e, n_used, xs, wgu, bgu, wdn, bdn)


COMBINE_ROWS = 32


def _combine_body(final, route_hbm, ys_hbm, x_ref, g2_ref, gate_ref, fg_ref, o_ref,
                  route_smem, buf_ref, rsem, sem):
    tm = x_ref.shape[0]
    slot = _route_fetch(route_hbm, route_smem, rsem)

    def start(r, c):
        pltpu.make_async_copy(ys_hbm.at[route_smem[slot, r]], buf_ref.at[r], sem).start()
        return c

    lax.fori_loop(0, TOP_K * tm, start, 0, unroll=DMA_UNROLL)
    pltpu.make_async_copy(ys_hbm.at[pl.ds(0, TOP_K * tm)], buf_ref, sem).wait()
    g2 = g2_ref[0]
    fg = fg_ref[...]

    def rows(n, c):
        r0 = pl.multiple_of(n * COMBINE_ROWS, COMBINE_ROWS)
        gates = gate_ref[pl.ds(r0, COMBINE_ROWS), :]
        moe = sum(gates[:, k:k + 1] * jnp.concatenate(
            [buf_ref[pl.ds(k * tm + r0, COMBINE_ROWS), cc, :] for cc in range(buf_ref.shape[1])], axis=-1)
            for k in range(TOP_K))
        x = x_ref[pl.ds(r0, COMBINE_ROWS), :] + g2 * moe
        if final:
            x = x * lax.rsqrt(jnp.mean(x * x, axis=-1, keepdims=True) + NORM_EPS) * fg
        o_ref[pl.ds(r0, COMBINE_ROWS), :] = x
        return c

    lax.fori_loop(0, tm // COMBINE_ROWS, rows, 0)


def _combine(route, ys, x, g2, gates_col, final_g, tiles_per_batch, final):
    t, d = x.shape
    tm = TOK_TILE
    return pl.pallas_call(
        functools.partial(_combine_body, final),
        out_shape=jax.ShapeDtypeStruct((t, d), F32),
        grid=(t // tm,),
        in_specs=[
            pl.BlockSpec(memory_space=pl.ANY), pl.BlockSpec(memory_space=pl.ANY),
            pl.BlockSpec((tm, d), lambda i: (i, 0)),
            pl.BlockSpec((1, 1, d), lambda i: (i // tiles_per_batch, 0, 0)),
            pl.BlockSpec((tm, ROUTE_ROWS), lambda i: (i, 0)),
            pl.BlockSpec((1, d), lambda i: (0, 0)),
        ],
        out_specs=pl.BlockSpec((tm, d), lambda i: (i, 0)),
        scratch_shapes=[
            pltpu.SMEM((2, TOP_K * tm), jnp.int32), pltpu.VMEM((TOP_K * tm, d // LANES, LANES), F32),
            pltpu.SemaphoreType.DMA((2,)), pltpu.SemaphoreType.DMA,
        ],
        compiler_params=_cparams(("arbitrary",)),
        name="moe_combine",
    )(route, ys, x, g2, gates_col, final_g)


def _moe_plan(counts, n_tiles):
    counts = counts.astype(jnp.int32)
    tiles = (counts + EXPERT_TILE - 1) // EXPERT_TILE
    tile_end = jnp.cumsum(tiles)
    pend = tile_end * EXPERT_TILE
    pstart = pend - tiles * EXPERT_TILE
    n_used = tile_end[-1:]
    tile = jnp.minimum(jnp.arange(n_tiles), n_used - 1)
    tile_e = jnp.sum((tile_end[None, :] <= tile[:, None]).astype(jnp.int32), axis=1)
    return pstart, pend, tile_e, n_used


def _tri(n, rel):
    r = jnp.arange(n)[:, None]
    c = jnp.arange(n)[None, :]
    return rel(r, c).astype(BF16)


def _constants(s):
    return dict(
        tab=_rope_tables(s),
        tri_le=_tri(ATT_TILE, lambda r, c: c <= r),
        tri_gt=_tri(ATT_TILE, lambda r, c: r > c),
        tri_lt=_tri(ATT_TILE, lambda r, c: r < c),
        tri_lt_tok=_tri(TOK_TILE, lambda r, c: r < c),
    )


def _prep_layer(p, l):
    bf = jnp.zeros((1, LANES), F32).at[0, MISC_FA:MISC_FA + HEADS].set(p["b_forget"][l].astype(F32))
    w, wt = _reorder_w_in(p["w_in"][l])
    return dict(
        w_in=w, w_in_t=wt,
        wkv=jnp.concatenate([p["w_uk"][l], p["w_uv"][l]], axis=-1).astype(BF16),
        wuvt=p["w_uv"][l].T.astype(BF16),
        kvg=p["kv_norm_g"][l].reshape(1, KV_RANK).astype(F32),
        bf=bf,
        g1=p["norm1_g"][l].reshape(1, -1).astype(F32),
        chunk_bias=_chunk_bias_table(p["rel_bias"][l], ATT_TILE),
    )


def _mixers(x, sc1, sh1, lp, cst):
    tq = ATT_TILE
    (fq, fk, fvt, dq, dkv, dvt, dqi, dkit, dwt, cq, ck, cvt, eq, ek, ev) = _input_stage(
        x, sc1, sh1, lp["g1"], lp["w_in"], lp["w_in_t"], lp["wkv"], lp["wuvt"], lp["kvg"], lp["bf"],
        cst["tri_le"], cst["tab"], tq)
    ya = _fox_attention(fq, fk, fvt, tq)
    yb = _dsa_attention(dq, dkv, dvt, dqi, dkit, dwt, cst["tri_gt"], tq)
    yc = _sb_attention(cq, ck, cvt, cst["tri_lt"], tq)
    yd = _chunk_attention(eq, ek, ev, lp["chunk_bias"], tq)
    return ya, yb, yc, yd


def _moe(p, l, x, h, route, gate, counts, g2, tiles_per_batch, final):
    t, d = x.shape
    nt = t // TOK_TILE
    n_tiles = t * TOP_K // EXPERT_TILE + N_EXPERTS
    pstart, pend, tile_e, n_used = _moe_plan(counts, n_tiles)
    ids, ranks = route[:, :TOP_K, :], route[:, TOP_K:, :]
    first_row = sum(jnp.where(ids == e, pstart[e], 0) for e in range(N_EXPERTS))
    route = (first_row + ranks).reshape(nt, TOP_K * TOK_TILE)
    xs = _dispatch(pstart, pend, route, h, n_tiles * EXPERT_TILE)
    ys = _experts(tile_e, n_used, xs,
                  p["w_gu"][l].astype(BF16), p["b_gu"][l][:, None, :].astype(F32),
                  p["w_dn"][l].astype(BF16), p["b_dn"][l][:, None, :].astype(F32))
    gates_col = gate.transpose(0, 2, 1).reshape(t, ROUTE_ROWS)
    return _combine(route, ys, x, g2, gates_col, p["final_g"].reshape(1, d).astype(F32), tiles_per_batch, final)


def kernel(x, c, w_ada, b_ada, norm1_g, norm2_g, w_in, b_forget, kv_norm_g, w_uk, w_uv, rel_bias, w_o,
           w_router, b_router, w_gu, b_gu, w_dn, b_dn, final_g):
    p = dict(w_in=w_in, b_forget=b_forget, kv_norm_g=kv_norm_g, w_uk=w_uk, w_uv=w_uv, rel_bias=rel_bias,
             norm1_g=norm1_g, w_gu=w_gu, b_gu=b_gu, w_dn=w_dn, b_dn=b_dn, final_g=final_g)
    b, s, d = x.shape
    t = b * s
    depth = w_ada.shape[0]
    tiles_per_batch = s // TOK_TILE
    mod = _modulation(c, w_ada, b_ada)
    cst = _constants(s)
    for l in range(depth):
        sh1, sc1, g1, sh2, sc2, g2 = [m[:, None, :] for m in jnp.split(mod[l], 6, axis=-1)]
        ys = _mixers(x, sc1, sh1, _prep_layer(p, l), cst)
        x2, h, route, gate, cnt = _output_stage(
            [y.reshape(t, GROUP_WIDTH) for y in ys], w_o[l].astype(BF16), x.reshape(t, d), g1, sc2, sh2,
            norm2_g[l].reshape(1, d).astype(F32), w_router[l].T.astype(BF16),
            b_router[l].reshape(N_EXPERTS, 1).astype(F32), cst["tri_lt_tok"], tiles_per_batch)
        x = _moe(p, l, x2, h, route, gate, cnt[:, 0], g2, tiles_per_batch, l == depth - 1).reshape(b, s, d)
    return x
```

```python
import functools
import math

import jax
import jax.numpy as jnp
from jax import lax
from jax.experimental import pallas as pl
from jax.experimental.pallas import tpu as pltpu

F32 = jnp.float32
BF16 = jnp.bfloat16

HEAD_DIM = 64
HEADS = 4
GROUP_WIDTH = HEADS * HEAD_DIM
CHUNK = 64
ROPE_THETA = 10000.0
NORM_EPS = 1e-6
NEG_INF = -1e30
KV_RANK = 128
IDX_HEADS = 4
IDX_DIM = 32
TOPK_MAX = 256
LEFT_CHUNKS = 8
MAX_REL = 128
N_EXPERTS = 32
TOP_K = 4
SWIGLU_ALPHA = 1.702
SWIGLU_LIMIT = 7.0
LOG2E = math.log2(math.e)
INT_MIN = -2 ** 31

LANES = 128
SUBLANES = 8
VMEM_LIMIT = 48 * 1024 * 1024

ATT_TILE = 256
TOK_TILE = 512
EXPERT_TILE = 512
ROUTE_ROWS = 2 * TOP_K
DMA_UNROLL = 8

COL_A = 0
COL_C = 3 * GROUP_WIDTH
COL_D = 6 * GROUP_WIDTH
COL_QB = 9 * GROUP_WIDTH
COL_LAT = COL_QB + GROUP_WIDTH
COL_QI = COL_LAT + KV_RANK
COL_MISC = COL_QI + IDX_HEADS * IDX_DIM
N_COLS = COL_MISC + LANES
MISC_WI = IDX_DIM
MISC_FA = IDX_DIM + IDX_HEADS
ROW_VA = 0
ROW_VC = GROUP_WIDTH
ROW_WI = 2 * GROUP_WIDTH
N_ROWS_T = ROW_WI + SUBLANES


def _cparams(sem):
    return pltpu.CompilerParams(dimension_semantics=sem, vmem_limit_bytes=VMEM_LIMIT)


def _split3(x):
    hi = x.astype(BF16)
    r = x - hi.astype(F32)
    mid = r.astype(BF16)
    lo = (r - mid.astype(F32)).astype(BF16)
    return hi, mid, lo


def _dot(a, b):
    return jnp.dot(a, b, preferred_element_type=F32)


def _dot_nt(a, b):
    return lax.dot_general(a, b, (((1,), (1,)), ((), ())), preferred_element_type=F32)


def _mod_body(c_ref, w_ref, b_ref, o_ref):
    c = c_ref[...]
    act = (c * jax.nn.sigmoid(c)).astype(BF16)
    o_ref[0] = _dot(act, w_ref[0].astype(BF16)) + b_ref[0]


def _modulation(c, w_ada, b_ada):
    depth, d, n = w_ada.shape
    b = c.shape[0]
    tn = 1024
    return pl.pallas_call(
        _mod_body,
        out_shape=jax.ShapeDtypeStruct((depth, b, n), F32),
        grid=(depth, n // tn),
        in_specs=[
            pl.BlockSpec((b, d), lambda l, j: (0, 0)),
            pl.BlockSpec((1, d, tn), lambda l, j: (l, 0, j)),
            pl.BlockSpec((1, 1, tn), lambda l, j: (l, 0, j)),
        ],
        out_specs=pl.BlockSpec((1, b, tn), lambda l, j: (l, 0, j)),
        compiler_params=_cparams(("arbitrary", "arbitrary")),
        name="adaln_mod",
    )(c, w_ada, b_ada.reshape(depth, 1, n))


def _rope_tables(s):
    pos = jnp.arange(s, dtype=F32)[:, None]
    lane = jnp.arange(LANES)[None, :]

    def tables(dim, active):
        half = dim // 2
        j = lane % dim
        inv = ROPE_THETA ** (-(j % half).astype(F32) / half)
        ang = pos * inv
        cos = jnp.where(active, jnp.cos(ang), 1.0)
        sin = jnp.where(active, jnp.sin(ang), 0.0)
        first = j < half
        return [cos, jnp.where(first, -sin, 0.0), jnp.where(first, 0.0, sin)]

    everywhere = lane >= 0
    t = (tables(HEAD_DIM, everywhere) + tables(HEAD_DIM, lane < HEAD_DIM)
         + tables(IDX_DIM, everywhere) + tables(IDX_DIM, lane < IDX_DIM))
    return jnp.stack(t).astype(F32)


def _rope(x, tab, k, half):
    cos, sin_a, sin_b = tab[3 * k], tab[3 * k + 1], tab[3 * k + 2]
    return x * cos + pltpu.roll(x, LANES - half, 1) * sin_a + pltpu.roll(x, half, 1) * sin_b


def _instage_body(x_ref, sc_ref, sh_ref, g_ref, w_ref, wt_ref, wkv_ref, wuvt_ref, kvg_ref, bf_ref, tri_ref, tab_ref,
                  fq_o, fk_o, fvt_o, dq_o, dkv_o, dvt_o, dqi_o, dkit_o, dwt_o,
                  cq_o, ck_o, cvt_o, eq_o, ek_o, ev_o, carry_ref):
    tm = x_ref.shape[1]
    x = x_ref[0]
    y = x * lax.rsqrt(jnp.mean(x * x, axis=-1, keepdims=True) + NORM_EPS) * g_ref[...]
    h = (y * (1.0 + sc_ref[0]) + sh_ref[0]).astype(BF16)

    def proj(c0, n):
        return _dot(h, w_ref[:, c0:c0 + n])

    def proj_t(r0, n):
        return _dot_nt(wt_ref[r0:r0 + n, :], h)

    def heads_out(val, refs, scale):
        for i, ref in enumerate(refs):
            for hd in range(HEADS):
                c0 = i * GROUP_WIDTH + hd * HEAD_DIM
                piece = val[:, c0:c0 + HEAD_DIM]
                if i == 0:
                    piece = piece * scale
                ref[0, hd] = piece.astype(ref.dtype)

    def heads_out_t(val, ref):
        for hd in range(HEADS):
            ref[0, hd] = val[hd * HEAD_DIM:(hd + 1) * HEAD_DIM, :].astype(ref.dtype)

    scale = HEAD_DIM ** -0.5
    heads_out(proj(COL_C, 2 * GROUP_WIDTH), (cq_o, ck_o), scale)
    heads_out_t(proj_t(ROW_VC, GROUP_WIDTH), cvt_o)
    heads_out(proj(COL_D, 3 * GROUP_WIDTH), (eq_o, ek_o, ev_o), scale)

    misc = _rope(proj(COL_MISC, LANES), tab_ref, 3, IDX_DIM // 2)
    lane = lax.broadcasted_iota(jnp.int32, (tm, LANES), 1)
    ki = jnp.where(lane < IDX_DIM, misc, 0.0)
    kit = ki + pltpu.roll(ki, IDX_DIM, 1) + pltpu.roll(ki, 2 * IDX_DIM, 1) + pltpu.roll(ki, 3 * IDX_DIM, 1)
    dkit_o[0] = kit.astype(BF16)
    dwt_o[0] = proj_t(ROW_WI, SUBLANES) * ((IDX_HEADS ** -0.5) * (IDX_DIM ** -0.5))
    qi = _rope(proj(COL_QI, LANES), tab_ref, 2, IDX_DIM // 2)
    for hd in range(IDX_HEADS):
        in_head = (lane >= hd * IDX_DIM) & (lane < (hd + 1) * IDX_DIM)
        dqi_o[0, hd] = jnp.where(in_head, qi, 0.0).astype(BF16)
    lat = proj(COL_LAT, KV_RANK)
    lat = (lat * lax.rsqrt(jnp.mean(lat * lat, axis=-1, keepdims=True) + NORM_EPS) * kvg_ref[...]).astype(BF16)
    dkv_o[0] = _rope(_dot(lat, wkv_ref[...]), tab_ref, 1, HEAD_DIM // 2).astype(BF16)
    dvt_o[0] = _dot_nt(wuvt_ref[...], lat).astype(BF16)
    qb = proj(COL_QB, GROUP_WIDTH)
    zeros64 = jnp.zeros((tm, HEAD_DIM), F32)
    for half_i in range(2):
        r = _rope(qb[:, half_i * LANES:(half_i + 1) * LANES], tab_ref, 0, HEAD_DIM // 2) * (scale * LOG2E)
        for j in range(2):
            piece = jnp.concatenate([r[:, j * HEAD_DIM:(j + 1) * HEAD_DIM], zeros64], axis=-1)
            dq_o[0, 2 * half_i + j] = piece.astype(BF16)

    @pl.when(pl.program_id(1) == 0)
    def _():
        carry_ref[...] = jnp.zeros_like(carry_ref)

    z = misc + bf_ref[...]
    log_f = jnp.minimum(z, 0.0) - jnp.log1p(jnp.exp(-jnp.abs(z)))
    tri = tri_ref[...]
    f_cum = carry_ref[...] + sum(_dot(tri, p) for p in _split3(log_f))
    carry_ref[...] = f_cum[tm - 1:tm, :]
    f_cum = f_cum * LOG2E
    pa = proj(COL_A, 2 * GROUP_WIDTH)
    heads_out_t(proj_t(ROW_VA, GROUP_WIDTH), fvt_o)
    lane64 = lax.broadcasted_iota(jnp.int32, (tm, HEAD_DIM), 1)
    for hd in range(HEADS):
        fh = f_cum[:, MISC_FA + hd:MISC_FA + hd + 1]
        f3 = [jnp.broadcast_to(p.astype(F32), (tm, HEAD_DIM)) for p in _split3(fh)]
        one = jnp.where(lane64 < 6, 1.0, 0.0)
        q_ext = jnp.where(lane64 == 0, f3[0], jnp.where(lane64 == 1, f3[1], jnp.where(lane64 == 2, f3[2], one)))
        k_ext = jnp.where(lane64 == 3, -f3[0], jnp.where(lane64 == 4, -f3[1], jnp.where(lane64 == 5, -f3[2], one)))
        q = pa[:, hd * HEAD_DIM:(hd + 1) * HEAD_DIM] * (scale * LOG2E)
        k = pa[:, GROUP_WIDTH + hd * HEAD_DIM:GROUP_WIDTH + (hd + 1) * HEAD_DIM]
        fq_o[0, hd] = jnp.concatenate([q, q_ext], axis=-1).astype(BF16)
        fk_o[0, hd] = jnp.concatenate([k, k_ext], axis=-1).astype(BF16)


def _reorder_w_in(w_in):
    gw = GROUP_WIDTH
    a0 = 0
    b0 = 3 * gw + HEADS
    c0 = b0 + gw + KV_RANK + IDX_HEADS * IDX_DIM + IDX_DIM + IDX_HEADS
    d0 = c0 + 3 * gw
    lat0 = b0 + gw
    qi0 = lat0 + KV_RANK
    ki0 = qi0 + IDX_HEADS * IDX_DIM
    wi0 = ki0 + IDX_DIM
    pad = jnp.zeros(w_in.shape[:-1] + (LANES - IDX_DIM - IDX_HEADS - HEADS,), w_in.dtype)
    w = jnp.concatenate([
        w_in[:, a0:a0 + 3 * gw], w_in[:, c0:c0 + 3 * gw], w_in[:, d0:d0 + 3 * gw],
        w_in[:, b0:b0 + gw], w_in[:, lat0:lat0 + KV_RANK], w_in[:, qi0:qi0 + IDX_HEADS * IDX_DIM],
        w_in[:, ki0:ki0 + IDX_DIM], w_in[:, wi0:wi0 + IDX_HEADS], w_in[:, 3 * gw:3 * gw + HEADS], pad,
    ], axis=-1)
    wt = jnp.concatenate([
        w_in[:, a0 + 2 * gw:a0 + 3 * gw], w_in[:, c0 + 2 * gw:c0 + 3 * gw], w_in[:, wi0:wi0 + IDX_HEADS],
        jnp.zeros((w_in.shape[0], SUBLANES - IDX_HEADS), w_in.dtype),
    ], axis=-1).T
    return w.astype(BF16), wt.astype(BF16)


def _input_stage(x, sc, sh, g, w, wt, wkv, wuvt, kvg, bf, tri, tab, tm):
    b, s, d = x.shape
    hq = lambda width: jax.ShapeDtypeStruct((b, HEADS, s, width), BF16)
    hqt = jax.ShapeDtypeStruct((b, HEADS, HEAD_DIM, s), BF16)
    flat = jax.ShapeDtypeStruct((b, s, LANES), BF16)
    out_shape = (
        hq(LANES), hq(LANES), hqt,
        hq(LANES), flat, jax.ShapeDtypeStruct((b, HEAD_DIM, s), BF16),
        hq(LANES), flat, jax.ShapeDtypeStruct((b, SUBLANES, s), F32),
        hq(HEAD_DIM), hq(HEAD_DIM), hqt,
        hq(HEAD_DIM), hq(HEAD_DIM), hq(HEAD_DIM),
    )
    hspec = lambda width: pl.BlockSpec((1, HEADS, tm, width), lambda i, j: (i, 0, j, 0))
    htspec = pl.BlockSpec((1, HEADS, HEAD_DIM, tm), lambda i, j: (i, 0, 0, j))
    fspec = pl.BlockSpec((1, tm, LANES), lambda i, j: (i, j, 0))
    tspec = lambda rows: pl.BlockSpec((1, rows, tm), lambda i, j: (i, 0, j))
    const2 = lambda shape: pl.BlockSpec(shape, lambda i, j: (0, 0))
    out_specs = (
        hspec(LANES), hspec(LANES), htspec,
        hspec(LANES), fspec, tspec(HEAD_DIM),
        hspec(LANES), fspec, tspec(SUBLANES),
        hspec(HEAD_DIM), hspec(HEAD_DIM), htspec,
        hspec(HEAD_DIM), hspec(HEAD_DIM), hspec(HEAD_DIM),
    )
    return pl.pallas_call(
        _instage_body,
        out_shape=out_shape,
        grid=(b, s // tm),
        in_specs=[
            pl.BlockSpec((1, tm, d), lambda i, j: (i, j, 0)),
            pl.BlockSpec((1, 1, d), lambda i, j: (i, 0, 0)),
            pl.BlockSpec((1, 1, d), lambda i, j: (i, 0, 0)),
            const2((1, d)),
            const2((d, N_COLS)),
            const2((N_ROWS_T, d)),
            const2((KV_RANK, LANES)),
            const2((HEAD_DIM, KV_RANK)),
            const2((1, KV_RANK)),
            const2((1, LANES)),
            const2((tm, tm)),
            pl.BlockSpec((12, tm, LANES), lambda i, j: (0, j, 0)),
        ],
        out_specs=out_specs,
        scratch_shapes=[pltpu.VMEM((1, LANES), F32)],
        compiler_params=_cparams(("arbitrary", "arbitrary")),
        name="mixer_input_stage",
    )(x, sc, sh, g, w, wt, wkv, wuvt, kvg, bf, tri, tab)


def _key_query_iota(t):
    return lax.broadcasted_iota(jnp.int32, (t, t), 0), lax.broadcasted_iota(jnp.int32, (t, t), 1)


def _softmax2_step(s, vt, m, l, acc):
    m_new = jnp.maximum(m, jnp.max(s, axis=0, keepdims=True))
    alpha = jnp.exp2(m - m_new)
    p = jnp.exp2(s - m_new)
    l_new = alpha * l + jnp.sum(p, axis=0, keepdims=True)
    acc_new = alpha * acc + _dot(vt, p.astype(BF16))
    return m_new, l_new, acc_new


def _skewed(n, stages):
    vals = [None] * n
    for step in range(n + len(stages) - 1):
        for j, stage in enumerate(stages):
            i = step - j
            if 0 <= i < n:
                vals[i] = stage(i, vals[i])


def _heads_to_rows(accs):
    return jnp.concatenate([a.T for a in accs], axis=-1)


def _fox_body(q_ref, k_ref, vt_ref, o_ref, m_ref, l_ref, acc_ref):
    qi = pl.program_id(1)
    tq = q_ref.shape[2]
    key, qry = _key_query_iota(tq)
    m_ref[...] = jnp.full_like(m_ref, NEG_INF)
    l_ref[...] = jnp.zeros_like(l_ref)
    acc_ref[...] = jnp.zeros_like(acc_ref)

    def block(kb, masked):
        k0 = pl.multiple_of(kb * tq, tq)

        def scores(hd, _):
            s = _dot_nt(k_ref[0, hd, pl.ds(k0, tq), :], q_ref[0, hd])
            return jnp.where(key <= qry, s, NEG_INF) if masked else s

        def update(hd, s):
            m_ref[hd], l_ref[hd], acc_ref[hd] = _softmax2_step(
                s, vt_ref[0, hd, :, pl.ds(k0, tq)], m_ref[hd], l_ref[hd], acc_ref[hd])

        _skewed(HEADS, [scores, update])

    def loop_body(kb, carry):
        block(kb, False)
        return carry

    lax.fori_loop(0, qi, loop_body, 0)
    block(qi, True)
    o_ref[0] = _heads_to_rows([acc_ref[hd] / l_ref[hd] for hd in range(HEADS)]).astype(o_ref.dtype)


def _fox_attention(q, k, vt, tq):
    b, _, s, _ = q.shape
    return pl.pallas_call(
        _fox_body,
        out_shape=jax.ShapeDtypeStruct((b, s, GROUP_WIDTH), BF16),
        grid=(b, s // tq),
        in_specs=[
            pl.BlockSpec((1, HEADS, tq, LANES), lambda i, j: (i, 0, j, 0)),
            pl.BlockSpec((1, HEADS, s, LANES), lambda i, j: (i, 0, 0, 0)),
            pl.BlockSpec((1, HEADS, HEAD_DIM, s), lambda i, j: (i, 0, 0, 0)),
        ],
        out_specs=pl.BlockSpec((1, tq, GROUP_WIDTH), lambda i, j: (i, j, 0)),
        scratch_shapes=[pltpu.VMEM((HEADS, 1, tq), F32), pltpu.VMEM((HEADS, 1, tq), F32),
                        pltpu.VMEM((HEADS, HEAD_DIM, tq), F32)],
        compiler_params=_cparams(("arbitrary", "arbitrary")),
        name="forgetting_attention",
    )(q, k, vt)


def _sb_body(q_ref, k_ref, vt_ref, tri_ref, o_ref, r_ref, acc_ref):
    qi = pl.program_id(1)
    tq = q_ref.shape[2]
    key, qry = _key_query_iota(tq)
    tri = tri_ref[...]
    r_ref[...] = jnp.zeros_like(r_ref)
    acc_ref[...] = jnp.zeros_like(acc_ref)

    def block(kb, masked):
        k0 = pl.multiple_of(kb * tq, tq)

        def logits(hd, _):
            return _dot_nt(k_ref[0, hd, pl.ds(k0, tq), :], q_ref[0, hd])

        def later_sums(hd, z):
            log_1m = -(jnp.maximum(z, 0.0) + jnp.log1p(jnp.exp(-jnp.abs(z))))
            if masked:
                log_1m = jnp.where(key < qry, log_1m, 0.0)
            after = r_ref[hd] + _dot(tri, log_1m.astype(BF16))
            return z, log_1m, after

        def accumulate(hd, val):
            z, log_1m, after = val
            w = jnp.exp(log_1m + z + after)
            if masked:
                w = jnp.where(key < qry, w, 0.0)
            acc_ref[hd] += _dot(vt_ref[0, hd, :, pl.ds(k0, tq)], w.astype(BF16))
            r_ref[hd] += jnp.sum(log_1m, axis=0, keepdims=True)

        _skewed(HEADS, [logits, later_sums, accumulate])

    block(qi, True)

    def loop_body(n, carry):
        block(qi - 1 - n, False)
        return carry

    lax.fori_loop(0, qi, loop_body, 0)
    o_ref[0] = _heads_to_rows([acc_ref[hd] for hd in range(HEADS)]).astype(o_ref.dtype)


def _sb_attention(q, k, vt, tri, tq):
    b, _, s, _ = q.shape
    return pl.pallas_call(
        _sb_body,
        out_shape=jax.ShapeDtypeStruct((b, s, GROUP_WIDTH), BF16),
        grid=(b, s // tq),
        in_specs=[
            pl.BlockSpec((1, HEADS, tq, HEAD_DIM), lambda i, j: (i, 0, j, 0)),
            pl.BlockSpec((1, HEADS, s, HEAD_DIM), lambda i, j: (i, 0, 0, 0)),
            pl.BlockSpec((1, HEADS, HEAD_DIM, s), lambda i, j: (i, 0, 0, 0)),
            pl.BlockSpec((tq, tq), lambda i, j: (0, 0)),
        ],
        out_specs=pl.BlockSpec((1, tq, GROUP_WIDTH), lambda i, j: (i, j, 0)),
        scratch_shapes=[pltpu.VMEM((HEADS, 1, tq), F32), pltpu.VMEM((HEADS, HEAD_DIM, tq), F32)],
        compiler_params=_cparams(("arbitrary", "arbitrary")),
        name="stick_breaking_attention",
    )(q, k, vt, tri)


def _chunk_bias_table(rel_table, tq):
    left = LEFT_CHUNKS * CHUNK
    width = left + tq
    d = jnp.arange(width + tq - 1) - (tq - 1) - left
    diag = rel_table[:, jnp.clip(d, -MAX_REL, MAX_REL) + MAX_REL].astype(F32)
    bias = jnp.stack([diag[:, tq - 1 - t:tq - 1 - t + width] for t in range(tq)], axis=1)
    t = jnp.arange(tq)[:, None]
    j = jnp.arange(width)[None, :]
    c0 = (t // CHUNK) * CHUNK
    visible = (j >= c0) & (j - left < c0 + CHUNK)
    return jnp.where(visible[None], bias, NEG_INF)


def _chunk_body(nb, q_ref, *refs):
    k_refs, v_refs = refs[:nb], refs[nb:2 * nb]
    bias_ref, o_ref = refs[2 * nb], refs[2 * nb + 1]
    qi = pl.program_id(1)
    outs = []
    for hd in range(HEADS):
        q = q_ref[0, hd]
        s_parts = []
        for n in range(nb):
            s = _dot_nt(q, k_refs[n][0, hd])
            s_parts.append(jnp.where(qi - (nb - 1) + n >= 0, s, NEG_INF))
        s = jnp.concatenate(s_parts, axis=-1) + bias_ref[hd]
        m = jnp.max(s, axis=-1, keepdims=True)
        p = jnp.exp(s - m)
        l = jnp.sum(p, axis=-1, keepdims=True)
        p = p.astype(BF16)
        tq = q.shape[0]
        acc = sum(_dot(p[:, n * tq:(n + 1) * tq], v_refs[n][0, hd]) for n in range(nb))
        outs.append(acc / l)
    o_ref[0] = jnp.concatenate(outs, axis=-1).astype(o_ref.dtype)


def _chunk_attention(q, k, v, bias, tq):
    b, _, s, _ = q.shape
    left = LEFT_CHUNKS * CHUNK
    assert left % tq == 0
    nb = left // tq + 1
    kv_specs = [
        pl.BlockSpec((1, HEADS, tq, HEAD_DIM), functools.partial(
            lambda i, j, n: (i, 0, jnp.maximum(j - (nb - 1) + n, 0), 0), n=n))
        for n in range(nb)
    ]
    return pl.pallas_call(
        functools.partial(_chunk_body, nb),
        out_shape=jax.ShapeDtypeStruct((b, s, GROUP_WIDTH), BF16),
        grid=(b, s // tq),
        in_specs=[pl.BlockSpec((1, HEADS, tq, HEAD_DIM), lambda i, j: (i, 0, j, 0))] + kv_specs + kv_specs
        + [pl.BlockSpec((HEADS, tq, nb * tq), lambda i, j: (0, 0, 0))],
        out_specs=pl.BlockSpec((1, tq, GROUP_WIDTH), lambda i, j: (i, j, 0)),
        compiler_params=_cparams(("arbitrary", "arbitrary")),
        name="chunked_relbias_attention",
    )(q, *([k] * nb), *([v] * nb), bias)


def _dsa_body(n_sel, q_ref, kv_ref, vt_ref, qi_ref, kit_ref, wt_ref, tri_ref, o_ref,
              key_ref, bias_ref, m_ref, l_ref, acc_ref):
    i = pl.program_id(1)
    tq = q_ref.shape[2]
    nkb = i + 1
    key_pos, qry_pos = _key_query_iota(tq)
    vis_diag = key_pos < (qry_pos // CHUNK + 1) * CHUNK
    wt = wt_ref[0]

    def score_block(kb, masked):
        k0 = pl.multiple_of(kb * tq, tq)
        kit = kit_ref[0, pl.ds(k0, tq), :]
        dots = [_dot_nt(kit, qi_ref[0, hd]) for hd in range(IDX_HEADS)]
        score = sum(wt[hd:hd + 1, :] * jnp.maximum(dots[hd], 0.0) for hd in range(IDX_HEADS))
        score = jnp.where(score == 0.0, 0.0, score)
        bits = pltpu.bitcast(score, jnp.int32)
        okey = jnp.where(bits < 0, bits ^ 0x7FFFFFFF, bits)
        if masked:
            okey = jnp.where(vis_diag, okey, INT_MIN)
        key_ref[kb] = okey

    def score_loop(kb, carry):
        score_block(kb, False)
        return carry

    lax.fori_loop(0, i, score_loop, 0)
    score_block(i, True)

    def count(pred):
        def body(kb, acc):
            ind = jnp.where(pred(key_ref[kb]), 1.0, 0.0)
            return acc + ind.reshape(tq // SUBLANES, SUBLANES, tq).sum(axis=0)
        acc = lax.fori_loop(0, nkb, body, jnp.zeros((SUBLANES, tq), F32))
        return jnp.sum(acc, axis=0, keepdims=True)

    def bit_step(it, ans_u):
        cand_u = ans_u | lax.shift_left(jnp.int32(1), 31 - it)
        cand = cand_u ^ INT_MIN
        return jnp.where(count(lambda k: k >= cand) >= n_sel, cand_u, ans_u)

    thr = lax.fori_loop(0, 32, bit_step, jnp.zeros((1, tq), jnp.int32)) ^ INT_MIN

    n_gt = count(lambda k: k > thr)
    need = jnp.where(thr != INT_MIN, n_sel - n_gt, 0.0)
    tri = tri_ref[...]

    def select_block(kb, seen):
        okey = key_ref[kb]
        eq = okey == thr
        eq_f = jnp.where(eq, 1.0, 0.0)
        earlier = seen + _dot(tri, eq_f.astype(BF16))
        tie_bias = jnp.where(eq, jnp.where(earlier < need, 0.0, NEG_INF), NEG_INF)
        bias_ref[kb] = jnp.where(okey > thr, 0.0, tie_bias)
        return seen + jnp.sum(eq_f, axis=0, keepdims=True)

    lax.fori_loop(0, nkb, select_block, jnp.zeros((1, tq), F32))

    m_ref[...] = jnp.full_like(m_ref, NEG_INF)
    l_ref[...] = jnp.zeros_like(l_ref)
    acc_ref[...] = jnp.zeros_like(acc_ref)

    def attend(kb, carry):
        k0 = pl.multiple_of(kb * tq, tq)

        def scores(hd, _):
            return _dot_nt(kv_ref[0, pl.ds(k0, tq), :], q_ref[0, hd]) + bias_ref[kb]

        def update(hd, s):
            m_ref[hd], l_ref[hd], acc_ref[hd] = _softmax2_step(
                s, vt_ref[0, :, pl.ds(k0, tq)], m_ref[hd], l_ref[hd], acc_ref[hd])

        _skewed(HEADS, [scores, update])
        return carry

    lax.fori_loop(0, nkb, attend, 0)
    o_ref[0] = _heads_to_rows([acc_ref[hd] / l_ref[hd] for hd in range(HEADS)]).astype(o_ref.dtype)


def _dsa_attention(q, kv, vt, qi, kit, wt, tri, tq):
    b, _, s, _ = q.shape
    n_sel = min(TOPK_MAX, s // 4)
    return pl.pallas_call(
        functools.partial(_dsa_body, n_sel),
        out_shape=jax.ShapeDtypeStruct((b, s, GROUP_WIDTH), BF16),
        grid=(b, s // tq),
        in_specs=[
            pl.BlockSpec((1, HEADS, tq, LANES), lambda i, j: (i, 0, j, 0)),
            pl.BlockSpec((1, s, LANES), lambda i, j: (i, 0, 0)),
            pl.BlockSpec((1, HEAD_DIM, s), lambda i, j: (i, 0, 0)),
            pl.BlockSpec((1, IDX_HEADS, tq, LANES), lambda i, j: (i, 0, j, 0)),
            pl.BlockSpec((1, s, LANES), lambda i, j: (i, 0, 0)),
            pl.BlockSpec((1, SUBLANES, tq), lambda i, j: (i, 0, j)),
            pl.BlockSpec((tq, tq), lambda i, j: (0, 0)),
        ],
        out_specs=pl.BlockSpec((1, tq, GROUP_WIDTH), lambda i, j: (i, j, 0)),
        scratch_shapes=[
            pltpu.VMEM((s // tq, tq, tq), jnp.int32), pltpu.VMEM((s // tq, tq, tq), F32),
            pltpu.VMEM((HEADS, 1, tq), F32), pltpu.VMEM((HEADS, 1, tq), F32),
            pltpu.VMEM((HEADS, HEAD_DIM, tq), F32),
        ],
        compiler_params=_cparams(("arbitrary", "arbitrary")),
        name="sparse_indexer_attention",
    )(q, kv, vt, qi, kit, wt, tri)


ROW_CHUNKS = 8


def _store_rows(ref, val, r0=0):
    n = val.shape[0]
    for c in range(ROW_CHUNKS):
        ref[pl.ds(r0 * ROW_CHUNKS + c, n, stride=ROW_CHUNKS), :] = val[:, c * LANES:(c + 1) * LANES]


def _load_rows(ref, r0, n):
    return jnp.concatenate(
        [ref[pl.ds(r0 * ROW_CHUNKS + c, n, stride=ROW_CHUNKS), :] for c in range(ROW_CHUNKS)], axis=-1)


def _outstage_body(ya_ref, yb_ref, yc_ref, yd_ref, wo_ref, x_ref, g1_ref, sc_ref, sh_ref, ng_ref,
                   wr_ref, br_ref, tri_ref, xo_ref, h_ref, route_ref, gate_ref, cnt_ref, carry_ref):
    tm = x_ref.shape[0]
    gw = GROUP_WIDTH
    mix = sum(_dot(r[...], wo_ref[n * gw:(n + 1) * gw, :]) for n, r in enumerate((ya_ref, yb_ref, yc_ref, yd_ref)))
    x = x_ref[...] + g1_ref[0] * mix
    xo_ref[...] = x
    y = x * lax.rsqrt(jnp.mean(x * x, axis=-1, keepdims=True) + NORM_EPS) * ng_ref[...]
    h = y * (1.0 + sc_ref[0]) + sh_ref[0]
    _store_rows(h_ref, h)

    @pl.when(pl.program_id(0) == 0)
    def _():
        carry_ref[...] = jnp.zeros_like(carry_ref)

    logits = _dot_nt(wr_ref[...], h.astype(BF16)) + br_ref[...]
    e_iota = lax.broadcasted_iota(jnp.int32, (N_EXPERTS, tm), 0).astype(F32)
    vals, ids = [], []
    for _ in range(TOP_K):
        top = jnp.max(logits, axis=0, keepdims=True)
        idx = jnp.min(jnp.where(logits == top, e_iota, float(N_EXPERTS)), axis=0, keepdims=True)
        logits = jnp.where(e_iota == idx, -jnp.inf, logits)
        vals.append(top)
        ids.append(idx)
    exps = [jnp.exp(v - vals[0]) for v in vals]
    denom = sum(exps)
    onehot = sum(jnp.where(e_iota == idx, 1.0, 0.0) for idx in ids)
    before = carry_ref[...] + _dot(onehot.astype(BF16), tri_ref[...])
    ranks = [jnp.sum(jnp.where(e_iota == idx, before, 0.0), axis=0, keepdims=True) for idx in ids]
    carry_ref[...] += jnp.sum(onehot, axis=1, keepdims=True)
    route_ref[0] = jnp.concatenate(ids + ranks, axis=0).astype(jnp.int32)
    gate_ref[0] = jnp.concatenate([e / denom for e in exps] + [jnp.zeros_like(denom)] * TOP_K, axis=0)
    cnt_ref[...] = jnp.broadcast_to(carry_ref[...], cnt_ref.shape)


def _output_stage(ys, wo, x, g1, sc2, sh2, ng, wr_t, br, tri, tiles_per_batch):
    t, d = x.shape
    tm = TOK_TILE
    nt = t // tm
    row = lambda width: pl.BlockSpec((tm, width), lambda i: (i, 0))
    const = lambda shape: pl.BlockSpec(shape, lambda i: (0,) * len(shape))
    per_batch = pl.BlockSpec((1, 1, d), lambda i: (i // tiles_per_batch, 0, 0))
    return pl.pallas_call(
        _outstage_body,
        out_shape=(
            jax.ShapeDtypeStruct((t, d), F32), jax.ShapeDtypeStruct((t * ROW_CHUNKS, LANES), F32),
            jax.ShapeDtypeStruct((nt, ROUTE_ROWS, tm), jnp.int32), jax.ShapeDtypeStruct((nt, ROUTE_ROWS, tm), F32),
            jax.ShapeDtypeStruct((N_EXPERTS, LANES), F32),
        ),
        grid=(nt,),
        in_specs=[row(GROUP_WIDTH)] * 4 + [const((d, d)), row(d), per_batch, per_batch, per_batch, const((1, d)),
                                           const((N_EXPERTS, d)), const((N_EXPERTS, 1)), const((tm, tm))],
        out_specs=(row(d), pl.BlockSpec((tm * ROW_CHUNKS, LANES), lambda i: (i, 0)),
                   pl.BlockSpec((1, ROUTE_ROWS, tm), lambda i: (i, 0, 0)),
                   pl.BlockSpec((1, ROUTE_ROWS, tm), lambda i: (i, 0, 0)),
                   const((N_EXPERTS, LANES))),
        scratch_shapes=[pltpu.VMEM((N_EXPERTS, 1), F32)],
        compiler_params=_cparams(("arbitrary",)),
        name="mixer_output_stage_router",
    )(*ys, wo, x, g1, sc2, sh2, ng, wr_t, br, tri)


def _route_fetch(route_hbm, route_smem, rsem):
    i = pl.program_id(0)
    n = route_hbm.shape[1]
    slot = lax.rem(i, 2)

    def fetch(step, sl):
        dst = route_smem.at[pl.ds(pl.multiple_of(sl * n, n), n)]
        return pltpu.make_async_copy(route_hbm.at[step], dst, rsem.at[sl])

    @pl.when(i == 0)
    def _():
        fetch(0, 0).start()

    fetch(i, slot).wait()

    @pl.when(i + 1 < pl.num_programs(0))
    def _():
        fetch(i + 1, 1 - slot).start()

    return slot * n


def _dispatch_body(pstart_ref, pend_ref, route_hbm, h_ref, xs_hbm, route_smem, zero_ref, rsem, zsem, sem):
    tm = h_ref.shape[0]
    i = pl.program_id(0)

    @pl.when(i == 0)
    def _():
        zero_ref[...] = jnp.zeros_like(zero_ref)

        def fill(e):
            start = pl.multiple_of(pend_ref[e] - EXPERT_TILE, EXPERT_TILE)
            return pltpu.make_async_copy(zero_ref, xs_hbm.at[pl.ds(start, EXPERT_TILE)], zsem)

        def start(e, c):
            @pl.when(pend_ref[e] > pstart_ref[e])
            def _():
                fill(e).start()
            return c

        def wait(e, c):
            @pl.when(pend_ref[e] > pstart_ref[e])
            def _():
                fill(e).wait()
            return c

        lax.fori_loop(0, N_EXPERTS, start, 0)
        lax.fori_loop(0, N_EXPERTS, wait, 0)

    rec = _route_fetch(route_hbm, route_smem, rsem)
    for k in range(TOP_K):
        def start(j, c, base=rec + k * tm):
            for prio in range(2):
                t = 2 * j + prio
                pltpu.make_async_copy(h_ref.at[t], xs_hbm.at[route_smem[base + t]], sem).start(priority=prio)
            return c

        lax.fori_loop(0, tm // 2, start, 0, unroll=DMA_UNROLL // 2)
    for _ in range(TOP_K):
        pltpu.make_async_copy(h_ref, xs_hbm.at[pl.ds(0, tm)], sem).wait()


def _dispatch(pstart, pend, route, h, n_rows):
    t, dc, _ = h.shape
    tm = TOK_TILE
    return pl.pallas_call(
        _dispatch_body,
        out_shape=jax.ShapeDtypeStruct((n_rows, dc, LANES), F32),
        grid_spec=pltpu.PrefetchScalarGridSpec(
            num_scalar_prefetch=2,
            grid=(t // tm,),
            in_specs=[pl.BlockSpec(memory_space=pl.ANY), pl.BlockSpec((tm, dc, LANES), lambda i, ps, pe: (i, 0, 0))],
            out_specs=pl.BlockSpec(memory_space=pl.ANY),
            scratch_shapes=[
                pltpu.SMEM((2 * TOP_K * tm,), jnp.int32), pltpu.VMEM((EXPERT_TILE, dc, LANES), F32),
                pltpu.SemaphoreType.DMA((2,)), pltpu.SemaphoreType.DMA, pltpu.SemaphoreType.DMA,
            ],
        ),
        compiler_params=_cparams(("arbitrary",)),
        name="moe_dispatch",
    )(pstart, pend, route, h)


def _expert_body(tile_e_ref, n_used_ref, x_ref, wgu_ref, bgu_ref, wdn_ref, bdn_ref, y_ref):
    @pl.when(pl.program_id(0) < n_used_ref[0])
    def _():
        f = wdn_ref.shape[1]
        gu = _dot(_load_rows(x_ref, 0, EXPERT_TILE).astype(BF16), wgu_ref[0]) + bgu_ref[0]
        glu = jnp.minimum(gu[:, :f], SWIGLU_LIMIT)
        lin = jnp.clip(gu[:, f:], -SWIGLU_LIMIT, SWIGLU_LIMIT)
        act = glu * jax.nn.sigmoid(SWIGLU_ALPHA * glu) * (lin + 1.0)
        _store_rows(y_ref, _dot(act.astype(BF16), wdn_ref[0]) + bdn_ref[0])


def _experts(tile_e, n_used, xs, wgu, bgu, wdn, bdn):
    p, dc, _ = xs.shape
    assert dc == ROW_CHUNKS
    d = dc * LANES
    f = wdn.shape[1]
    tm = EXPERT_TILE
    used = lambda i, te, nu: jnp.minimum(i, nu[0] - 1)
    return pl.pallas_call(
        _expert_body,
        out_shape=jax.ShapeDtypeStruct((p * dc, LANES), F32),
        grid_spec=pltpu.PrefetchScalarGridSpec(
            num_scalar_prefetch=2,
            grid=(p // tm,),
            in_specs=[
                pl.BlockSpec((tm * dc, LANES), lambda i, te, nu: (used(i, te, nu), 0)),
                pl.BlockSpec((1, d, 2 * f), lambda i, te, nu: (te[i], 0, 0)),
                pl.BlockSpec((1, 1, 2 * f), lambda i, te, nu: (te[i], 0, 0)),
                pl.BlockSpec((1, f, d), lambda i, te, nu: (te[i], 0, 0)),
                pl.BlockSpec((1, 1, d), lambda i, te, nu: (te[i], 0, 0)),
            ],
            out_specs=pl.BlockSpec((tm * dc, LANES), lambda i, te, nu: (used(i, te, nu), 0)),
        ),
        compiler_params=_cparams(("arbitrary",)),
        name="moe_experts",
    )(tile_e, n_used, xs.reshape(p * dc, LANES), wgu, bgu, wdn, bdn).reshape(p, dc, LANES)


COMBINE_ROWS = 32


def _combine_body(final, route_hbm, ys_hbm, x_ref, g2_ref, gate_ref, fg_ref, o_ref,
                  route_smem, buf_ref, rsem, sem):
    tm = x_ref.shape[0]
    rec = _route_fetch(route_hbm, route_smem, rsem)

    def start(j, c):
        for prio in range(2):
            r = 2 * j + prio
            dst = buf_ref.at[pl.ds(pl.multiple_of(r * ROW_CHUNKS, ROW_CHUNKS), ROW_CHUNKS)]
            pltpu.make_async_copy(ys_hbm.at[route_smem[rec + r]], dst, sem).start(priority=prio)
        return c

    lax.fori_loop(0, TOP_K * tm // 2, start, 0, unroll=DMA_UNROLL // 2)
    pltpu.make_async_copy(buf_ref, buf_ref, sem).wait()
    g2 = g2_ref[0]
    fg = fg_ref[...]

    def rows(n, c):
        r0 = pl.multiple_of(n * COMBINE_ROWS, COMBINE_ROWS)
        gates = gate_ref[pl.ds(r0, COMBINE_ROWS), :]
        moe = sum(gates[:, k:k + 1] * _load_rows(buf_ref, k * tm + r0, COMBINE_ROWS) for k in range(TOP_K))
        x = x_ref[pl.ds(r0, COMBINE_ROWS), :] + g2 * moe
        if final:
            x = x * lax.rsqrt(jnp.mean(x * x, axis=-1, keepdims=True) + NORM_EPS) * fg
        o_ref[pl.ds(r0, COMBINE_ROWS), :] = x
        return c

    lax.fori_loop(0, tm // COMBINE_ROWS, rows, 0)


def _combine(route, ys, x, g2, gates_col, final_g, tiles_per_batch, final):
    t, d = x.shape
    tm = TOK_TILE
    return pl.pallas_call(
        functools.partial(_combine_body, final),
        out_shape=jax.ShapeDtypeStruct((t, d), F32),
        grid=(t // tm,),
        in_specs=[
            pl.BlockSpec(memory_space=pl.ANY), pl.BlockSpec(memory_space=pl.ANY),
            pl.BlockSpec((tm, d), lambda i: (i, 0)),
            pl.BlockSpec((1, 1, d), lambda i: (i // tiles_per_batch, 0, 0)),
            pl.BlockSpec((tm, ROUTE_ROWS), lambda i: (i, 0)),
            pl.BlockSpec((1, d), lambda i: (0, 0)),
        ],
        out_specs=pl.BlockSpec((tm, d), lambda i: (i, 0)),
        scratch_shapes=[
            pltpu.SMEM((2 * TOP_K * tm,), jnp.int32), pltpu.VMEM((TOP_K * tm * ROW_CHUNKS, LANES), F32),
            pltpu.SemaphoreType.DMA((2,)), pltpu.SemaphoreType.DMA,
        ],
        compiler_params=_cparams(("arbitrary",)),
        name="moe_combine",
    )(route, ys, x, g2, gates_col, final_g)


def _moe_plan(counts, n_tiles):
    counts = counts.astype(jnp.int32)
    tiles = (counts + EXPERT_TILE - 1) // EXPERT_TILE
    tile_end = jnp.cumsum(tiles)
    pend = tile_end * EXPERT_TILE
    pstart = pend - tiles * EXPERT_TILE
    n_used = tile_end[-1:]
    tile = jnp.minimum(jnp.arange(n_tiles), n_used - 1)
    tile_e = jnp.sum((tile_end[None, :] <= tile[:, None]).astype(jnp.int32), axis=1)
    return pstart, pend, tile_e, n_used


def _tri(n, rel):
    r = jnp.arange(n)[:, None]
    c = jnp.arange(n)[None, :]
    return rel(r, c).astype(BF16)


def _constants(s):
    return dict(
        tab=_rope_tables(s),
        tri_le=_tri(ATT_TILE, lambda r, c: c <= r),
        tri_gt=_tri(ATT_TILE, lambda r, c: r > c),
        tri_lt=_tri(ATT_TILE, lambda r, c: r < c),
        tri_lt_tok=_tri(TOK_TILE, lambda r, c: r < c),
    )


def _prep_layer(p, l):
    bf = jnp.zeros((1, LANES), F32).at[0, MISC_FA:MISC_FA + HEADS].set(p["b_forget"][l].astype(F32))
    w, wt = _reorder_w_in(p["w_in"][l])
    return dict(
        w_in=w, w_in_t=wt,
        wkv=jnp.concatenate([p["w_uk"][l], p["w_uv"][l]], axis=-1).astype(BF16),
        wuvt=p["w_uv"][l].T.astype(BF16),
        kvg=p["kv_norm_g"][l].reshape(1, KV_RANK).astype(F32),
        bf=bf,
        g1=p["norm1_g"][l].reshape(1, -1).astype(F32),
        chunk_bias=_chunk_bias_table(p["rel_bias"][l], ATT_TILE),
    )


def _mixers(x, sc1, sh1, lp, cst):
    tq = ATT_TILE
    (fq, fk, fvt, dq, dkv, dvt, dqi, dkit, dwt, cq, ck, cvt, eq, ek, ev) = _input_stage(
        x, sc1, sh1, lp["g1"], lp["w_in"], lp["w_in_t"], lp["wkv"], lp["wuvt"], lp["kvg"], lp["bf"],
        cst["tri_le"], cst["tab"], tq)
    ya = _fox_attention(fq, fk, fvt, tq)
    yb = _dsa_attention(dq, dkv, dvt, dqi, dkit, dwt, cst["tri_gt"], tq)
    yc = _sb_attention(cq, ck, cvt, cst["tri_lt"], tq)
    yd = _chunk_attention(eq, ek, ev, lp["chunk_bias"], tq)
    return ya, yb, yc, yd


def _moe(p, l, x, h, route, gate, counts, g2, tiles_per_batch, final):
    t, d = x.shape
    nt = t // TOK_TILE
    n_tiles = t * TOP_K // EXPERT_TILE + N_EXPERTS
    pstart, pend, tile_e, n_used = _moe_plan(counts, n_tiles)
    ids, ranks = route[:, :TOP_K, :], route[:, TOP_K:, :]
    first_row = sum(jnp.where(ids == e, pstart[e], 0) for e in range(N_EXPERTS))
    route = (first_row + ranks).reshape(nt, TOP_K * TOK_TILE)
    xs = _dispatch(pstart, pend, route, h.reshape(t, ROW_CHUNKS, LANES), n_tiles * EXPERT_TILE)
    ys = _experts(tile_e, n_used, xs,
                  p["w_gu"][l].astype(BF16), p["b_gu"][l][:, None, :].astype(F32),
                  p["w_dn"][l].astype(BF16), p["b_dn"][l][:, None, :].astype(F32))
    gates_col = gate.transpose(0, 2, 1).reshape(t, ROUTE_ROWS)
    return _combine(route, ys, x, g2, gates_col, p["final_g"].reshape(1, d).astype(F32), tiles_per_batch, final)


def kernel(x, c, w_ada, b_ada, norm1_g, norm2_g, w_in, b_forget, kv_norm_g, w_uk, w_uv, rel_bias, w_o,
           w_router, b_router, w_gu, b_gu, w_dn, b_dn, final_g):
    p = dict(w_in=w_in, b_forget=b_forget, kv_norm_g=kv_norm_g, w_uk=w_uk, w_uv=w_uv, rel_bias=rel_bias,
             norm1_g=norm1_g, w_gu=w_gu, b_gu=b_gu, w_dn=w_dn, b_dn=b_dn, final_g=final_g)
    b, s, d = x.shape
    t = b * s
    depth = w_ada.shape[0]
    tiles_per_batch = s // TOK_TILE
    mod = _modulation(c, w_ada, b_ada)
    cst = _constants(s)
    for l in range(depth):
        sh1, sc1, g1, sh2, sc2, g2 = [m[:, None, :] for m in jnp.split(mod[l], 6, axis=-1)]
        ys = _mixers(x, sc1, sh1, _prep_layer(p, l), cst)
        x2, h, route, gate, cnt = _output_stage(
            [y.reshape(t, GROUP_WIDTH) for y in ys], w_o[l].astype(BF16), x.reshape(t, d), g1, sc2, sh2,
            norm2_g[l].reshape(1, d).astype(F32), w_router[l].T.astype(BF16),
            b_router[l].reshape(N_EXPERTS, 1).astype(F32), cst["tri_lt_tok"], tiles_per_batch)
        x = _moe(p, l, x2, h, route, gate, cnt[:, 0], g2, tiles_per_batch, l == depth - 1).reshape(b, s, d)
    return x
```

```python
import functools
import math

import jax
import jax.numpy as jnp
from jax import lax
from jax.experimental import pallas as pl
from jax.experimental.pallas import tpu as pltpu

F32 = jnp.float32
BF16 = jnp.bfloat16

HEAD_DIM = 64
HEADS = 4
GROUP_WIDTH = HEADS * HEAD_DIM
CHUNK = 64
ROPE_THETA = 10000.0
NORM_EPS = 1e-6
NEG_INF = -1e30
KV_RANK = 128
IDX_HEADS = 4
IDX_DIM = 32
TOPK_MAX = 256
LEFT_CHUNKS = 8
MAX_REL = 128
N_EXPERTS = 32
TOP_K = 4
SWIGLU_ALPHA = 1.702
SWIGLU_LIMIT = 7.0
LOG2E = math.log2(math.e)
INT_MIN = -2 ** 31

LANES = 128
SUBLANES = 8
VMEM_LIMIT = 48 * 1024 * 1024

ATT_TILE = 256
TOK_TILE = 512
EXPERT_TILE = 512
ROUTE_ROWS = 2 * TOP_K
DMA_UNROLL = 8

COL_A = 0
COL_C = 3 * GROUP_WIDTH
COL_D = 6 * GROUP_WIDTH
COL_QB = 9 * GROUP_WIDTH
COL_LAT = COL_QB + GROUP_WIDTH
COL_QI = COL_LAT + KV_RANK
COL_MISC = COL_QI + IDX_HEADS * IDX_DIM
N_COLS = COL_MISC + LANES
MISC_WI = IDX_DIM
MISC_FA = IDX_DIM + IDX_HEADS
ROW_VA = 0
ROW_VC = GROUP_WIDTH
ROW_WI = 2 * GROUP_WIDTH
N_ROWS_T = ROW_WI + SUBLANES


def _cparams(sem):
    return pltpu.CompilerParams(dimension_semantics=sem, vmem_limit_bytes=VMEM_LIMIT)


def _split3(x):
    hi = x.astype(BF16)
    r = x - hi.astype(F32)
    mid = r.astype(BF16)
    lo = (r - mid.astype(F32)).astype(BF16)
    return hi, mid, lo


def _dot(a, b):
    return jnp.dot(a, b, preferred_element_type=F32)


def _dot_nt(a, b):
    return lax.dot_general(a, b, (((1,), (1,)), ((), ())), preferred_element_type=F32)


def _mod_body(c_ref, w_ref, b_ref, o_ref):
    c = c_ref[...]
    act = (c * jax.nn.sigmoid(c)).astype(BF16)
    o_ref[0] = _dot(act, w_ref[0].astype(BF16)) + b_ref[0]


def _modulation(c, w_ada, b_ada):
    depth, d, n = w_ada.shape
    b = c.shape[0]
    tn = 1024
    return pl.pallas_call(
        _mod_body,
        out_shape=jax.ShapeDtypeStruct((depth, b, n), F32),
        grid=(depth, n // tn),
        in_specs=[
            pl.BlockSpec((b, d), lambda l, j: (0, 0)),
            pl.BlockSpec((1, d, tn), lambda l, j: (l, 0, j)),
            pl.BlockSpec((1, 1, tn), lambda l, j: (l, 0, j)),
        ],
        out_specs=pl.BlockSpec((1, b, tn), lambda l, j: (l, 0, j)),
        compiler_params=_cparams(("arbitrary", "arbitrary")),
        name="adaln_mod",
    )(c, w_ada, b_ada.reshape(depth, 1, n))


def _rope_tables(s):
    pos = jnp.arange(s, dtype=F32)[:, None]
    lane = jnp.arange(LANES)[None, :]

    def tables(dim, active):
        half = dim // 2
        j = lane % dim
        inv = ROPE_THETA ** (-(j % half).astype(F32) / half)
        ang = pos * inv
        cos = jnp.where(active, jnp.cos(ang), 1.0)
        sin = jnp.where(active, jnp.sin(ang), 0.0)
        first = j < half
        return [cos, jnp.where(first, -sin, 0.0), jnp.where(first, 0.0, sin)]

    everywhere = lane >= 0
    t = (tables(HEAD_DIM, everywhere) + tables(HEAD_DIM, lane < HEAD_DIM)
         + tables(IDX_DIM, everywhere) + tables(IDX_DIM, lane < IDX_DIM))
    return jnp.stack(t).astype(F32)


def _rope(x, tab, k, half):
    cos, sin_a, sin_b = tab[3 * k], tab[3 * k + 1], tab[3 * k + 2]
    return x * cos + pltpu.roll(x, LANES - half, 1) * sin_a + pltpu.roll(x, half, 1) * sin_b


def _instage_body(x_ref, sc_ref, sh_ref, g_ref, w_ref, wt_ref, wkv_ref, wuvt_ref, kvg_ref, bf_ref, tri_ref, tab_ref,
                  fq_o, fk_o, fvt_o, dq_o, dkv_o, dvt_o, dqi_o, dkit_o, dwt_o,
                  cq_o, ck_o, cvt_o, eq_o, ek_o, ev_o, carry_ref):
    tm = x_ref.shape[1]
    x = x_ref[0]
    y = x * lax.rsqrt(jnp.mean(x * x, axis=-1, keepdims=True) + NORM_EPS) * g_ref[...]
    h = (y * (1.0 + sc_ref[0]) + sh_ref[0]).astype(BF16)

    def proj(c0, n):
        return _dot(h, w_ref[:, c0:c0 + n])

    def proj_t(r0, n):
        return _dot_nt(wt_ref[r0:r0 + n, :], h)

    def heads_out(val, refs, scale):
        for i, ref in enumerate(refs):
            for hd in range(HEADS):
                c0 = i * GROUP_WIDTH + hd * HEAD_DIM
                piece = val[:, c0:c0 + HEAD_DIM]
                if i == 0:
                    piece = piece * scale
                ref[0, hd] = piece.astype(ref.dtype)

    def heads_out_t(val, ref):
        for hd in range(HEADS):
            ref[0, hd] = val[hd * HEAD_DIM:(hd + 1) * HEAD_DIM, :].astype(ref.dtype)

    scale = HEAD_DIM ** -0.5
    heads_out(proj(COL_C, 2 * GROUP_WIDTH), (cq_o, ck_o), scale)
    heads_out_t(proj_t(ROW_VC, GROUP_WIDTH), cvt_o)
    heads_out(proj(COL_D, 3 * GROUP_WIDTH), (eq_o, ek_o, ev_o), scale)

    misc = _rope(proj(COL_MISC, LANES), tab_ref, 3, IDX_DIM // 2)
    lane = lax.broadcasted_iota(jnp.int32, (tm, LANES), 1)
    ki = jnp.where(lane < IDX_DIM, misc, 0.0)
    kit = ki + pltpu.roll(ki, IDX_DIM, 1) + pltpu.roll(ki, 2 * IDX_DIM, 1) + pltpu.roll(ki, 3 * IDX_DIM, 1)
    dkit_o[0] = kit.astype(BF16)
    dwt_o[0] = proj_t(ROW_WI, SUBLANES) * ((IDX_HEADS ** -0.5) * (IDX_DIM ** -0.5))
    qi = _rope(proj(COL_QI, LANES), tab_ref, 2, IDX_DIM // 2)
    for hd in range(IDX_HEADS):
        in_head = (lane >= hd * IDX_DIM) & (lane < (hd + 1) * IDX_DIM)
        dqi_o[0, hd] = jnp.where(in_head, qi, 0.0).astype(BF16)
    lat = proj(COL_LAT, KV_RANK)
    lat = (lat * lax.rsqrt(jnp.mean(lat * lat, axis=-1, keepdims=True) + NORM_EPS) * kvg_ref[...]).astype(BF16)
    dkv_o[0] = _rope(_dot(lat, wkv_ref[...]), tab_ref, 1, HEAD_DIM // 2).astype(BF16)
    dvt_o[0] = _dot_nt(wuvt_ref[...], lat).astype(BF16)
    qb = proj(COL_QB, GROUP_WIDTH)
    zeros64 = jnp.zeros((tm, HEAD_DIM), F32)
    for half_i in range(2):
        r = _rope(qb[:, half_i * LANES:(half_i + 1) * LANES], tab_ref, 0, HEAD_DIM // 2) * (scale * LOG2E)
        for j in range(2):
            piece = jnp.concatenate([r[:, j * HEAD_DIM:(j + 1) * HEAD_DIM], zeros64], axis=-1)
            dq_o[0, 2 * half_i + j] = piece.astype(BF16)

    @pl.when(pl.program_id(1) == 0)
    def _():
        carry_ref[...] = jnp.zeros_like(carry_ref)

    z = misc + bf_ref[...]
    log_f = jnp.minimum(z, 0.0) - jnp.log1p(jnp.exp(-jnp.abs(z)))
    tri = tri_ref[...]
    f_cum = carry_ref[...] + sum(_dot(tri, p) for p in _split3(log_f))
    carry_ref[...] = f_cum[tm - 1:tm, :]
    f_cum = f_cum * LOG2E
    pa = proj(COL_A, 2 * GROUP_WIDTH)
    heads_out_t(proj_t(ROW_VA, GROUP_WIDTH), fvt_o)
    lane64 = lax.broadcasted_iota(jnp.int32, (tm, HEAD_DIM), 1)
    for hd in range(HEADS):
        fh = f_cum[:, MISC_FA + hd:MISC_FA + hd + 1]
        f3 = [jnp.broadcast_to(p.astype(F32), (tm, HEAD_DIM)) for p in _split3(fh)]
        one = jnp.where(lane64 < 6, 1.0, 0.0)
        q_ext = jnp.where(lane64 == 0, f3[0], jnp.where(lane64 == 1, f3[1], jnp.where(lane64 == 2, f3[2], one)))
        k_ext = jnp.where(lane64 == 3, -f3[0], jnp.where(lane64 == 4, -f3[1], jnp.where(lane64 == 5, -f3[2], one)))
        q = pa[:, hd * HEAD_DIM:(hd + 1) * HEAD_DIM] * (scale * LOG2E)
        k = pa[:, GROUP_WIDTH + hd * HEAD_DIM:GROUP_WIDTH + (hd + 1) * HEAD_DIM]
        fq_o[0, hd] = jnp.concatenate([q, q_ext], axis=-1).astype(BF16)
        fk_o[0, hd] = jnp.concatenate([k, k_ext], axis=-1).astype(BF16)


def _reorder_w_in(w_in):
    gw = GROUP_WIDTH
    a0 = 0
    b0 = 3 * gw + HEADS
    c0 = b0 + gw + KV_RANK + IDX_HEADS * IDX_DIM + IDX_DIM + IDX_HEADS
    d0 = c0 + 3 * gw
    lat0 = b0 + gw
    qi0 = lat0 + KV_RANK
    ki0 = qi0 + IDX_HEADS * IDX_DIM
    wi0 = ki0 + IDX_DIM
    pad = jnp.zeros(w_in.shape[:-1] + (LANES - IDX_DIM - IDX_HEADS - HEADS,), w_in.dtype)
    w = jnp.concatenate([
        w_in[:, a0:a0 + 3 * gw], w_in[:, c0:c0 + 3 * gw], w_in[:, d0:d0 + 3 * gw],
        w_in[:, b0:b0 + gw], w_in[:, lat0:lat0 + KV_RANK], w_in[:, qi0:qi0 + IDX_HEADS * IDX_DIM],
        w_in[:, ki0:ki0 + IDX_DIM], w_in[:, wi0:wi0 + IDX_HEADS], w_in[:, 3 * gw:3 * gw + HEADS], pad,
    ], axis=-1)
    wt = jnp.concatenate([
        w_in[:, a0 + 2 * gw:a0 + 3 * gw], w_in[:, c0 + 2 * gw:c0 + 3 * gw], w_in[:, wi0:wi0 + IDX_HEADS],
        jnp.zeros((w_in.shape[0], SUBLANES - IDX_HEADS), w_in.dtype),
    ], axis=-1).T
    return w.astype(BF16), wt.astype(BF16)


def _input_stage(x, sc, sh, g, w, wt, wkv, wuvt, kvg, bf, tri, tab, tm):
    b, s, d = x.shape
    hq = lambda width: jax.ShapeDtypeStruct((b, HEADS, s, width), BF16)
    hqt = jax.ShapeDtypeStruct((b, HEADS, HEAD_DIM, s), BF16)
    flat = jax.ShapeDtypeStruct((b, s, LANES), BF16)
    out_shape = (
        hq(LANES), hq(LANES), hqt,
        hq(LANES), flat, jax.ShapeDtypeStruct((b, HEAD_DIM, s), BF16),
        hq(LANES), flat, jax.ShapeDtypeStruct((b, SUBLANES, s), F32),
        hq(HEAD_DIM), hq(HEAD_DIM), hqt,
        hq(HEAD_DIM), hq(HEAD_DIM), hq(HEAD_DIM),
    )
    hspec = lambda width: pl.BlockSpec((1, HEADS, tm, width), lambda i, j: (i, 0, j, 0))
    htspec = pl.BlockSpec((1, HEADS, HEAD_DIM, tm), lambda i, j: (i, 0, 0, j))
    fspec = pl.BlockSpec((1, tm, LANES), lambda i, j: (i, j, 0))
    tspec = lambda rows: pl.BlockSpec((1, rows, tm), lambda i, j: (i, 0, j))
    const2 = lambda shape: pl.BlockSpec(shape, lambda i, j: (0, 0))
    out_specs = (
        hspec(LANES), hspec(LANES), htspec,
        hspec(LANES), fspec, tspec(HEAD_DIM),
        hspec(LANES), fspec, tspec(SUBLANES),
        hspec(HEAD_DIM), hspec(HEAD_DIM), htspec,
        hspec(HEAD_DIM), hspec(HEAD_DIM), hspec(HEAD_DIM),
    )
    return pl.pallas_call(
        _instage_body,
        out_shape=out_shape,
        grid=(b, s // tm),
        in_specs=[
            pl.BlockSpec((1, tm, d), lambda i, j: (i, j, 0)),
            pl.BlockSpec((1, 1, d), lambda i, j: (i, 0, 0)),
            pl.BlockSpec((1, 1, d), lambda i, j: (i, 0, 0)),
            const2((1, d)),
            const2((d, N_COLS)),
            const2((N_ROWS_T, d)),
            const2((KV_RANK, LANES)),
            const2((HEAD_DIM, KV_RANK)),
            const2((1, KV_RANK)),
            const2((1, LANES)),
            const2((tm, tm)),
            pl.BlockSpec((12, tm, LANES), lambda i, j: (0, j, 0)),
        ],
        out_specs=out_specs,
        scratch_shapes=[pltpu.VMEM((1, LANES), F32)],
        compiler_params=_cparams(("arbitrary", "arbitrary")),
        name="mixer_input_stage",
    )(x, sc, sh, g, w, wt, wkv, wuvt, kvg, bf, tri, tab)


def _key_query_iota(t):
    return lax.broadcasted_iota(jnp.int32, (t, t), 0), lax.broadcasted_iota(jnp.int32, (t, t), 1)


def _softmax2_step(s, vt, m, l, acc):
    m_new = jnp.maximum(m, jnp.max(s, axis=0, keepdims=True))
    alpha = jnp.exp2(m - m_new)
    p = jnp.exp2(s - m_new)
    l_new = alpha * l + jnp.sum(p, axis=0, keepdims=True)
    acc_new = alpha * acc + _dot(vt, p.astype(BF16))
    return m_new, l_new, acc_new


def _skewed(n, stages):
    vals = [None] * n
    for step in range(n + len(stages) - 1):
        for j, stage in enumerate(stages):
            i = step - j
            if 0 <= i < n:
                vals[i] = stage(i, vals[i])


def _heads_to_rows(accs):
    return jnp.concatenate([a.T for a in accs], axis=-1)


def _fox_body(q_ref, k_ref, vt_ref, o_ref, m_ref, l_ref, acc_ref):
    qi = pl.program_id(1)
    tq = q_ref.shape[2]
    key, qry = _key_query_iota(tq)
    m_ref[...] = jnp.full_like(m_ref, NEG_INF)
    l_ref[...] = jnp.zeros_like(l_ref)
    acc_ref[...] = jnp.zeros_like(acc_ref)

    def block(kb, masked):
        k0 = pl.multiple_of(kb * tq, tq)

        def scores(hd, _):
            s = _dot_nt(k_ref[0, hd, pl.ds(k0, tq), :], q_ref[0, hd])
            return jnp.where(key <= qry, s, NEG_INF) if masked else s

        def update(hd, s):
            m_ref[hd], l_ref[hd], acc_ref[hd] = _softmax2_step(
                s, vt_ref[0, hd, :, pl.ds(k0, tq)], m_ref[hd], l_ref[hd], acc_ref[hd])

        _skewed(HEADS, [scores, update])

    def loop_body(kb, carry):
        block(kb, False)
        return carry

    lax.fori_loop(0, qi, loop_body, 0)
    block(qi, True)
    o_ref[0] = _heads_to_rows([acc_ref[hd] / l_ref[hd] for hd in range(HEADS)]).astype(o_ref.dtype)


def _fox_attention(q, k, vt, tq):
    b, _, s, _ = q.shape
    return pl.pallas_call(
        _fox_body,
        out_shape=jax.ShapeDtypeStruct((b, s, GROUP_WIDTH), BF16),
        grid=(b, s // tq),
        in_specs=[
            pl.BlockSpec((1, HEADS, tq, LANES), lambda i, j: (i, 0, j, 0)),
            pl.BlockSpec((1, HEADS, s, LANES), lambda i, j: (i, 0, 0, 0)),
            pl.BlockSpec((1, HEADS, HEAD_DIM, s), lambda i, j: (i, 0, 0, 0)),
        ],
        out_specs=pl.BlockSpec((1, tq, GROUP_WIDTH), lambda i, j: (i, j, 0)),
        scratch_shapes=[pltpu.VMEM((HEADS, 1, tq), F32), pltpu.VMEM((HEADS, 1, tq), F32),
                        pltpu.VMEM((HEADS, HEAD_DIM, tq), F32)],
        compiler_params=_cparams(("arbitrary", "arbitrary")),
        name="forgetting_attention",
    )(q, k, vt)


def _sb_body(q_ref, k_ref, vt_ref, tri_ref, o_ref, r_ref, acc_ref):
    qi = pl.program_id(1)
    tq = q_ref.shape[2]
    key, qry = _key_query_iota(tq)
    tri = tri_ref[...]
    r_ref[...] = jnp.zeros_like(r_ref)
    acc_ref[...] = jnp.zeros_like(acc_ref)

    def block(kb, masked):
        k0 = pl.multiple_of(kb * tq, tq)

        def logits(hd, _):
            return _dot_nt(k_ref[0, hd, pl.ds(k0, tq), :], q_ref[0, hd])

        def later_sums(hd, z):
            log_1m = -(jnp.maximum(z, 0.0) + jnp.log1p(jnp.exp(-jnp.abs(z))))
            if masked:
                log_1m = jnp.where(key < qry, log_1m, 0.0)
            after = r_ref[hd] + _dot(tri, log_1m.astype(BF16))
            return z, log_1m, after

        def accumulate(hd, val):
            z, log_1m, after = val
            w = jnp.exp(log_1m + z + after)
            if masked:
                w = jnp.where(key < qry, w, 0.0)
            acc_ref[hd] += _dot(vt_ref[0, hd, :, pl.ds(k0, tq)], w.astype(BF16))
            r_ref[hd] += jnp.sum(log_1m, axis=0, keepdims=True)

        _skewed(HEADS, [logits, later_sums, accumulate])

    block(qi, True)

    def loop_body(n, carry):
        block(qi - 1 - n, False)
        return carry

    lax.fori_loop(0, qi, loop_body, 0)
    o_ref[0] = _heads_to_rows([acc_ref[hd] for hd in range(HEADS)]).astype(o_ref.dtype)


def _sb_attention(q, k, vt, tri, tq):
    b, _, s, _ = q.shape
    return pl.pallas_call(
        _sb_body,
        out_shape=jax.ShapeDtypeStruct((b, s, GROUP_WIDTH), BF16),
        grid=(b, s // tq),
        in_specs=[
            pl.BlockSpec((1, HEADS, tq, HEAD_DIM), lambda i, j: (i, 0, j, 0)),
            pl.BlockSpec((1, HEADS, s, HEAD_DIM), lambda i, j: (i, 0, 0, 0)),
            pl.BlockSpec((1, HEADS, HEAD_DIM, s), lambda i, j: (i, 0, 0, 0)),
            pl.BlockSpec((tq, tq), lambda i, j: (0, 0)),
        ],
        out_specs=pl.BlockSpec((1, tq, GROUP_WIDTH), lambda i, j: (i, j, 0)),
        scratch_shapes=[pltpu.VMEM((HEADS, 1, tq), F32), pltpu.VMEM((HEADS, HEAD_DIM, tq), F32)],
        compiler_params=_cparams(("arbitrary", "arbitrary")),
        name="stick_breaking_attention",
    )(q, k, vt, tri)


def _chunk_bias_table(rel_table, tq):
    left = LEFT_CHUNKS * CHUNK
    width = left + tq
    d = jnp.arange(width + tq - 1) - (tq - 1) - left
    diag = rel_table[:, jnp.clip(d, -MAX_REL, MAX_REL) + MAX_REL].astype(F32)
    bias = jnp.stack([diag[:, tq - 1 - t:tq - 1 - t + width] for t in range(tq)], axis=1)
    t = jnp.arange(tq)[:, None]
    j = jnp.arange(width)[None, :]
    c0 = (t // CHUNK) * CHUNK
    visible = (j >= c0) & (j - left < c0 + CHUNK)
    return jnp.where(visible[None], bias, NEG_INF)


def _chunk_body(nb, q_ref, *refs):
    k_refs, v_refs = refs[:nb], refs[nb:2 * nb]
    bias_ref, o_ref = refs[2 * nb], refs[2 * nb + 1]
    qi = pl.program_id(1)
    outs = []
    for hd in range(HEADS):
        q = q_ref[0, hd]
        s_parts = []
        for n in range(nb):
            s = _dot_nt(q, k_refs[n][0, hd])
            s_parts.append(jnp.where(qi - (nb - 1) + n >= 0, s, NEG_INF))
        s = jnp.concatenate(s_parts, axis=-1) + bias_ref[hd]
        m = jnp.max(s, axis=-1, keepdims=True)
        p = jnp.exp(s - m)
        l = jnp.sum(p, axis=-1, keepdims=True)
        p = p.astype(BF16)
        tq = q.shape[0]
        acc = sum(_dot(p[:, n * tq:(n + 1) * tq], v_refs[n][0, hd]) for n in range(nb))
        outs.append(acc / l)
    o_ref[0] = jnp.concatenate(outs, axis=-1).astype(o_ref.dtype)


def _chunk_attention(q, k, v, bias, tq):
    b, _, s, _ = q.shape
    left = LEFT_CHUNKS * CHUNK
    assert left % tq == 0
    nb = left // tq + 1
    kv_specs = [
        pl.BlockSpec((1, HEADS, tq, HEAD_DIM), functools.partial(
            lambda i, j, n: (i, 0, jnp.maximum(j - (nb - 1) + n, 0), 0), n=n))
        for n in range(nb)
    ]
    return pl.pallas_call(
        functools.partial(_chunk_body, nb),
        out_shape=jax.ShapeDtypeStruct((b, s, GROUP_WIDTH), BF16),
        grid=(b, s // tq),
        in_specs=[pl.BlockSpec((1, HEADS, tq, HEAD_DIM), lambda i, j: (i, 0, j, 0))] + kv_specs + kv_specs
        + [pl.BlockSpec((HEADS, tq, nb * tq), lambda i, j: (0, 0, 0))],
        out_specs=pl.BlockSpec((1, tq, GROUP_WIDTH), lambda i, j: (i, j, 0)),
        compiler_params=_cparams(("arbitrary", "arbitrary")),
        name="chunked_relbias_attention",
    )(q, *([k] * nb), *([v] * nb), bias)


def _dsa_body(n_sel, q_ref, kv_ref, vt_ref, qi_ref, kit_ref, wt_ref, tri_ref, o_ref,
              key_ref, bias_ref, m_ref, l_ref, acc_ref):
    i = pl.program_id(1)
    tq = q_ref.shape[2]
    nkb = i + 1
    key_pos, qry_pos = _key_query_iota(tq)
    vis_diag = key_pos < (qry_pos // CHUNK + 1) * CHUNK
    wt = wt_ref[0]

    def score_block(kb, masked):
        k0 = pl.multiple_of(kb * tq, tq)
        kit = kit_ref[0, pl.ds(k0, tq), :]
        dots = [_dot_nt(kit, qi_ref[0, hd]) for hd in range(IDX_HEADS)]
        score = sum(wt[hd:hd + 1, :] * jnp.maximum(dots[hd], 0.0) for hd in range(IDX_HEADS))
        score = jnp.where(score == 0.0, 0.0, score)
        bits = pltpu.bitcast(score, jnp.int32)
        okey = jnp.where(bits < 0, bits ^ 0x7FFFFFFF, bits)
        if masked:
            okey = jnp.where(vis_diag, okey, INT_MIN)
        key_ref[kb] = okey

    def score_loop(kb, carry):
        score_block(kb, False)
        return carry

    lax.fori_loop(0, i, score_loop, 0)
    score_block(i, True)

    def count(pred):
        def body(kb, acc):
            ind = jnp.where(pred(key_ref[kb]), 1.0, 0.0)
            return acc + ind.reshape(tq // SUBLANES, SUBLANES, tq).sum(axis=0)
        acc = lax.fori_loop(0, nkb, body, jnp.zeros((SUBLANES, tq), F32))
        return jnp.sum(acc, axis=0, keepdims=True)

    def bit_step(it, ans_u):
        cand_u = ans_u | lax.shift_left(jnp.int32(1), 31 - it)
        cand = cand_u ^ INT_MIN
        return jnp.where(count(lambda k: k >= cand) >= n_sel, cand_u, ans_u)

    thr = lax.fori_loop(0, 32, bit_step, jnp.zeros((1, tq), jnp.int32)) ^ INT_MIN

    n_gt = count(lambda k: k > thr)
    need = jnp.where(thr != INT_MIN, n_sel - n_gt, 0.0)
    tri = tri_ref[...]

    def select_block(kb, seen):
        okey = key_ref[kb]
        eq = okey == thr
        eq_f = jnp.where(eq, 1.0, 0.0)
        earlier = seen + _dot(tri, eq_f.astype(BF16))
        tie_bias = jnp.where(eq, jnp.where(earlier < need, 0.0, NEG_INF), NEG_INF)
        bias_ref[kb] = jnp.where(okey > thr, 0.0, tie_bias)
        return seen + jnp.sum(eq_f, axis=0, keepdims=True)

    lax.fori_loop(0, nkb, select_block, jnp.zeros((1, tq), F32))

    m_ref[...] = jnp.full_like(m_ref, NEG_INF)
    l_ref[...] = jnp.zeros_like(l_ref)
    acc_ref[...] = jnp.zeros_like(acc_ref)

    def attend(kb, carry):
        k0 = pl.multiple_of(kb * tq, tq)

        def scores(hd, _):
            return _dot_nt(kv_ref[0, pl.ds(k0, tq), :], q_ref[0, hd]) + bias_ref[kb]

        def update(hd, s):
            m_ref[hd], l_ref[hd], acc_ref[hd] = _softmax2_step(
                s, vt_ref[0, :, pl.ds(k0, tq)], m_ref[hd], l_ref[hd], acc_ref[hd])

        _skewed(HEADS, [scores, update])
        return carry

    lax.fori_loop(0, nkb, attend, 0)
    o_ref[0] = _heads_to_rows([acc_ref[hd] / l_ref[hd] for hd in range(HEADS)]).astype(o_ref.dtype)


def _dsa_attention(q, kv, vt, qi, kit, wt, tri, tq):
    b, _, s, _ = q.shape
    n_sel = min(TOPK_MAX, s // 4)
    return pl.pallas_call(
        functools.partial(_dsa_body, n_sel),
        out_shape=jax.ShapeDtypeStruct((b, s, GROUP_WIDTH), BF16),
        grid=(b, s // tq),
        in_specs=[
            pl.BlockSpec((1, HEADS, tq, LANES), lambda i, j: (i, 0, j, 0)),
            pl.BlockSpec((1, s, LANES), lambda i, j: (i, 0, 0)),
            pl.BlockSpec((1, HEAD_DIM, s), lambda i, j: (i, 0, 0)),
            pl.BlockSpec((1, IDX_HEADS, tq, LANES), lambda i, j: (i, 0, j, 0)),
            pl.BlockSpec((1, s, LANES), lambda i, j: (i, 0, 0)),
            pl.BlockSpec((1, SUBLANES, tq), lambda i, j: (i, 0, j)),
            pl.BlockSpec((tq, tq), lambda i, j: (0, 0)),
        ],
        out_specs=pl.BlockSpec((1, tq, GROUP_WIDTH), lambda i, j: (i, j, 0)),
        scratch_shapes=[
            pltpu.VMEM((s // tq, tq, tq), jnp.int32), pltpu.VMEM((s // tq, tq, tq), F32),
            pltpu.VMEM((HEADS, 1, tq), F32), pltpu.VMEM((HEADS, 1, tq), F32),
            pltpu.VMEM((HEADS, HEAD_DIM, tq), F32),
        ],
        compiler_params=_cparams(("arbitrary", "arbitrary")),
        name="sparse_indexer_attention",
    )(q, kv, vt, qi, kit, wt, tri)


ROW_CHUNKS = 8


def _store_rows(ref, val, r0=0):
    n = val.shape[0]
    for c in range(ROW_CHUNKS):
        ref[pl.ds(r0 * ROW_CHUNKS + c, n, stride=ROW_CHUNKS), :] = val[:, c * LANES:(c + 1) * LANES]


def _load_rows(ref, r0, n):
    return jnp.concatenate(
        [ref[pl.ds(r0 * ROW_CHUNKS + c, n, stride=ROW_CHUNKS), :] for c in range(ROW_CHUNKS)], axis=-1)


def _outstage_body(ya_ref, yb_ref, yc_ref, yd_ref, wo_ref, x_ref, g1_ref, sc_ref, sh_ref, ng_ref,
                   wr_ref, br_ref, tri_ref, xo_ref, h_ref, route_ref, gate_ref, cnt_ref, carry_ref):
    tm = x_ref.shape[0]
    gw = GROUP_WIDTH
    mix = sum(_dot(r[...], wo_ref[n * gw:(n + 1) * gw, :]) for n, r in enumerate((ya_ref, yb_ref, yc_ref, yd_ref)))
    x = x_ref[...] + g1_ref[0] * mix
    xo_ref[...] = x
    y = x * lax.rsqrt(jnp.mean(x * x, axis=-1, keepdims=True) + NORM_EPS) * ng_ref[...]
    h = y * (1.0 + sc_ref[0]) + sh_ref[0]
    _store_rows(h_ref, h)

    @pl.when(pl.program_id(0) == 0)
    def _():
        carry_ref[...] = jnp.zeros_like(carry_ref)

    logits = _dot_nt(wr_ref[...], h.astype(BF16)) + br_ref[...]
    e_iota = lax.broadcasted_iota(jnp.int32, (N_EXPERTS, tm), 0).astype(F32)
    vals, ids = [], []
    for _ in range(TOP_K):
        top = jnp.max(logits, axis=0, keepdims=True)
        idx = jnp.min(jnp.where(logits == top, e_iota, float(N_EXPERTS)), axis=0, keepdims=True)
        logits = jnp.where(e_iota == idx, -jnp.inf, logits)
        vals.append(top)
        ids.append(idx)
    exps = [jnp.exp(v - vals[0]) for v in vals]
    denom = sum(exps)
    onehot = sum(jnp.where(e_iota == idx, 1.0, 0.0) for idx in ids)
    before = carry_ref[...] + _dot(onehot.astype(BF16), tri_ref[...])
    ranks = [jnp.sum(jnp.where(e_iota == idx, before, 0.0), axis=0, keepdims=True) for idx in ids]
    carry_ref[...] += jnp.sum(onehot, axis=1, keepdims=True)
    route_ref[0] = jnp.concatenate(ids + ranks, axis=0).astype(jnp.int32)
    gate_ref[0] = jnp.concatenate([e / denom for e in exps] + [jnp.zeros_like(denom)] * TOP_K, axis=0)
    cnt_ref[...] = jnp.broadcast_to(carry_ref[...], cnt_ref.shape)


def _output_stage(ys, wo, x, g1, sc2, sh2, ng, wr_t, br, tri, tiles_per_batch):
    t, d = x.shape
    tm = TOK_TILE
    nt = t // tm
    row = lambda width: pl.BlockSpec((tm, width), lambda i: (i, 0))
    const = lambda shape: pl.BlockSpec(shape, lambda i: (0,) * len(shape))
    per_batch = pl.BlockSpec((1, 1, d), lambda i: (i // tiles_per_batch, 0, 0))
    return pl.pallas_call(
        _outstage_body,
        out_shape=(
            jax.ShapeDtypeStruct((t, d), F32), jax.ShapeDtypeStruct((t * ROW_CHUNKS, LANES), F32),
            jax.ShapeDtypeStruct((nt, ROUTE_ROWS, tm), jnp.int32), jax.ShapeDtypeStruct((nt, ROUTE_ROWS, tm), F32),
            jax.ShapeDtypeStruct((N_EXPERTS, LANES), F32),
        ),
        grid=(nt,),
        in_specs=[row(GROUP_WIDTH)] * 4 + [const((d, d)), row(d), per_batch, per_batch, per_batch, const((1, d)),
                                           const((N_EXPERTS, d)), const((N_EXPERTS, 1)), const((tm, tm))],
        out_specs=(row(d), pl.BlockSpec((tm * ROW_CHUNKS, LANES), lambda i: (i, 0)),
                   pl.BlockSpec((1, ROUTE_ROWS, tm), lambda i: (i, 0, 0)),
                   pl.BlockSpec((1, ROUTE_ROWS, tm), lambda i: (i, 0, 0)),
                   const((N_EXPERTS, LANES))),
        scratch_shapes=[pltpu.VMEM((N_EXPERTS, 1), F32)],
        compiler_params=_cparams(("arbitrary",)),
        name="mixer_output_stage_router",
    )(*ys, wo, x, g1, sc2, sh2, ng, wr_t, br, tri)


def _route_fetch(route_hbm, route_smem, rsem):
    i = pl.program_id(0)
    n = route_hbm.shape[1]
    slot = lax.rem(i, 2)

    def fetch(step, sl):
        dst = route_smem.at[pl.ds(pl.multiple_of(sl * n, n), n)]
        return pltpu.make_async_copy(route_hbm.at[step], dst, rsem.at[sl])

    @pl.when(i == 0)
    def _():
        fetch(0, 0).start()

    fetch(i, slot).wait()

    @pl.when(i + 1 < pl.num_programs(0))
    def _():
        fetch(i + 1, 1 - slot).start()

    return slot * n


H_BUFFERS = 3


def _dispatch_body(pstart_ref, pend_ref, route_hbm, h_hbm, xs_hbm, route_smem, hbuf, zero_ref, rsem, hsem, zsem, sem):
    tm = hbuf.shape[1]
    i = pl.program_id(0)
    n = pl.num_programs(0)

    def h_load(step, b):
        return pltpu.make_async_copy(h_hbm.at[pl.ds(pl.multiple_of(step * tm, tm), tm)], hbuf.at[b], hsem.at[b])

    def wait_rows(parity):
        for _ in range(TOP_K):
            pltpu.make_async_copy(hbuf.at[0], xs_hbm.at[pl.ds(0, tm)], sem.at[parity]).wait()

    @pl.when(i == 0)
    def _():
        h_load(0, 0).start()

        @pl.when(n > 1)
        def _():
            h_load(1, 1).start()

        zero_ref[...] = jnp.zeros_like(zero_ref)

        def fill(e):
            start = pl.multiple_of(pend_ref[e] - EXPERT_TILE, EXPERT_TILE)
            return pltpu.make_async_copy(zero_ref, xs_hbm.at[pl.ds(start, EXPERT_TILE)], zsem)

        def start(e, c):
            @pl.when(pend_ref[e] > pstart_ref[e])
            def _():
                fill(e).start()
            return c

        def wait(e, c):
            @pl.when(pend_ref[e] > pstart_ref[e])
            def _():
                fill(e).wait()
            return c

        lax.fori_loop(0, N_EXPERTS, start, 0)
        lax.fori_loop(0, N_EXPERTS, wait, 0)

    rec = _route_fetch(route_hbm, route_smem, rsem)
    b = lax.rem(i, H_BUFFERS)
    parity = lax.rem(i, 2)
    h_load(i, b).wait()
    h_ref = hbuf.at[b]
    for k in range(TOP_K):
        def start(j, c, base=rec + k * tm):
            for prio in range(2):
                t = 2 * j + prio
                pltpu.make_async_copy(h_ref.at[t], xs_hbm.at[route_smem[base + t]], sem.at[parity]).start(priority=prio)
            return c

        lax.fori_loop(0, tm // 2, start, 0, unroll=DMA_UNROLL // 2)

    @pl.when(i > 0)
    def _():
        wait_rows(1 - parity)

    @pl.when(i + 2 < n)
    def _():
        h_load(i + 2, lax.rem(i + 2, H_BUFFERS)).start()

    @pl.when(i == n - 1)
    def _():
        wait_rows(parity)


def _dispatch(pstart, pend, route, h, n_rows):
    t, dc, _ = h.shape
    tm = TOK_TILE
    return pl.pallas_call(
        _dispatch_body,
        out_shape=jax.ShapeDtypeStruct((n_rows, dc, LANES), F32),
        grid_spec=pltpu.PrefetchScalarGridSpec(
            num_scalar_prefetch=2,
            grid=(t // tm,),
            in_specs=[pl.BlockSpec(memory_space=pl.ANY), pl.BlockSpec(memory_space=pl.ANY)],
            out_specs=pl.BlockSpec(memory_space=pl.ANY),
            scratch_shapes=[
                pltpu.SMEM((2 * TOP_K * tm,), jnp.int32), pltpu.VMEM((H_BUFFERS, tm, dc, LANES), F32),
                pltpu.VMEM((EXPERT_TILE, dc, LANES), F32),
                pltpu.SemaphoreType.DMA((2,)), pltpu.SemaphoreType.DMA((H_BUFFERS,)), pltpu.SemaphoreType.DMA,
                pltpu.SemaphoreType.DMA((2,)),
            ],
        ),
        compiler_params=_cparams(("arbitrary",)),
        name="moe_dispatch",
    )(pstart, pend, route, h)


def _expert_body(tile_e_ref, n_used_ref, x_ref, wgu_ref, bgu_ref, wdn_ref, bdn_ref, y_ref):
    @pl.when(pl.program_id(0) < n_used_ref[0])
    def _():
        f = wdn_ref.shape[1]
        gu = _dot(_load_rows(x_ref, 0, EXPERT_TILE).astype(BF16), wgu_ref[0]) + bgu_ref[0]
        glu = jnp.minimum(gu[:, :f], SWIGLU_LIMIT)
        lin = jnp.clip(gu[:, f:], -SWIGLU_LIMIT, SWIGLU_LIMIT)
        act = glu * jax.nn.sigmoid(SWIGLU_ALPHA * glu) * (lin + 1.0)
        _store_rows(y_ref, _dot(act.astype(BF16), wdn_ref[0]) + bdn_ref[0])


def _experts(tile_e, n_used, xs, wgu, bgu, wdn, bdn):
    p, dc, _ = xs.shape
    assert dc == ROW_CHUNKS
    d = dc * LANES
    f = wdn.shape[1]
    tm = EXPERT_TILE
    used = lambda i, te, nu: jnp.minimum(i, nu[0] - 1)
    return pl.pallas_call(
        _expert_body,
        out_shape=jax.ShapeDtypeStruct((p * dc, LANES), F32),
        grid_spec=pltpu.PrefetchScalarGridSpec(
            num_scalar_prefetch=2,
            grid=(p // tm,),
            in_specs=[
                pl.BlockSpec((tm * dc, LANES), lambda i, te, nu: (used(i, te, nu), 0)),
                pl.BlockSpec((1, d, 2 * f), lambda i, te, nu: (te[i], 0, 0)),
                pl.BlockSpec((1, 1, 2 * f), lambda i, te, nu: (te[i], 0, 0)),
                pl.BlockSpec((1, f, d), lambda i, te, nu: (te[i], 0, 0)),
                pl.BlockSpec((1, 1, d), lambda i, te, nu: (te[i], 0, 0)),
            ],
            out_specs=pl.BlockSpec((tm * dc, LANES), lambda i, te, nu: (used(i, te, nu), 0)),
        ),
        compiler_params=_cparams(("arbitrary",)),
        name="moe_experts",
    )(tile_e, n_used, xs.reshape(p * dc, LANES), wgu, bgu, wdn, bdn).reshape(p, dc, LANES)


COMBINE_ROWS = 32
ROUTE_SLOTS = 3


def _combine_body(final, route_hbm, ys_hbm, x_ref, g2_ref, gate_ref, fg_ref, o_ref,
                  route_smem, buf_ref, rsem, sem):
    tm = x_ref.shape[0]
    i = pl.program_id(0)
    n = pl.num_programs(0)
    n_rec = TOP_K * tm

    def rec_fetch(step):
        sl = lax.rem(step, ROUTE_SLOTS)
        dst = route_smem.at[pl.ds(pl.multiple_of(sl * n_rec, n_rec), n_rec)]
        return pltpu.make_async_copy(route_hbm.at[step], dst, rsem.at[sl])

    def issue_gathers(step):
        rec = lax.rem(step, ROUTE_SLOTS) * n_rec
        half = lax.rem(step, 2)
        row0 = half * n_rec

        def start(j, c):
            for prio in range(2):
                r = 2 * j + prio
                dst = buf_ref.at[pl.ds(pl.multiple_of((row0 + r) * ROW_CHUNKS, ROW_CHUNKS), ROW_CHUNKS)]
                pltpu.make_async_copy(ys_hbm.at[route_smem[rec + r]], dst, sem.at[half]).start(priority=prio)
            return c

        lax.fori_loop(0, n_rec // 2, start, 0, unroll=DMA_UNROLL // 2)

    @pl.when(i == 0)
    def _():
        rec_fetch(0).start()

        @pl.when(n > 1)
        def _():
            rec_fetch(1).start()

        rec_fetch(0).wait()
        issue_gathers(0)

    @pl.when(i + 1 < n)
    def _():
        rec_fetch(i + 1).wait()
        issue_gathers(i + 1)

    @pl.when(i + 2 < n)
    def _():
        rec_fetch(i + 2).start()

    half = lax.rem(i, 2)
    half_view = buf_ref.at[pl.ds(0, n_rec * ROW_CHUNKS)]
    pltpu.make_async_copy(half_view, half_view, sem.at[half]).wait()
    g2 = g2_ref[0]
    fg = fg_ref[...]

    def rows(chunk, c):
        r0 = pl.multiple_of(chunk * COMBINE_ROWS, COMBINE_ROWS)
        gates = gate_ref[pl.ds(r0, COMBINE_ROWS), :]
        moe = sum(gates[:, k:k + 1] * _load_rows(buf_ref, half * n_rec + k * tm + r0, COMBINE_ROWS)
                  for k in range(TOP_K))
        x = x_ref[pl.ds(r0, COMBINE_ROWS), :] + g2 * moe
        if final:
            x = x * lax.rsqrt(jnp.mean(x * x, axis=-1, keepdims=True) + NORM_EPS) * fg
        o_ref[pl.ds(r0, COMBINE_ROWS), :] = x
        return c

    lax.fori_loop(0, tm // COMBINE_ROWS, rows, 0)


def _combine(route, ys, x, g2, gates_col, final_g, tiles_per_batch, final):
    t, d = x.shape
    tm = TOK_TILE
    return pl.pallas_call(
        functools.partial(_combine_body, final),
        out_shape=jax.ShapeDtypeStruct((t, d), F32),
        grid=(t // tm,),
        in_specs=[
            pl.BlockSpec(memory_space=pl.ANY), pl.BlockSpec(memory_space=pl.ANY),
            pl.BlockSpec((tm, d), lambda i: (i, 0)),
            pl.BlockSpec((1, 1, d), lambda i: (i // tiles_per_batch, 0, 0)),
            pl.BlockSpec((tm, ROUTE_ROWS), lambda i: (i, 0)),
            pl.BlockSpec((1, d), lambda i: (0, 0)),
        ],
        out_specs=pl.BlockSpec((tm, d), lambda i: (i, 0)),
        scratch_shapes=[
            pltpu.SMEM((ROUTE_SLOTS * TOP_K * tm,), jnp.int32), pltpu.VMEM((2 * TOP_K * tm * ROW_CHUNKS, LANES), F32),
            pltpu.SemaphoreType.DMA((ROUTE_SLOTS,)), pltpu.SemaphoreType.DMA((2,)),
        ],
        compiler_params=_cparams(("arbitrary",)),
        name="moe_combine",
    )(route, ys, x, g2, gates_col, final_g)


def _moe_plan(counts, n_tiles):
    counts = counts.astype(jnp.int32)
    tiles = (counts + EXPERT_TILE - 1) // EXPERT_TILE
    tile_end = jnp.cumsum(tiles)
    pend = tile_end * EXPERT_TILE
    pstart = pend - tiles * EXPERT_TILE
    n_used = tile_end[-1:]
    tile = jnp.minimum(jnp.arange(n_tiles), n_used - 1)
    tile_e = jnp.sum((tile_end[None, :] <= tile[:, None]).astype(jnp.int32), axis=1)
    return pstart, pend, tile_e, n_used


def _tri(n, rel):
    r = jnp.arange(n)[:, None]
    c = jnp.arange(n)[None, :]
    return rel(r, c).astype(BF16)


def _constants(s):
    return dict(
        tab=_rope_tables(s),
        tri_le=_tri(ATT_TILE, lambda r, c: c <= r),
        tri_gt=_tri(ATT_TILE, lambda r, c: r > c),
        tri_lt=_tri(ATT_TILE, lambda r, c: r < c),
        tri_lt_tok=_tri(TOK_TILE, lambda r, c: r < c),
    )


def _prep_layer(p, l):
    bf = jnp.zeros((1, LANES), F32).at[0, MISC_FA:MISC_FA + HEADS].set(p["b_forget"][l].astype(F32))
    w, wt = _reorder_w_in(p["w_in"][l])
    return dict(
        w_in=w, w_in_t=wt,
        wkv=jnp.concatenate([p["w_uk"][l], p["w_uv"][l]], axis=-1).astype(BF16),
        wuvt=p["w_uv"][l].T.astype(BF16),
        kvg=p["kv_norm_g"][l].reshape(1, KV_RANK).astype(F32),
        bf=bf,
        g1=p["norm1_g"][l].reshape(1, -1).astype(F32),
        chunk_bias=_chunk_bias_table(p["rel_bias"][l], ATT_TILE),
    )


def _mixers(x, sc1, sh1, lp, cst):
    tq = ATT_TILE
    (fq, fk, fvt, dq, dkv, dvt, dqi, dkit, dwt, cq, ck, cvt, eq, ek, ev) = _input_stage(
        x, sc1, sh1, lp["g1"], lp["w_in"], lp["w_in_t"], lp["wkv"], lp["wuvt"], lp["kvg"], lp["bf"],
        cst["tri_le"], cst["tab"], tq)
    ya = _fox_attention(fq, fk, fvt, tq)
    yb = _dsa_attention(dq, dkv, dvt, dqi, dkit, dwt, cst["tri_gt"], tq)
    yc = _sb_attention(cq, ck, cvt, cst["tri_lt"], tq)
    yd = _chunk_attention(eq, ek, ev, lp["chunk_bias"], tq)
    return ya, yb, yc, yd


def _moe(p, l, x, h, route, gate, counts, g2, tiles_per_batch, final):
    t, d = x.shape
    nt = t // TOK_TILE
    n_tiles = t * TOP_K // EXPERT_TILE + N_EXPERTS
    pstart, pend, tile_e, n_used = _moe_plan(counts, n_tiles)
    ids, ranks = route[:, :TOP_K, :], route[:, TOP_K:, :]
    first_row = sum(jnp.where(ids == e, pstart[e], 0) for e in range(N_EXPERTS))
    route = (first_row + ranks).reshape(nt, TOP_K * TOK_TILE)
    xs = _dispatch(pstart, pend, route, h.reshape(t, ROW_CHUNKS, LANES), n_tiles * EXPERT_TILE)
    ys = _experts(tile_e, n_used, xs,
                  p["w_gu"][l].astype(BF16), p["b_gu"][l][:, None, :].astype(F32),
                  p["w_dn"][l].astype(BF16), p["b_dn"][l][:, None, :].astype(F32))
    gates_col = gate.transpose(0, 2, 1).reshape(t, ROUTE_ROWS)
    return _combine(route, ys, x, g2, gates_col, p["final_g"].reshape(1, d).astype(F32), tiles_per_batch, final)


def kernel(x, c, w_ada, b_ada, norm1_g, norm2_g, w_in, b_forget, kv_norm_g, w_uk, w_uv, rel_bias, w_o,
           w_router, b_router, w_gu, b_gu, w_dn, b_dn, final_g):
    p = dict(w_in=w_in, b_forget=b_forget, kv_norm_g=kv_norm_g, w_uk=w_uk, w_uv=w_uv, rel_bias=rel_bias,
             norm1_g=norm1_g, w_gu=w_gu, b_gu=b_gu, w_dn=w_dn, b_dn=b_dn, final_g=final_g)
    b, s, d = x.shape
    t = b * s
    depth = w_ada.shape[0]
    tiles_per_batch = s // TOK_TILE
    mod = _modulation(c, w_ada, b_ada)
    cst = _constants(s)
    for l in range(depth):
        sh1, sc1, g1, sh2, sc2, g2 = [m[:, None, :] for m in jnp.split(mod[l], 6, axis=-1)]
        ys = _mixers(x, sc1, sh1, _prep_layer(p, l), cst)
        x2, h, route, gate, cnt = _output_stage(
            [y.reshape(t, GROUP_WIDTH) for y in ys], w_o[l].astype(BF16), x.reshape(t, d), g1, sc2, sh2,
            norm2_g[l].reshape(1, d).astype(F32), w_router[l].T.astype(BF16),
            b_router[l].reshape(N_EXPERTS, 1).astype(F32), cst["tri_lt_tok"], tiles_per_batch)
        x = _moe(p, l, x2, h, route, gate, cnt[:, 0], g2, tiles_per_batch, l == depth - 1).reshape(b, s, d)
    return x
```

```python
import functools
import math

import jax
import jax.numpy as jnp
from jax import lax
from jax.experimental import pallas as pl
from jax.experimental.pallas import tpu as pltpu

F32 = jnp.float32
BF16 = jnp.bfloat16

HEAD_DIM = 64
HEADS = 4
GROUP_WIDTH = HEADS * HEAD_DIM
CHUNK = 64
ROPE_THETA = 10000.0
NORM_EPS = 1e-6
NEG_INF = -1e30
KV_RANK = 128
IDX_HEADS = 4
IDX_DIM = 32
TOPK_MAX = 256
LEFT_CHUNKS = 8
MAX_REL = 128
N_EXPERTS = 32
TOP_K = 4
SWIGLU_ALPHA = 1.702
SWIGLU_LIMIT = 7.0
LOG2E = math.log2(math.e)
INT_MIN = -2 ** 31

LANES = 128
SUBLANES = 8
VMEM_LIMIT = 48 * 1024 * 1024

ATT_TILE = 256
TOK_TILE = 512
EXPERT_TILE = 512
ROUTE_ROWS = 2 * TOP_K
DMA_UNROLL = 8

COL_A = 0
COL_C = 3 * GROUP_WIDTH
COL_D = 6 * GROUP_WIDTH
COL_QB = 9 * GROUP_WIDTH
COL_LAT = COL_QB + GROUP_WIDTH
COL_QI = COL_LAT + KV_RANK
COL_MISC = COL_QI + IDX_HEADS * IDX_DIM
N_COLS = COL_MISC + LANES
MISC_WI = IDX_DIM
MISC_FA = IDX_DIM + IDX_HEADS
ROW_VA = 0
ROW_VC = GROUP_WIDTH
ROW_WI = 2 * GROUP_WIDTH
N_ROWS_T = ROW_WI + SUBLANES


def _cparams(sem):
    return pltpu.CompilerParams(dimension_semantics=sem, vmem_limit_bytes=VMEM_LIMIT)


def _split3(x):
    hi = x.astype(BF16)
    r = x - hi.astype(F32)
    mid = r.astype(BF16)
    lo = (r - mid.astype(F32)).astype(BF16)
    return hi, mid, lo


def _dot(a, b):
    return jnp.dot(a, b, preferred_element_type=F32)


def _dot_nt(a, b):
    return lax.dot_general(a, b, (((1,), (1,)), ((), ())), preferred_element_type=F32)


def _mod_body(c_ref, w_ref, b_ref, o_ref):
    c = c_ref[...]
    act = (c * jax.nn.sigmoid(c)).astype(BF16)
    o_ref[0] = _dot(act, w_ref[0].astype(BF16)) + b_ref[0]


def _modulation(c, w_ada, b_ada):
    depth, d, n = w_ada.shape
    b = c.shape[0]
    tn = 1024
    return pl.pallas_call(
        _mod_body,
        out_shape=jax.ShapeDtypeStruct((depth, b, n), F32),
        grid=(depth, n // tn),
        in_specs=[
            pl.BlockSpec((b, d), lambda l, j: (0, 0)),
            pl.BlockSpec((1, d, tn), lambda l, j: (l, 0, j)),
            pl.BlockSpec((1, 1, tn), lambda l, j: (l, 0, j)),
        ],
        out_specs=pl.BlockSpec((1, b, tn), lambda l, j: (l, 0, j)),
        compiler_params=_cparams(("arbitrary", "arbitrary")),
        name="adaln_mod",
    )(c, w_ada, b_ada.reshape(depth, 1, n))


def _rope_tables(s):
    pos = jnp.arange(s, dtype=F32)[:, None]
    lane = jnp.arange(LANES)[None, :]

    def tables(dim, active):
        half = dim // 2
        j = lane % dim
        inv = ROPE_THETA ** (-(j % half).astype(F32) / half)
        ang = pos * inv
        cos = jnp.where(active, jnp.cos(ang), 1.0)
        sin = jnp.where(active, jnp.sin(ang), 0.0)
        first = j < half
        return [cos, jnp.where(first, -sin, 0.0), jnp.where(first, 0.0, sin)]

    everywhere = lane >= 0
    t = (tables(HEAD_DIM, everywhere) + tables(HEAD_DIM, lane < HEAD_DIM)
         + tables(IDX_DIM, everywhere) + tables(IDX_DIM, lane < IDX_DIM))
    return jnp.stack(t).astype(F32)


def _rope(x, tab, k, half):
    cos, sin_a, sin_b = tab[3 * k], tab[3 * k + 1], tab[3 * k + 2]
    return x * cos + pltpu.roll(x, LANES - half, 1) * sin_a + pltpu.roll(x, half, 1) * sin_b


def _instage_body(x_ref, sc_ref, sh_ref, g_ref, w_ref, wt_ref, wkv_ref, wuvt_ref, kvg_ref, bf_ref, tri_ref, tab_ref,
                  fq_o, fk_o, fvt_o, dq_o, dkv_o, dvt_o, dqi_o, dkit_o, dwt_o,
                  cq_o, ck_o, cvt_o, eq_o, ek_o, ev_o, carry_ref):
    tm = x_ref.shape[1]
    x = x_ref[0]
    y = x * lax.rsqrt(jnp.mean(x * x, axis=-1, keepdims=True) + NORM_EPS) * g_ref[...]
    h = (y * (1.0 + sc_ref[0]) + sh_ref[0]).astype(BF16)

    def proj(c0, n):
        return _dot(h, w_ref[:, c0:c0 + n])

    def proj_t(r0, n):
        return _dot_nt(wt_ref[r0:r0 + n, :], h)

    def heads_out(val, refs, scale):
        for i, ref in enumerate(refs):
            for hd in range(HEADS):
                c0 = i * GROUP_WIDTH + hd * HEAD_DIM
                piece = val[:, c0:c0 + HEAD_DIM]
                if i == 0:
                    piece = piece * scale
                ref[0, hd] = piece.astype(ref.dtype)

    def heads_out_t(val, ref):
        for hd in range(HEADS):
            ref[0, hd] = val[hd * HEAD_DIM:(hd + 1) * HEAD_DIM, :].astype(ref.dtype)

    scale = HEAD_DIM ** -0.5
    heads_out(proj(COL_C, 2 * GROUP_WIDTH), (cq_o, ck_o), scale)
    heads_out_t(proj_t(ROW_VC, GROUP_WIDTH), cvt_o)
    heads_out(proj(COL_D, 3 * GROUP_WIDTH), (eq_o, ek_o, ev_o), scale)

    misc = _rope(proj(COL_MISC, LANES), tab_ref, 3, IDX_DIM // 2)
    lane = lax.broadcasted_iota(jnp.int32, (tm, LANES), 1)
    ki = jnp.where(lane < IDX_DIM, misc, 0.0)
    kit = ki + pltpu.roll(ki, IDX_DIM, 1) + pltpu.roll(ki, 2 * IDX_DIM, 1) + pltpu.roll(ki, 3 * IDX_DIM, 1)
    dkit_o[0] = kit.astype(BF16)
    dwt_o[0] = proj_t(ROW_WI, SUBLANES) * ((IDX_HEADS ** -0.5) * (IDX_DIM ** -0.5))
    qi = _rope(proj(COL_QI, LANES), tab_ref, 2, IDX_DIM // 2)
    for hd in range(IDX_HEADS):
        in_head = (lane >= hd * IDX_DIM) & (lane < (hd + 1) * IDX_DIM)
        dqi_o[0, hd] = jnp.where(in_head, qi, 0.0).astype(BF16)
    lat = proj(COL_LAT, KV_RANK)
    lat = (lat * lax.rsqrt(jnp.mean(lat * lat, axis=-1, keepdims=True) + NORM_EPS) * kvg_ref[...]).astype(BF16)
    dkv_o[0] = _rope(_dot(lat, wkv_ref[...]), tab_ref, 1, HEAD_DIM // 2).astype(BF16)
    dvt_o[0] = _dot_nt(wuvt_ref[...], lat).astype(BF16)
    qb = proj(COL_QB, GROUP_WIDTH)
    zeros64 = jnp.zeros((tm, HEAD_DIM), F32)
    for half_i in range(2):
        r = _rope(qb[:, half_i * LANES:(half_i + 1) * LANES], tab_ref, 0, HEAD_DIM // 2) * (scale * LOG2E)
        for j in range(2):
            piece = jnp.concatenate([r[:, j * HEAD_DIM:(j + 1) * HEAD_DIM], zeros64], axis=-1)
            dq_o[0, 2 * half_i + j] = piece.astype(BF16)

    @pl.when(pl.program_id(1) == 0)
    def _():
        carry_ref[...] = jnp.zeros_like(carry_ref)

    z = misc + bf_ref[...]
    log_f = jnp.minimum(z, 0.0) - jnp.log1p(jnp.exp(-jnp.abs(z)))
    tri = tri_ref[...]
    f_cum = carry_ref[...] + sum(_dot(tri, p) for p in _split3(log_f))
    carry_ref[...] = f_cum[tm - 1:tm, :]
    f_cum = f_cum * LOG2E
    pa = proj(COL_A, 2 * GROUP_WIDTH)
    heads_out_t(proj_t(ROW_VA, GROUP_WIDTH), fvt_o)
    lane64 = lax.broadcasted_iota(jnp.int32, (tm, HEAD_DIM), 1)
    for hd in range(HEADS):
        fh = f_cum[:, MISC_FA + hd:MISC_FA + hd + 1]
        f3 = [jnp.broadcast_to(p.astype(F32), (tm, HEAD_DIM)) for p in _split3(fh)]
        one = jnp.where(lane64 < 6, 1.0, 0.0)
        q_ext = jnp.where(lane64 == 0, f3[0], jnp.where(lane64 == 1, f3[1], jnp.where(lane64 == 2, f3[2], one)))
        k_ext = jnp.where(lane64 == 3, -f3[0], jnp.where(lane64 == 4, -f3[1], jnp.where(lane64 == 5, -f3[2], one)))
        q = pa[:, hd * HEAD_DIM:(hd + 1) * HEAD_DIM] * (scale * LOG2E)
        k = pa[:, GROUP_WIDTH + hd * HEAD_DIM:GROUP_WIDTH + (hd + 1) * HEAD_DIM]
        fq_o[0, hd] = jnp.concatenate([q, q_ext], axis=-1).astype(BF16)
        fk_o[0, hd] = jnp.concatenate([k, k_ext], axis=-1).astype(BF16)


def _reorder_w_in(w_in):
    gw = GROUP_WIDTH
    a0 = 0
    b0 = 3 * gw + HEADS
    c0 = b0 + gw + KV_RANK + IDX_HEADS * IDX_DIM + IDX_DIM + IDX_HEADS
    d0 = c0 + 3 * gw
    lat0 = b0 + gw
    qi0 = lat0 + KV_RANK
    ki0 = qi0 + IDX_HEADS * IDX_DIM
    wi0 = ki0 + IDX_DIM
    pad = jnp.zeros(w_in.shape[:-1] + (LANES - IDX_DIM - IDX_HEADS - HEADS,), w_in.dtype)
    w = jnp.concatenate([
        w_in[:, a0:a0 + 3 * gw], w_in[:, c0:c0 + 3 * gw], w_in[:, d0:d0 + 3 * gw],
        w_in[:, b0:b0 + gw], w_in[:, lat0:lat0 + KV_RANK], w_in[:, qi0:qi0 + IDX_HEADS * IDX_DIM],
        w_in[:, ki0:ki0 + IDX_DIM], w_in[:, wi0:wi0 + IDX_HEADS], w_in[:, 3 * gw:3 * gw + HEADS], pad,
    ], axis=-1)
    wt = jnp.concatenate([
        w_in[:, a0 + 2 * gw:a0 + 3 * gw], w_in[:, c0 + 2 * gw:c0 + 3 * gw], w_in[:, wi0:wi0 + IDX_HEADS],
        jnp.zeros((w_in.shape[0], SUBLANES - IDX_HEADS), w_in.dtype),
    ], axis=-1).T
    return w.astype(BF16), wt.astype(BF16)


def _input_stage(x, sc, sh, g, w, wt, wkv, wuvt, kvg, bf, tri, tab, tm):
    b, s, d = x.shape
    hq = lambda width: jax.ShapeDtypeStruct((b, HEADS, s, width), BF16)
    hqt = jax.ShapeDtypeStruct((b, HEADS, HEAD_DIM, s), BF16)
    flat = jax.ShapeDtypeStruct((b, s, LANES), BF16)
    out_shape = (
        hq(LANES), hq(LANES), hqt,
        hq(LANES), flat, jax.ShapeDtypeStruct((b, HEAD_DIM, s), BF16),
        hq(LANES), flat, jax.ShapeDtypeStruct((b, SUBLANES, s), F32),
        hq(HEAD_DIM), hq(HEAD_DIM), hqt,
        hq(HEAD_DIM), hq(HEAD_DIM), hq(HEAD_DIM),
    )
    hspec = lambda width: pl.BlockSpec((1, HEADS, tm, width), lambda i, j: (i, 0, j, 0))
    htspec = pl.BlockSpec((1, HEADS, HEAD_DIM, tm), lambda i, j: (i, 0, 0, j))
    fspec = pl.BlockSpec((1, tm, LANES), lambda i, j: (i, j, 0))
    tspec = lambda rows: pl.BlockSpec((1, rows, tm), lambda i, j: (i, 0, j))
    const2 = lambda shape: pl.BlockSpec(shape, lambda i, j: (0, 0))
    out_specs = (
        hspec(LANES), hspec(LANES), htspec,
        hspec(LANES), fspec, tspec(HEAD_DIM),
        hspec(LANES), fspec, tspec(SUBLANES),
        hspec(HEAD_DIM), hspec(HEAD_DIM), htspec,
        hspec(HEAD_DIM), hspec(HEAD_DIM), hspec(HEAD_DIM),
    )
    return pl.pallas_call(
        _instage_body,
        out_shape=out_shape,
        grid=(b, s // tm),
        in_specs=[
            pl.BlockSpec((1, tm, d), lambda i, j: (i, j, 0)),
            pl.BlockSpec((1, 1, d), lambda i, j: (i, 0, 0)),
            pl.BlockSpec((1, 1, d), lambda i, j: (i, 0, 0)),
            const2((1, d)),
            const2((d, N_COLS)),
            const2((N_ROWS_T, d)),
            const2((KV_RANK, LANES)),
            const2((HEAD_DIM, KV_RANK)),
            const2((1, KV_RANK)),
            const2((1, LANES)),
            const2((tm, tm)),
            pl.BlockSpec((12, tm, LANES), lambda i, j: (0, j, 0)),
        ],
        out_specs=out_specs,
        scratch_shapes=[pltpu.VMEM((1, LANES), F32)],
        compiler_params=_cparams(("arbitrary", "arbitrary")),
        name="mixer_input_stage",
    )(x, sc, sh, g, w, wt, wkv, wuvt, kvg, bf, tri, tab)


def _key_query_iota(t):
    return lax.broadcasted_iota(jnp.int32, (t, t), 0), lax.broadcasted_iota(jnp.int32, (t, t), 1)


def _softmax2_step(s, vt, m, l, acc):
    m_new = jnp.maximum(m, jnp.max(s, axis=0, keepdims=True))
    alpha = jnp.exp2(m - m_new)
    p = jnp.exp2(s - m_new)
    l_new = alpha * l + jnp.sum(p, axis=0, keepdims=True)
    acc_new = alpha * acc + _dot(vt, p.astype(BF16))
    return m_new, l_new, acc_new


def _skewed(n, stages):
    vals = [None] * n
    for step in range(n + len(stages) - 1):
        for j, stage in enumerate(stages):
            i = step - j
            if 0 <= i < n:
                vals[i] = stage(i, vals[i])


def _heads_to_rows(accs):
    return jnp.concatenate([a.T for a in accs], axis=-1)


def _fox_body(q_ref, k_ref, vt_ref, o_ref, m_ref, l_ref, acc_ref):
    qi = pl.program_id(1)
    tq = q_ref.shape[2]
    key, qry = _key_query_iota(tq)
    m_ref[...] = jnp.full_like(m_ref, NEG_INF)
    l_ref[...] = jnp.zeros_like(l_ref)
    acc_ref[...] = jnp.zeros_like(acc_ref)

    def blocks(kbs):
        items = [(pl.multiple_of(kb * tq, tq), masked, hd) for kb, masked in kbs for hd in range(HEADS)]

        def scores(i, _):
            k0, masked, hd = items[i]
            s = _dot_nt(k_ref[0, hd, pl.ds(k0, tq), :], q_ref[0, hd])
            return jnp.where(key <= qry, s, NEG_INF) if masked else s

        def update(i, s):
            k0, _, hd = items[i]
            m_ref[hd], l_ref[hd], acc_ref[hd] = _softmax2_step(
                s, vt_ref[0, hd, :, pl.ds(k0, tq)], m_ref[hd], l_ref[hd], acc_ref[hd])

        _skewed(len(items), [scores, update])

    def pair_body(j, carry):
        blocks([(2 * j, False), (2 * j + 1, False)])
        return carry

    lax.fori_loop(0, lax.shift_right_logical(qi, 1), pair_body, 0)

    @pl.when((qi & 1) == 1)
    def _():
        blocks([(qi - 1, False), (qi, True)])

    @pl.when((qi & 1) == 0)
    def _():
        blocks([(qi, True)])

    o_ref[0] = _heads_to_rows([acc_ref[hd] / l_ref[hd] for hd in range(HEADS)]).astype(o_ref.dtype)


def _fox_attention(q, k, vt, tq):
    b, _, s, _ = q.shape
    return pl.pallas_call(
        _fox_body,
        out_shape=jax.ShapeDtypeStruct((b, s, GROUP_WIDTH), BF16),
        grid=(b, s // tq),
        in_specs=[
            pl.BlockSpec((1, HEADS, tq, LANES), lambda i, j: (i, 0, j, 0)),
            pl.BlockSpec((1, HEADS, s, LANES), lambda i, j: (i, 0, 0, 0)),
            pl.BlockSpec((1, HEADS, HEAD_DIM, s), lambda i, j: (i, 0, 0, 0)),
        ],
        out_specs=pl.BlockSpec((1, tq, GROUP_WIDTH), lambda i, j: (i, j, 0)),
        scratch_shapes=[pltpu.VMEM((HEADS, 1, tq), F32), pltpu.VMEM((HEADS, 1, tq), F32),
                        pltpu.VMEM((HEADS, HEAD_DIM, tq), F32)],
        compiler_params=_cparams(("arbitrary", "arbitrary")),
        name="forgetting_attention",
    )(q, k, vt)


def _sb_body(q_ref, k_ref, vt_ref, tri_ref, o_ref, r_ref, acc_ref):
    qi = pl.program_id(1)
    tq = q_ref.shape[2]
    key, qry = _key_query_iota(tq)
    tri = tri_ref[...]
    r_ref[...] = jnp.zeros_like(r_ref)
    acc_ref[...] = jnp.zeros_like(acc_ref)

    def blocks(kbs):
        items = [(pl.multiple_of(kb * tq, tq), masked, hd) for kb, masked in kbs for hd in range(HEADS)]

        def logits(i, _):
            k0, _, hd = items[i]
            return _dot_nt(k_ref[0, hd, pl.ds(k0, tq), :], q_ref[0, hd])

        def later_sums(i, z):
            _, masked, hd = items[i]
            log_1m = -(jnp.maximum(z, 0.0) + jnp.log1p(jnp.exp(-jnp.abs(z))))
            if masked:
                log_1m = jnp.where(key < qry, log_1m, 0.0)
            after = r_ref[hd] + _dot(tri, log_1m.astype(BF16))
            return z, log_1m, after

        def accumulate(i, val):
            k0, masked, hd = items[i]
            z, log_1m, after = val
            w = jnp.exp(log_1m + z + after)
            if masked:
                w = jnp.where(key < qry, w, 0.0)
            acc_ref[hd] += _dot(vt_ref[0, hd, :, pl.ds(k0, tq)], w.astype(BF16))
            r_ref[hd] += jnp.sum(log_1m, axis=0, keepdims=True)

        _skewed(len(items), [logits, later_sums, accumulate])

    @pl.when((qi & 1) == 1)
    def _():
        blocks([(qi, True), (qi - 1, False)])

    @pl.when((qi & 1) == 0)
    def _():
        blocks([(qi, True)])

    top = qi - 1 - (qi & 1)

    def pair_body(j, carry):
        blocks([(top - 2 * j, False), (top - 2 * j - 1, False)])
        return carry

    lax.fori_loop(0, lax.shift_right_logical(qi, 1), pair_body, 0)
    o_ref[0] = _heads_to_rows([acc_ref[hd] for hd in range(HEADS)]).astype(o_ref.dtype)


def _sb_attention(q, k, vt, tri, tq):
    b, _, s, _ = q.shape
    return pl.pallas_call(
        _sb_body,
        out_shape=jax.ShapeDtypeStruct((b, s, GROUP_WIDTH), BF16),
        grid=(b, s // tq),
        in_specs=[
            pl.BlockSpec((1, HEADS, tq, HEAD_DIM), lambda i, j: (i, 0, j, 0)),
            pl.BlockSpec((1, HEADS, s, HEAD_DIM), lambda i, j: (i, 0, 0, 0)),
            pl.BlockSpec((1, HEADS, HEAD_DIM, s), lambda i, j: (i, 0, 0, 0)),
            pl.BlockSpec((tq, tq), lambda i, j: (0, 0)),
        ],
        out_specs=pl.BlockSpec((1, tq, GROUP_WIDTH), lambda i, j: (i, j, 0)),
        scratch_shapes=[pltpu.VMEM((HEADS, 1, tq), F32), pltpu.VMEM((HEADS, HEAD_DIM, tq), F32)],
        compiler_params=_cparams(("arbitrary", "arbitrary")),
        name="stick_breaking_attention",
    )(q, k, vt, tri)


def _chunk_bias_table(rel_table, tq):
    left = LEFT_CHUNKS * CHUNK
    width = left + tq
    d = jnp.arange(width + tq - 1) - (tq - 1) - left
    diag = rel_table[:, jnp.clip(d, -MAX_REL, MAX_REL) + MAX_REL].astype(F32)
    bias = jnp.stack([diag[:, tq - 1 - t:tq - 1 - t + width] for t in range(tq)], axis=1)
    t = jnp.arange(tq)[:, None]
    j = jnp.arange(width)[None, :]
    c0 = (t // CHUNK) * CHUNK
    visible = (j >= c0) & (j - left < c0 + CHUNK)
    return jnp.where(visible[None], bias, NEG_INF)


def _chunk_body(nb, q_ref, *refs):
    k_refs, v_refs = refs[:nb], refs[nb:2 * nb]
    bias_ref, o_ref = refs[2 * nb], refs[2 * nb + 1]
    qi = pl.program_id(1)
    outs = []
    for hd in range(HEADS):
        q = q_ref[0, hd]
        s_parts = []
        for n in range(nb):
            s = _dot_nt(q, k_refs[n][0, hd])
            s_parts.append(jnp.where(qi - (nb - 1) + n >= 0, s, NEG_INF))
        s = jnp.concatenate(s_parts, axis=-1) + bias_ref[hd]
        m = jnp.max(s, axis=-1, keepdims=True)
        p = jnp.exp(s - m)
        l = jnp.sum(p, axis=-1, keepdims=True)
        p = p.astype(BF16)
        tq = q.shape[0]
        acc = sum(_dot(p[:, n * tq:(n + 1) * tq], v_refs[n][0, hd]) for n in range(nb))
        outs.append(acc / l)
    o_ref[0] = jnp.concatenate(outs, axis=-1).astype(o_ref.dtype)


def _chunk_attention(q, k, v, bias, tq):
    b, _, s, _ = q.shape
    left = LEFT_CHUNKS * CHUNK
    assert left % tq == 0
    nb = left // tq + 1
    kv_specs = [
        pl.BlockSpec((1, HEADS, tq, HEAD_DIM), functools.partial(
            lambda i, j, n: (i, 0, jnp.maximum(j - (nb - 1) + n, 0), 0), n=n))
        for n in range(nb)
    ]
    return pl.pallas_call(
        functools.partial(_chunk_body, nb),
        out_shape=jax.ShapeDtypeStruct((b, s, GROUP_WIDTH), BF16),
        grid=(b, s // tq),
        in_specs=[pl.BlockSpec((1, HEADS, tq, HEAD_DIM), lambda i, j: (i, 0, j, 0))] + kv_specs + kv_specs
        + [pl.BlockSpec((HEADS, tq, nb * tq), lambda i, j: (0, 0, 0))],
        out_specs=pl.BlockSpec((1, tq, GROUP_WIDTH), lambda i, j: (i, j, 0)),
        compiler_params=_cparams(("arbitrary", "arbitrary")),
        name="chunked_relbias_attention",
    )(q, *([k] * nb), *([v] * nb), bias)


def _dsa_body(n_sel, q_ref, kv_ref, vt_ref, qi_ref, kit_ref, wt_ref, tri_ref, o_ref,
              key_ref, bias_ref, m_ref, l_ref, acc_ref):
    i = pl.program_id(1)
    tq = q_ref.shape[2]
    nkb = i + 1
    key_pos, qry_pos = _key_query_iota(tq)
    vis_diag = key_pos < (qry_pos // CHUNK + 1) * CHUNK
    wt = wt_ref[0]

    def score_block(kb, masked):
        k0 = pl.multiple_of(kb * tq, tq)
        kit = kit_ref[0, pl.ds(k0, tq), :]
        dots = [_dot_nt(kit, qi_ref[0, hd]) for hd in range(IDX_HEADS)]
        score = sum(wt[hd:hd + 1, :] * jnp.maximum(dots[hd], 0.0) for hd in range(IDX_HEADS))
        score = jnp.where(score == 0.0, 0.0, score)
        bits = pltpu.bitcast(score, jnp.int32)
        okey = jnp.where(bits < 0, bits ^ 0x7FFFFFFF, bits)
        if masked:
            okey = jnp.where(vis_diag, okey, INT_MIN)
        key_ref[kb] = okey

    def score_loop(kb, carry):
        score_block(kb, False)
        return carry

    lax.fori_loop(0, i, score_loop, 0)
    score_block(i, True)

    def count(pred):
        def body(kb, acc):
            ind = jnp.where(pred(key_ref[kb]), 1.0, 0.0)
            return acc + ind.reshape(tq // SUBLANES, SUBLANES, tq).sum(axis=0)
        acc = lax.fori_loop(0, nkb, body, jnp.zeros((SUBLANES, tq), F32))
        return jnp.sum(acc, axis=0, keepdims=True)

    def bit_step(it, ans_u):
        cand_u = ans_u | lax.shift_left(jnp.int32(1), 31 - it)
        cand = cand_u ^ INT_MIN
        return jnp.where(count(lambda k: k >= cand) >= n_sel, cand_u, ans_u)

    thr = lax.fori_loop(0, 32, bit_step, jnp.zeros((1, tq), jnp.int32)) ^ INT_MIN

    n_gt = count(lambda k: k > thr)
    need = jnp.where(thr != INT_MIN, n_sel - n_gt, 0.0)
    tri = tri_ref[...]

    def select_block(kb, seen):
        okey = key_ref[kb]
        eq = okey == thr
        eq_f = jnp.where(eq, 1.0, 0.0)
        earlier = seen + _dot(tri, eq_f.astype(BF16))
        tie_bias = jnp.where(eq, jnp.where(earlier < need, 0.0, NEG_INF), NEG_INF)
        bias_ref[kb] = jnp.where(okey > thr, 0.0, tie_bias)
        return seen + jnp.sum(eq_f, axis=0, keepdims=True)

    lax.fori_loop(0, nkb, select_block, jnp.zeros((1, tq), F32))

    m_ref[...] = jnp.full_like(m_ref, NEG_INF)
    l_ref[...] = jnp.zeros_like(l_ref)
    acc_ref[...] = jnp.zeros_like(acc_ref)

    def attend(kbs):
        items = [(kb, pl.multiple_of(kb * tq, tq), hd) for kb in kbs for hd in range(HEADS)]

        def scores(n, _):
            kb, k0, hd = items[n]
            return _dot_nt(kv_ref[0, pl.ds(k0, tq), :], q_ref[0, hd]) + bias_ref[kb]

        def update(n, s):
            _, k0, hd = items[n]
            m_ref[hd], l_ref[hd], acc_ref[hd] = _softmax2_step(
                s, vt_ref[0, :, pl.ds(k0, tq)], m_ref[hd], l_ref[hd], acc_ref[hd])

        _skewed(len(items), [scores, update])

    def attend_pair(j, carry):
        attend([2 * j, 2 * j + 1])
        return carry

    lax.fori_loop(0, lax.shift_right_logical(nkb, 1), attend_pair, 0)

    @pl.when((nkb & 1) == 1)
    def _():
        attend([nkb - 1])
    o_ref[0] = _heads_to_rows([acc_ref[hd] / l_ref[hd] for hd in range(HEADS)]).astype(o_ref.dtype)


def _dsa_attention(q, kv, vt, qi, kit, wt, tri, tq):
    b, _, s, _ = q.shape
    n_sel = min(TOPK_MAX, s // 4)
    return pl.pallas_call(
        functools.partial(_dsa_body, n_sel),
        out_shape=jax.ShapeDtypeStruct((b, s, GROUP_WIDTH), BF16),
        grid=(b, s // tq),
        in_specs=[
            pl.BlockSpec((1, HEADS, tq, LANES), lambda i, j: (i, 0, j, 0)),
            pl.BlockSpec((1, s, LANES), lambda i, j: (i, 0, 0)),
            pl.BlockSpec((1, HEAD_DIM, s), lambda i, j: (i, 0, 0)),
            pl.BlockSpec((1, IDX_HEADS, tq, LANES), lambda i, j: (i, 0, j, 0)),
            pl.BlockSpec((1, s, LANES), lambda i, j: (i, 0, 0)),
            pl.BlockSpec((1, SUBLANES, tq), lambda i, j: (i, 0, j)),
            pl.BlockSpec((tq, tq), lambda i, j: (0, 0)),
        ],
        out_specs=pl.BlockSpec((1, tq, GROUP_WIDTH), lambda i, j: (i, j, 0)),
        scratch_shapes=[
            pltpu.VMEM((s // tq, tq, tq), jnp.int32), pltpu.VMEM((s // tq, tq, tq), F32),
            pltpu.VMEM((HEADS, 1, tq), F32), pltpu.VMEM((HEADS, 1, tq), F32),
            pltpu.VMEM((HEADS, HEAD_DIM, tq), F32),
        ],
        compiler_params=_cparams(("arbitrary", "arbitrary")),
        name="sparse_indexer_attention",
    )(q, kv, vt, qi, kit, wt, tri)


ROW_CHUNKS = 8


def _store_rows(ref, val, r0=0):
    n = val.shape[0]
    for c in range(ROW_CHUNKS):
        ref[pl.ds(r0 * ROW_CHUNKS + c, n, stride=ROW_CHUNKS), :] = val[:, c * LANES:(c + 1) * LANES]


def _load_rows(ref, r0, n):
    return jnp.concatenate(
        [ref[pl.ds(r0 * ROW_CHUNKS + c, n, stride=ROW_CHUNKS), :] for c in range(ROW_CHUNKS)], axis=-1)


def _outstage_body(ya_ref, yb_ref, yc_ref, yd_ref, wo_ref, x_ref, g1_ref, sc_ref, sh_ref, ng_ref,
                   wr_ref, br_ref, tri_ref, xo_ref, h_ref, route_ref, gate_ref, cnt_ref, carry_ref):
    tm = x_ref.shape[0]
    gw = GROUP_WIDTH
    mix = sum(_dot(r[...], wo_ref[n * gw:(n + 1) * gw, :]) for n, r in enumerate((ya_ref, yb_ref, yc_ref, yd_ref)))
    x = x_ref[...] + g1_ref[0] * mix
    xo_ref[...] = x
    y = x * lax.rsqrt(jnp.mean(x * x, axis=-1, keepdims=True) + NORM_EPS) * ng_ref[...]
    h = y * (1.0 + sc_ref[0]) + sh_ref[0]
    _store_rows(h_ref, h)

    @pl.when(pl.program_id(0) == 0)
    def _():
        carry_ref[...] = jnp.zeros_like(carry_ref)

    logits = _dot_nt(wr_ref[...], h.astype(BF16)) + br_ref[...]
    e_iota = lax.broadcasted_iota(jnp.int32, (N_EXPERTS, tm), 0).astype(F32)
    vals, ids = [], []
    for _ in range(TOP_K):
        top = jnp.max(logits, axis=0, keepdims=True)
        idx = jnp.min(jnp.where(logits == top, e_iota, float(N_EXPERTS)), axis=0, keepdims=True)
        logits = jnp.where(e_iota == idx, -jnp.inf, logits)
        vals.append(top)
        ids.append(idx)
    exps = [jnp.exp(v - vals[0]) for v in vals]
    denom = sum(exps)
    onehot = sum(jnp.where(e_iota == idx, 1.0, 0.0) for idx in ids)
    before = carry_ref[...] + _dot(onehot.astype(BF16), tri_ref[...])
    ranks = [jnp.sum(jnp.where(e_iota == idx, before, 0.0), axis=0, keepdims=True) for idx in ids]
    carry_ref[...] += jnp.sum(onehot, axis=1, keepdims=True)
    route_ref[0] = jnp.concatenate(ids + ranks, axis=0).astype(jnp.int32)
    gate_ref[0] = jnp.concatenate([e / denom for e in exps] + [jnp.zeros_like(denom)] * TOP_K, axis=0)
    cnt_ref[...] = jnp.broadcast_to(carry_ref[...], cnt_ref.shape)


def _output_stage(ys, wo, x, g1, sc2, sh2, ng, wr_t, br, tri, tiles_per_batch):
    t, d = x.shape
    tm = TOK_TILE
    nt = t // tm
    row = lambda width: pl.BlockSpec((tm, width), lambda i: (i, 0))
    const = lambda shape: pl.BlockSpec(shape, lambda i: (0,) * len(shape))
    per_batch = pl.BlockSpec((1, 1, d), lambda i: (i // tiles_per_batch, 0, 0))
    return pl.pallas_call(
        _outstage_body,
        out_shape=(
            jax.ShapeDtypeStruct((t, d), F32), jax.ShapeDtypeStruct((t * ROW_CHUNKS, LANES), F32),
            jax.ShapeDtypeStruct((nt, ROUTE_ROWS, tm), jnp.int32), jax.ShapeDtypeStruct((nt, ROUTE_ROWS, tm), F32),
            jax.ShapeDtypeStruct((N_EXPERTS, LANES), F32),
        ),
        grid=(nt,),
        in_specs=[row(GROUP_WIDTH)] * 4 + [const((d, d)), row(d), per_batch, per_batch, per_batch, const((1, d)),
                                           const((N_EXPERTS, d)), const((N_EXPERTS, 1)), const((tm, tm))],
        out_specs=(row(d), pl.BlockSpec((tm * ROW_CHUNKS, LANES), lambda i: (i, 0)),
                   pl.BlockSpec((1, ROUTE_ROWS, tm), lambda i: (i, 0, 0)),
                   pl.BlockSpec((1, ROUTE_ROWS, tm), lambda i: (i, 0, 0)),
                   const((N_EXPERTS, LANES))),
        scratch_shapes=[pltpu.VMEM((N_EXPERTS, 1), F32)],
        compiler_params=_cparams(("arbitrary",)),
        name="mixer_output_stage_router",
    )(*ys, wo, x, g1, sc2, sh2, ng, wr_t, br, tri)


def _route_fetch(route_hbm, route_smem, rsem):
    i = pl.program_id(0)
    n = route_hbm.shape[1]
    slot = lax.rem(i, 2)

    def fetch(step, sl):
        dst = route_smem.at[pl.ds(pl.multiple_of(sl * n, n), n)]
        return pltpu.make_async_copy(route_hbm.at[step], dst, rsem.at[sl])

    @pl.when(i == 0)
    def _():
        fetch(0, 0).start()

    fetch(i, slot).wait()

    @pl.when(i + 1 < pl.num_programs(0))
    def _():
        fetch(i + 1, 1 - slot).start()

    return slot * n


H_BUFFERS = 3


def _dispatch_body(pstart_ref, pend_ref, route_hbm, h_hbm, xs_hbm, route_smem, hbuf, zero_ref, rsem, hsem, zsem, sem):
    tm = hbuf.shape[1]
    i = pl.program_id(0)
    n = pl.num_programs(0)

    def h_load(step, b):
        return pltpu.make_async_copy(h_hbm.at[pl.ds(pl.multiple_of(step * tm, tm), tm)], hbuf.at[b], hsem.at[b])

    def wait_rows(parity):
        for _ in range(TOP_K):
            pltpu.make_async_copy(hbuf.at[0], xs_hbm.at[pl.ds(0, tm)], sem.at[parity]).wait()

    @pl.when(i == 0)
    def _():
        h_load(0, 0).start()

        @pl.when(n > 1)
        def _():
            h_load(1, 1).start()

        zero_ref[...] = jnp.zeros_like(zero_ref)

        def fill(e):
            start = pl.multiple_of(pend_ref[e] - EXPERT_TILE, EXPERT_TILE)
            return pltpu.make_async_copy(zero_ref, xs_hbm.at[pl.ds(start, EXPERT_TILE)], zsem)

        def start(e, c):
            @pl.when(pend_ref[e] > pstart_ref[e])
            def _():
                fill(e).start()
            return c

        def wait(e, c):
            @pl.when(pend_ref[e] > pstart_ref[e])
            def _():
                fill(e).wait()
            return c

        lax.fori_loop(0, N_EXPERTS, start, 0)
        lax.fori_loop(0, N_EXPERTS, wait, 0)

    rec = _route_fetch(route_hbm, route_smem, rsem)
    b = lax.rem(i, H_BUFFERS)
    parity = lax.rem(i, 2)
    h_load(i, b).wait()
    h_ref = hbuf.at[b]
    for k in range(TOP_K):
        def start(j, c, base=rec + k * tm):
            for prio in range(2):
                t = 2 * j + prio
                pltpu.make_async_copy(h_ref.at[t], xs_hbm.at[route_smem[base + t]], sem.at[parity]).start(priority=prio)
            return c

        lax.fori_loop(0, tm // 2, start, 0, unroll=DMA_UNROLL // 2)

    @pl.when(i > 0)
    def _():
        wait_rows(1 - parity)

    @pl.when(i + 2 < n)
    def _():
        h_load(i + 2, lax.rem(i + 2, H_BUFFERS)).start()

    @pl.when(i == n - 1)
    def _():
        wait_rows(parity)


def _dispatch(pstart, pend, route, h, n_rows):
    t, dc, _ = h.shape
    tm = TOK_TILE
    return pl.pallas_call(
        _dispatch_body,
        out_shape=jax.ShapeDtypeStruct((n_rows, dc, LANES), F32),
        grid_spec=pltpu.PrefetchScalarGridSpec(
            num_scalar_prefetch=2,
            grid=(t // tm,),
            in_specs=[pl.BlockSpec(memory_space=pl.ANY), pl.BlockSpec(memory_space=pl.ANY)],
            out_specs=pl.BlockSpec(memory_space=pl.ANY),
            scratch_shapes=[
                pltpu.SMEM((2 * TOP_K * tm,), jnp.int32), pltpu.VMEM((H_BUFFERS, tm, dc, LANES), F32),
                pltpu.VMEM((EXPERT_TILE, dc, LANES), F32),
                pltpu.SemaphoreType.DMA((2,)), pltpu.SemaphoreType.DMA((H_BUFFERS,)), pltpu.SemaphoreType.DMA,
                pltpu.SemaphoreType.DMA((2,)),
            ],
        ),
        compiler_params=_cparams(("arbitrary",)),
        name="moe_dispatch",
    )(pstart, pend, route, h)


def _expert_body(tile_e_ref, n_used_ref, x_ref, wgu_ref, bgu_ref, wdn_ref, bdn_ref, y_ref):
    @pl.when(pl.program_id(0) < n_used_ref[0])
    def _():
        f = wdn_ref.shape[1]
        gu = _dot(_load_rows(x_ref, 0, EXPERT_TILE).astype(BF16), wgu_ref[0]) + bgu_ref[0]
        glu = jnp.minimum(gu[:, :f], SWIGLU_LIMIT)
        lin = jnp.clip(gu[:, f:], -SWIGLU_LIMIT, SWIGLU_LIMIT)
        act = glu * jax.nn.sigmoid(SWIGLU_ALPHA * glu) * (lin + 1.0)
        _store_rows(y_ref, _dot(act.astype(BF16), wdn_ref[0]) + bdn_ref[0])


def _experts(tile_e, n_used, xs, wgu, bgu, wdn, bdn):
    p, dc, _ = xs.shape
    assert dc == ROW_CHUNKS
    d = dc * LANES
    f = wdn.shape[1]
    tm = EXPERT_TILE
    used = lambda i, te, nu: jnp.minimum(i, nu[0] - 1)
    return pl.pallas_call(
        _expert_body,
        out_shape=jax.ShapeDtypeStruct((p * dc, LANES), F32),
        grid_spec=pltpu.PrefetchScalarGridSpec(
            num_scalar_prefetch=2,
            grid=(p // tm,),
            in_specs=[
                pl.BlockSpec((tm * dc, LANES), lambda i, te, nu: (used(i, te, nu), 0)),
                pl.BlockSpec((1, d, 2 * f), lambda i, te, nu: (te[i], 0, 0)),
                pl.BlockSpec((1, 1, 2 * f), lambda i, te, nu: (te[i], 0, 0)),
                pl.BlockSpec((1, f, d), lambda i, te, nu: (te[i], 0, 0)),
                pl.BlockSpec((1, 1, d), lambda i, te, nu: (te[i], 0, 0)),
            ],
            out_specs=pl.BlockSpec((tm * dc, LANES), lambda i, te, nu: (used(i, te, nu), 0)),
        ),
        compiler_params=_cparams(("arbitrary",)),
        name="moe_experts",
    )(tile_e, n_used, xs.reshape(p * dc, LANES), wgu, bgu, wdn, bdn).reshape(p, dc, LANES)


COMBINE_ROWS = 32
ROUTE_SLOTS = 3


def _combine_body(final, route_hbm, ys_hbm, x_ref, g2_ref, gate_ref, fg_ref, o_ref,
                  route_smem, buf_ref, rsem, sem):
    tm = x_ref.shape[0]
    i = pl.program_id(0)
    n = pl.num_programs(0)
    n_rec = TOP_K * tm

    def rec_fetch(step):
        sl = lax.rem(step, ROUTE_SLOTS)
        dst = route_smem.at[pl.ds(pl.multiple_of(sl * n_rec, n_rec), n_rec)]
        return pltpu.make_async_copy(route_hbm.at[step], dst, rsem.at[sl])

    def issue_gathers(step):
        rec = lax.rem(step, ROUTE_SLOTS) * n_rec
        half = lax.rem(step, 2)
        row0 = half * n_rec

        def start(j, c):
            for prio in range(2):
                r = 2 * j + prio
                dst = buf_ref.at[pl.ds(pl.multiple_of((row0 + r) * ROW_CHUNKS, ROW_CHUNKS), ROW_CHUNKS)]
                pltpu.make_async_copy(ys_hbm.at[route_smem[rec + r]], dst, sem.at[half]).start(priority=prio)
            return c

        lax.fori_loop(0, n_rec // 2, start, 0, unroll=DMA_UNROLL // 2)

    @pl.when(i == 0)
    def _():
        rec_fetch(0).start()

        @pl.when(n > 1)
        def _():
            rec_fetch(1).start()

        rec_fetch(0).wait()
        issue_gathers(0)

    @pl.when(i + 1 < n)
    def _():
        rec_fetch(i + 1).wait()
        issue_gathers(i + 1)

    @pl.when(i + 2 < n)
    def _():
        rec_fetch(i + 2).start()

    half = lax.rem(i, 2)
    half_view = buf_ref.at[pl.ds(0, n_rec * ROW_CHUNKS)]
    pltpu.make_async_copy(half_view, half_view, sem.at[half]).wait()
    g2 = g2_ref[0]
    fg = fg_ref[...]

    def rows(chunk, c):
        r0 = pl.multiple_of(chunk * COMBINE_ROWS, COMBINE_ROWS)
        gates = gate_ref[pl.ds(r0, COMBINE_ROWS), :]
        moe = sum(gates[:, k:k + 1] * _load_rows(buf_ref, half * n_rec + k * tm + r0, COMBINE_ROWS)
                  for k in range(TOP_K))
        x = x_ref[pl.ds(r0, COMBINE_ROWS), :] + g2 * moe
        if final:
            x = x * lax.rsqrt(jnp.mean(x * x, axis=-1, keepdims=True) + NORM_EPS) * fg
        o_ref[pl.ds(r0, COMBINE_ROWS), :] = x
        return c

    lax.fori_loop(0, tm // COMBINE_ROWS, rows, 0)


def _combine(route, ys, x, g2, gates_col, final_g, tiles_per_batch, final):
    t, d = x.shape
    tm = TOK_TILE
    return pl.pallas_call(
        functools.partial(_combine_body, final),
        out_shape=jax.ShapeDtypeStruct((t, d), F32),
        grid=(t // tm,),
        in_specs=[
            pl.BlockSpec(memory_space=pl.ANY), pl.BlockSpec(memory_space=pl.ANY),
            pl.BlockSpec((tm, d), lambda i: (i, 0)),
            pl.BlockSpec((1, 1, d), lambda i: (i // tiles_per_batch, 0, 0)),
            pl.BlockSpec((tm, ROUTE_ROWS), lambda i: (i, 0)),
            pl.BlockSpec((1, d), lambda i: (0, 0)),
        ],
        out_specs=pl.BlockSpec((tm, d), lambda i: (i, 0)),
        scratch_shapes=[
            pltpu.SMEM((ROUTE_SLOTS * TOP_K * tm,), jnp.int32), pltpu.VMEM((2 * TOP_K * tm * ROW_CHUNKS, LANES), F32),
            pltpu.SemaphoreType.DMA((ROUTE_SLOTS,)), pltpu.SemaphoreType.DMA((2,)),
        ],
        compiler_params=_cparams(("arbitrary",)),
        name="moe_combine",
    )(route, ys, x, g2, gates_col, final_g)


def _moe_plan(counts, n_tiles):
    counts = counts.astype(jnp.int32)
    tiles = (counts + EXPERT_TILE - 1) // EXPERT_TILE
    tile_end = jnp.cumsum(tiles)
    pend = tile_end * EXPERT_TILE
    pstart = pend - tiles * EXPERT_TILE
    n_used = tile_end[-1:]
    tile = jnp.minimum(jnp.arange(n_tiles), n_used - 1)
    tile_e = jnp.sum((tile_end[None, :] <= tile[:, None]).astype(jnp.int32), axis=1)
    return pstart, pend, tile_e, n_used


def _tri(n, rel):
    r = jnp.arange(n)[:, None]
    c = jnp.arange(n)[None, :]
    return rel(r, c).astype(BF16)


def _constants(s):
    return dict(
        tab=_rope_tables(s),
        tri_le=_tri(ATT_TILE, lambda r, c: c <= r),
        tri_gt=_tri(ATT_TILE, lambda r, c: r > c),
        tri_lt=_tri(ATT_TILE, lambda r, c: r < c),
        tri_lt_tok=_tri(TOK_TILE, lambda r, c: r < c),
    )


def _prep_layer(p, l):
    bf = jnp.zeros((1, LANES), F32).at[0, MISC_FA:MISC_FA + HEADS].set(p["b_forget"][l].astype(F32))
    w, wt = _reorder_w_in(p["w_in"][l])
    return dict(
        w_in=w, w_in_t=wt,
        wkv=jnp.concatenate([p["w_uk"][l], p["w_uv"][l]], axis=-1).astype(BF16),
        wuvt=p["w_uv"][l].T.astype(BF16),
        kvg=p["kv_norm_g"][l].reshape(1, KV_RANK).astype(F32),
        bf=bf,
        g1=p["norm1_g"][l].reshape(1, -1).astype(F32),
        chunk_bias=_chunk_bias_table(p["rel_bias"][l], ATT_TILE),
    )


def _mixers(x, sc1, sh1, lp, cst):
    tq = ATT_TILE
    (fq, fk, fvt, dq, dkv, dvt, dqi, dkit, dwt, cq, ck, cvt, eq, ek, ev) = _input_stage(
        x, sc1, sh1, lp["g1"], lp["w_in"], lp["w_in_t"], lp["wkv"], lp["wuvt"], lp["kvg"], lp["bf"],
        cst["tri_le"], cst["tab"], tq)
    ya = _fox_attention(fq, fk, fvt, tq)
    yb = _dsa_attention(dq, dkv, dvt, dqi, dkit, dwt, cst["tri_gt"], tq)
    yc = _sb_attention(cq, ck, cvt, cst["tri_lt"], tq)
    yd = _chunk_attention(eq, ek, ev, lp["chunk_bias"], tq)
    return ya, yb, yc, yd


def _moe(p, l, x, h, route, gate, counts, g2, tiles_per_batch, final):
    t, d = x.shape
    nt = t // TOK_TILE
    n_tiles = t * TOP_K // EXPERT_TILE + N_EXPERTS
    pstart, pend, tile_e, n_used = _moe_plan(counts, n_tiles)
    ids, ranks = route[:, :TOP_K, :], route[:, TOP_K:, :]
    first_row = sum(jnp.where(ids == e, pstart[e], 0) for e in range(N_EXPERTS))
    route = (first_row + ranks).reshape(nt, TOP_K * TOK_TILE)
    xs = _dispatch(pstart, pend, route, h.reshape(t, ROW_CHUNKS, LANES), n_tiles * EXPERT_TILE)
    ys = _experts(tile_e, n_used, xs,
                  p["w_gu"][l].astype(BF16), p["b_gu"][l][:, None, :].astype(F32),
                  p["w_dn"][l].astype(BF16), p["b_dn"][l][:, None, :].astype(F32))
    gates_col = gate.transpose(0, 2, 1).reshape(t, ROUTE_ROWS)
    return _combine(route, ys, x, g2, gates_col, p["final_g"].reshape(1, d).astype(F32), tiles_per_batch, final)


def kernel(x, c, w_ada, b_ada, norm1_g, norm2_g, w_in, b_forget, kv_norm_g, w_uk, w_uv, rel_bias, w_o,
           w_router, b_router, w_gu, b_gu, w_dn, b_dn, final_g):
    p = dict(w_in=w_in, b_forget=b_forget, kv_norm_g=kv_norm_g, w_uk=w_uk, w_uv=w_uv, rel_bias=rel_bias,
             norm1_g=norm1_g, w_gu=w_gu, b_gu=b_gu, w_dn=w_dn, b_dn=b_dn, final_g=final_g)
    b, s, d = x.shape
    t = b * s
    depth = w_ada.shape[0]
    tiles_per_batch = s // TOK_TILE
    mod = _modulation(c, w_ada, b_ada)
    cst = _constants(s)
    for l in range(depth):
        sh1, sc1, g1, sh2, sc2, g2 = [m[:, None, :] for m in jnp.split(mod[l], 6, axis=-1)]
        ys = _mixers(x, sc1, sh1, _prep_layer(p, l), cst)
        x2, h, route, gate, cnt = _output_stage(
            [y.reshape(t, GROUP_WIDTH) for y in ys], w_o[l].astype(BF16), x.reshape(t, d), g1, sc2, sh2,
            norm2_g[l].reshape(1, d).astype(F32), w_router[l].T.astype(BF16),
            b_router[l].reshape(N_EXPERTS, 1).astype(F32), cst["tri_lt_tok"], tiles_per_batch)
        x = _moe(p, l, x2, h, route, gate, cnt[:, 0], g2, tiles_per_batch, l == depth - 1).reshape(b, s, d)
    return x
```

```python
import functools
import math

import jax
import jax.numpy as jnp
from jax import lax
from jax.experimental import pallas as pl
from jax.experimental.pallas import tpu as pltpu

F32 = jnp.float32
BF16 = jnp.bfloat16

HEAD_DIM = 64
HEADS = 4
GROUP_WIDTH = HEADS * HEAD_DIM
CHUNK = 64
ROPE_THETA = 10000.0
NORM_EPS = 1e-6
NEG_INF = -1e30
KV_RANK = 128
IDX_HEADS = 4
IDX_DIM = 32
TOPK_MAX = 256
LEFT_CHUNKS = 8
MAX_REL = 128
N_EXPERTS = 32
TOP_K = 4
SWIGLU_ALPHA = 1.702
SWIGLU_LIMIT = 7.0
LOG2E = math.log2(math.e)
INT_MIN = -2 ** 31

LANES = 128
SUBLANES = 8
VMEM_LIMIT = 48 * 1024 * 1024

ATT_TILE = 256
TOK_TILE = 512
EXPERT_TILE = 512
ROUTE_ROWS = 2 * TOP_K
DMA_UNROLL = 8

COL_A = 0
COL_C = 3 * GROUP_WIDTH
COL_D = 6 * GROUP_WIDTH
COL_QB = 9 * GROUP_WIDTH
COL_LAT = COL_QB + GROUP_WIDTH
COL_QI = COL_LAT + KV_RANK
COL_MISC = COL_QI + IDX_HEADS * IDX_DIM
N_COLS = COL_MISC + LANES
MISC_WI = IDX_DIM
MISC_FA = IDX_DIM + IDX_HEADS
ROW_VA = 0
ROW_VC = GROUP_WIDTH
ROW_WI = 2 * GROUP_WIDTH
N_ROWS_T = ROW_WI + SUBLANES


def _cparams(sem):
    return pltpu.CompilerParams(dimension_semantics=sem, vmem_limit_bytes=VMEM_LIMIT)


def _split3(x):
    hi = x.astype(BF16)
    r = x - hi.astype(F32)
    mid = r.astype(BF16)
    lo = (r - mid.astype(F32)).astype(BF16)
    return hi, mid, lo


def _dot(a, b):
    return jnp.dot(a, b, preferred_element_type=F32)


def _dot_nt(a, b):
    return lax.dot_general(a, b, (((1,), (1,)), ((), ())), preferred_element_type=F32)


def _mod_body(c_ref, w_ref, b_ref, o_ref):
    c = c_ref[...]
    act = (c * jax.nn.sigmoid(c)).astype(BF16)
    o_ref[0] = _dot(act, w_ref[0].astype(BF16)) + b_ref[0]


def _modulation(c, w_ada, b_ada):
    depth, d, n = w_ada.shape
    b = c.shape[0]
    tn = 1024
    return pl.pallas_call(
        _mod_body,
        out_shape=jax.ShapeDtypeStruct((depth, b, n), F32),
        grid=(depth, n // tn),
        in_specs=[
            pl.BlockSpec((b, d), lambda l, j: (0, 0)),
            pl.BlockSpec((1, d, tn), lambda l, j: (l, 0, j)),
            pl.BlockSpec((1, 1, tn), lambda l, j: (l, 0, j)),
        ],
        out_specs=pl.BlockSpec((1, b, tn), lambda l, j: (l, 0, j)),
        compiler_params=_cparams(("arbitrary", "arbitrary")),
        name="adaln_mod",
    )(c, w_ada, b_ada.reshape(depth, 1, n))


def _rope_tables(s):
    pos = jnp.arange(s, dtype=F32)[:, None]
    lane = jnp.arange(LANES)[None, :]

    def tables(dim, active):
        half = dim // 2
        j = lane % dim
        inv = ROPE_THETA ** (-(j % half).astype(F32) / half)
        ang = pos * inv
        cos = jnp.where(active, jnp.cos(ang), 1.0)
        sin = jnp.where(active, jnp.sin(ang), 0.0)
        first = j < half
        return [cos, jnp.where(first, -sin, 0.0), jnp.where(first, 0.0, sin)]

    everywhere = lane >= 0
    t = (tables(HEAD_DIM, everywhere) + tables(HEAD_DIM, lane < HEAD_DIM)
         + tables(IDX_DIM, everywhere) + tables(IDX_DIM, lane < IDX_DIM))
    return jnp.stack(t).astype(F32)


def _rope(x, tab, k, half):
    cos, sin_a, sin_b = tab[3 * k], tab[3 * k + 1], tab[3 * k + 2]
    return x * cos + pltpu.roll(x, LANES - half, 1) * sin_a + pltpu.roll(x, half, 1) * sin_b


def _instage_body(x_ref, sc_ref, sh_ref, g_ref, w_ref, wt_ref, wkv_ref, wuvt_ref, kvg_ref, bf_ref, tri_ref, tab_ref,
                  fq_o, fk_o, fvt_o, dq_o, dkv_o, dvt_o, dqi_o, dkit_o, dwt_o,
                  cq_o, ck_o, cvt_o, eq_o, ek_o, ev_o, carry_ref):
    tm = x_ref.shape[1]
    x = x_ref[0]
    y = x * lax.rsqrt(jnp.mean(x * x, axis=-1, keepdims=True) + NORM_EPS) * g_ref[...]
    h = (y * (1.0 + sc_ref[0]) + sh_ref[0]).astype(BF16)

    def proj(c0, n):
        return _dot(h, w_ref[:, c0:c0 + n])

    def proj_t(r0, n):
        return _dot_nt(wt_ref[r0:r0 + n, :], h)

    def heads_out(val, refs, scale):
        for i, ref in enumerate(refs):
            for hd in range(HEADS):
                c0 = i * GROUP_WIDTH + hd * HEAD_DIM
                piece = val[:, c0:c0 + HEAD_DIM]
                if i == 0:
                    piece = piece * scale
                ref[0, hd] = piece.astype(ref.dtype)

    def heads_out_t(val, ref):
        for hd in range(HEADS):
            ref[0, hd] = val[hd * HEAD_DIM:(hd + 1) * HEAD_DIM, :].astype(ref.dtype)

    scale = HEAD_DIM ** -0.5
    heads_out(proj(COL_C, 2 * GROUP_WIDTH), (cq_o, ck_o), scale)
    heads_out_t(proj_t(ROW_VC, GROUP_WIDTH), cvt_o)
    heads_out(proj(COL_D, 3 * GROUP_WIDTH), (eq_o, ek_o, ev_o), scale)

    misc = _rope(proj(COL_MISC, LANES), tab_ref, 3, IDX_DIM // 2)
    lane = lax.broadcasted_iota(jnp.int32, (tm, LANES), 1)
    ki = jnp.where(lane < IDX_DIM, misc, 0.0)
    kit = ki + pltpu.roll(ki, IDX_DIM, 1) + pltpu.roll(ki, 2 * IDX_DIM, 1) + pltpu.roll(ki, 3 * IDX_DIM, 1)
    dkit_o[0] = kit.astype(BF16)
    dwt_o[0] = proj_t(ROW_WI, SUBLANES) * ((IDX_HEADS ** -0.5) * (IDX_DIM ** -0.5))
    qi = _rope(proj(COL_QI, LANES), tab_ref, 2, IDX_DIM // 2)
    for hd in range(IDX_HEADS):
        in_head = (lane >= hd * IDX_DIM) & (lane < (hd + 1) * IDX_DIM)
        dqi_o[0, hd] = jnp.where(in_head, qi, 0.0).astype(BF16)
    lat = proj(COL_LAT, KV_RANK)
    lat = (lat * lax.rsqrt(jnp.mean(lat * lat, axis=-1, keepdims=True) + NORM_EPS) * kvg_ref[...]).astype(BF16)
    dkv_o[0] = _rope(_dot(lat, wkv_ref[...]), tab_ref, 1, HEAD_DIM // 2).astype(BF16)
    dvt_o[0] = _dot_nt(wuvt_ref[...], lat).astype(BF16)
    qb = proj(COL_QB, GROUP_WIDTH)
    zeros64 = jnp.zeros((tm, HEAD_DIM), F32)
    for half_i in range(2):
        r = _rope(qb[:, half_i * LANES:(half_i + 1) * LANES], tab_ref, 0, HEAD_DIM // 2) * (scale * LOG2E)
        for j in range(2):
            piece = jnp.concatenate([r[:, j * HEAD_DIM:(j + 1) * HEAD_DIM], zeros64], axis=-1)
            dq_o[0, 2 * half_i + j] = piece.astype(BF16)

    @pl.when(pl.program_id(1) == 0)
    def _():
        carry_ref[...] = jnp.zeros_like(carry_ref)

    z = misc + bf_ref[...]
    log_f = jnp.minimum(z, 0.0) - jnp.log1p(jnp.exp(-jnp.abs(z)))
    tri = tri_ref[...]
    f_cum = carry_ref[...] + sum(_dot(tri, p) for p in _split3(log_f))
    carry_ref[...] = f_cum[tm - 1:tm, :]
    f_cum = f_cum * LOG2E
    pa = proj(COL_A, 2 * GROUP_WIDTH)
    heads_out_t(proj_t(ROW_VA, GROUP_WIDTH), fvt_o)
    lane64 = lax.broadcasted_iota(jnp.int32, (tm, HEAD_DIM), 1)
    for hd in range(HEADS):
        fh = f_cum[:, MISC_FA + hd:MISC_FA + hd + 1]
        f3 = [jnp.broadcast_to(p.astype(F32), (tm, HEAD_DIM)) for p in _split3(fh)]
        one = jnp.where(lane64 < 6, 1.0, 0.0)
        q_ext = jnp.where(lane64 == 0, f3[0], jnp.where(lane64 == 1, f3[1], jnp.where(lane64 == 2, f3[2], one)))
        k_ext = jnp.where(lane64 == 3, -f3[0], jnp.where(lane64 == 4, -f3[1], jnp.where(lane64 == 5, -f3[2], one)))
        q = pa[:, hd * HEAD_DIM:(hd + 1) * HEAD_DIM] * (scale * LOG2E)
        k = pa[:, GROUP_WIDTH + hd * HEAD_DIM:GROUP_WIDTH + (hd + 1) * HEAD_DIM]
        fq_o[0, hd] = jnp.concatenate([q, q_ext], axis=-1).astype(BF16)
        fk_o[0, hd] = jnp.concatenate([k, k_ext], axis=-1).astype(BF16)


def _reorder_w_in(w_in):
    gw = GROUP_WIDTH
    a0 = 0
    b0 = 3 * gw + HEADS
    c0 = b0 + gw + KV_RANK + IDX_HEADS * IDX_DIM + IDX_DIM + IDX_HEADS
    d0 = c0 + 3 * gw
    lat0 = b0 + gw
    qi0 = lat0 + KV_RANK
    ki0 = qi0 + IDX_HEADS * IDX_DIM
    wi0 = ki0 + IDX_DIM
    pad = jnp.zeros(w_in.shape[:-1] + (LANES - IDX_DIM - IDX_HEADS - HEADS,), w_in.dtype)
    w = jnp.concatenate([
        w_in[:, a0:a0 + 3 * gw], w_in[:, c0:c0 + 3 * gw], w_in[:, d0:d0 + 3 * gw],
        w_in[:, b0:b0 + gw], w_in[:, lat0:lat0 + KV_RANK], w_in[:, qi0:qi0 + IDX_HEADS * IDX_DIM],
        w_in[:, ki0:ki0 + IDX_DIM], w_in[:, wi0:wi0 + IDX_HEADS], w_in[:, 3 * gw:3 * gw + HEADS], pad,
    ], axis=-1)
    wt = jnp.concatenate([
        w_in[:, a0 + 2 * gw:a0 + 3 * gw], w_in[:, c0 + 2 * gw:c0 + 3 * gw], w_in[:, wi0:wi0 + IDX_HEADS],
        jnp.zeros((w_in.shape[0], SUBLANES - IDX_HEADS), w_in.dtype),
    ], axis=-1).T
    return w.astype(BF16), wt.astype(BF16)


def _input_stage(x, sc, sh, g, w, wt, wkv, wuvt, kvg, bf, tri, tab, tm):
    b, s, d = x.shape
    hq = lambda width: jax.ShapeDtypeStruct((b, HEADS, s, width), BF16)
    hqt = jax.ShapeDtypeStruct((b, HEADS, HEAD_DIM, s), BF16)
    flat = jax.ShapeDtypeStruct((b, s, LANES), BF16)
    out_shape = (
        hq(LANES), hq(LANES), hqt,
        hq(LANES), flat, jax.ShapeDtypeStruct((b, HEAD_DIM, s), BF16),
        hq(LANES), flat, jax.ShapeDtypeStruct((b, SUBLANES, s), F32),
        hq(HEAD_DIM), hq(HEAD_DIM), hqt,
        hq(HEAD_DIM), hq(HEAD_DIM), hq(HEAD_DIM),
    )
    hspec = lambda width: pl.BlockSpec((1, HEADS, tm, width), lambda i, j: (i, 0, j, 0))
    htspec = pl.BlockSpec((1, HEADS, HEAD_DIM, tm), lambda i, j: (i, 0, 0, j))
    fspec = pl.BlockSpec((1, tm, LANES), lambda i, j: (i, j, 0))
    tspec = lambda rows: pl.BlockSpec((1, rows, tm), lambda i, j: (i, 0, j))
    const2 = lambda shape: pl.BlockSpec(shape, lambda i, j: (0, 0))
    out_specs = (
        hspec(LANES), hspec(LANES), htspec,
        hspec(LANES), fspec, tspec(HEAD_DIM),
        hspec(LANES), fspec, tspec(SUBLANES),
        hspec(HEAD_DIM), hspec(HEAD_DIM), htspec,
        hspec(HEAD_DIM), hspec(HEAD_DIM), hspec(HEAD_DIM),
    )
    return pl.pallas_call(
        _instage_body,
        out_shape=out_shape,
        grid=(b, s // tm),
        in_specs=[
            pl.BlockSpec((1, tm, d), lambda i, j: (i, j, 0)),
            pl.BlockSpec((1, 1, d), lambda i, j: (i, 0, 0)),
            pl.BlockSpec((1, 1, d), lambda i, j: (i, 0, 0)),
            const2((1, d)),
            const2((d, N_COLS)),
            const2((N_ROWS_T, d)),
            const2((KV_RANK, LANES)),
            const2((HEAD_DIM, KV_RANK)),
            const2((1, KV_RANK)),
            const2((1, LANES)),
            const2((tm, tm)),
            pl.BlockSpec((12, tm, LANES), lambda i, j: (0, j, 0)),
        ],
        out_specs=out_specs,
        scratch_shapes=[pltpu.VMEM((1, LANES), F32)],
        compiler_params=_cparams(("arbitrary", "arbitrary")),
        name="mixer_input_stage",
    )(x, sc, sh, g, w, wt, wkv, wuvt, kvg, bf, tri, tab)


def _key_query_iota(t):
    return lax.broadcasted_iota(jnp.int32, (t, t), 0), lax.broadcasted_iota(jnp.int32, (t, t), 1)


def _softmax2_step(s, vt, m, l, acc):
    m_new = jnp.maximum(m, jnp.max(s, axis=0, keepdims=True))
    alpha = jnp.exp2(m - m_new)
    p = jnp.exp2(s - m_new)
    l_new = alpha * l + jnp.sum(p, axis=0, keepdims=True)
    acc_new = alpha * acc + _dot(vt, p.astype(BF16))
    return m_new, l_new, acc_new


def _skewed(n, stages):
    vals = [None] * n
    for step in range(n + len(stages) - 1):
        for j, stage in enumerate(stages):
            i = step - j
            if 0 <= i < n:
                vals[i] = stage(i, vals[i])


def _heads_to_rows(accs):
    return jnp.concatenate([a.T for a in accs], axis=-1)


def _fox_body(q_ref, k_ref, vt_ref, o_ref, m_ref, l_ref, acc_ref):
    qi = pl.program_id(1)
    tq = q_ref.shape[2]
    key, qry = _key_query_iota(tq)
    m_ref[...] = jnp.full_like(m_ref, NEG_INF)
    l_ref[...] = jnp.zeros_like(l_ref)
    acc_ref[...] = jnp.zeros_like(acc_ref)

    def blocks(kbs):
        items = [(pl.multiple_of(kb * tq, tq), masked, hd) for kb, masked in kbs for hd in range(HEADS)]

        def scores(i, _):
            k0, masked, hd = items[i]
            s = _dot_nt(k_ref[0, hd, pl.ds(k0, tq), :], q_ref[0, hd])
            return jnp.where(key <= qry, s, NEG_INF) if masked else s

        def update(i, s):
            k0, _, hd = items[i]
            m_ref[hd], l_ref[hd], acc_ref[hd] = _softmax2_step(
                s, vt_ref[0, hd, :, pl.ds(k0, tq)], m_ref[hd], l_ref[hd], acc_ref[hd])

        _skewed(len(items), [scores, update])

    def pair_body(j, carry):
        blocks([(2 * j, False), (2 * j + 1, False)])
        return carry

    lax.fori_loop(0, lax.shift_right_logical(qi, 1), pair_body, 0)

    @pl.when((qi & 1) == 1)
    def _():
        blocks([(qi - 1, False), (qi, True)])

    @pl.when((qi & 1) == 0)
    def _():
        blocks([(qi, True)])

    o_ref[0] = _heads_to_rows([acc_ref[hd] / l_ref[hd] for hd in range(HEADS)]).astype(o_ref.dtype)


def _fox_attention(q, k, vt, tq):
    b, _, s, _ = q.shape
    return pl.pallas_call(
        _fox_body,
        out_shape=jax.ShapeDtypeStruct((b, s, GROUP_WIDTH), BF16),
        grid=(b, s // tq),
        in_specs=[
            pl.BlockSpec((1, HEADS, tq, LANES), lambda i, j: (i, 0, j, 0)),
            pl.BlockSpec((1, HEADS, s, LANES), lambda i, j: (i, 0, 0, 0)),
            pl.BlockSpec((1, HEADS, HEAD_DIM, s), lambda i, j: (i, 0, 0, 0)),
        ],
        out_specs=pl.BlockSpec((1, tq, GROUP_WIDTH), lambda i, j: (i, j, 0)),
        scratch_shapes=[pltpu.VMEM((HEADS, 1, tq), F32), pltpu.VMEM((HEADS, 1, tq), F32),
                        pltpu.VMEM((HEADS, HEAD_DIM, tq), F32)],
        compiler_params=_cparams(("arbitrary", "arbitrary")),
        name="forgetting_attention",
    )(q, k, vt)


def _sb_body(q_ref, k_ref, vt_ref, tri_ref, o_ref, r_ref, acc_ref):
    qi = pl.program_id(1)
    tq = q_ref.shape[2]
    key, qry = _key_query_iota(tq)
    tri = tri_ref[...]
    r_ref[...] = jnp.zeros_like(r_ref)
    acc_ref[...] = jnp.zeros_like(acc_ref)

    def blocks(kbs):
        items = [(pl.multiple_of(kb * tq, tq), masked, hd) for kb, masked in kbs for hd in range(HEADS)]

        def logits(i, _):
            k0, _, hd = items[i]
            return _dot_nt(k_ref[0, hd, pl.ds(k0, tq), :], q_ref[0, hd])

        def later_sums(i, z):
            _, masked, hd = items[i]
            log_1m = -(jnp.maximum(z, 0.0) + jnp.log1p(jnp.exp(-jnp.abs(z))))
            if masked:
                log_1m = jnp.where(key < qry, log_1m, 0.0)
            after = r_ref[hd] + _dot(tri, log_1m.astype(BF16))
            return z, log_1m, after

        def accumulate(i, val):
            k0, masked, hd = items[i]
            z, log_1m, after = val
            w = jnp.exp(log_1m + z + after)
            if masked:
                w = jnp.where(key < qry, w, 0.0)
            acc_ref[hd] += _dot(vt_ref[0, hd, :, pl.ds(k0, tq)], w.astype(BF16))
            r_ref[hd] += jnp.sum(log_1m, axis=0, keepdims=True)

        _skewed(len(items), [logits, later_sums, accumulate])

    @pl.when((qi & 1) == 1)
    def _():
        blocks([(qi, True), (qi - 1, False)])

    @pl.when((qi & 1) == 0)
    def _():
        blocks([(qi, True)])

    top = qi - 1 - (qi & 1)

    def pair_body(j, carry):
        blocks([(top - 2 * j, False), (top - 2 * j - 1, False)])
        return carry

    lax.fori_loop(0, lax.shift_right_logical(qi, 1), pair_body, 0)
    o_ref[0] = _heads_to_rows([acc_ref[hd] for hd in range(HEADS)]).astype(o_ref.dtype)


def _sb_attention(q, k, vt, tri, tq):
    b, _, s, _ = q.shape
    return pl.pallas_call(
        _sb_body,
        out_shape=jax.ShapeDtypeStruct((b, s, GROUP_WIDTH), BF16),
        grid=(b, s // tq),
        in_specs=[
            pl.BlockSpec((1, HEADS, tq, HEAD_DIM), lambda i, j: (i, 0, j, 0)),
            pl.BlockSpec((1, HEADS, s, HEAD_DIM), lambda i, j: (i, 0, 0, 0)),
            pl.BlockSpec((1, HEADS, HEAD_DIM, s), lambda i, j: (i, 0, 0, 0)),
            pl.BlockSpec((tq, tq), lambda i, j: (0, 0)),
        ],
        out_specs=pl.BlockSpec((1, tq, GROUP_WIDTH), lambda i, j: (i, j, 0)),
        scratch_shapes=[pltpu.VMEM((HEADS, 1, tq), F32), pltpu.VMEM((HEADS, HEAD_DIM, tq), F32)],
        compiler_params=_cparams(("arbitrary", "arbitrary")),
        name="stick_breaking_attention",
    )(q, k, vt, tri)


def _chunk_bias_table(rel_table, tq):
    left = LEFT_CHUNKS * CHUNK
    width = left + tq
    d = jnp.arange(width + tq - 1) - (tq - 1) - left
    diag = rel_table[:, jnp.clip(d, -MAX_REL, MAX_REL) + MAX_REL].astype(F32)
    bias = jnp.stack([diag[:, tq - 1 - t:tq - 1 - t + width] for t in range(tq)], axis=1)
    t = jnp.arange(tq)[:, None]
    j = jnp.arange(width)[None, :]
    c0 = (t // CHUNK) * CHUNK
    visible = (j >= c0) & (j - left < c0 + CHUNK)
    return jnp.where(visible[None], bias, NEG_INF)


def _chunk_body(nb, q_ref, *refs):
    k_refs, v_refs = refs[:nb], refs[nb:2 * nb]
    bias_ref, o_ref = refs[2 * nb], refs[2 * nb + 1]
    qi = pl.program_id(1)
    outs = []
    for hd in range(HEADS):
        q = q_ref[0, hd]
        s_parts = []
        for n in range(nb):
            s = _dot_nt(q, k_refs[n][0, hd])
            s_parts.append(jnp.where(qi - (nb - 1) + n >= 0, s, NEG_INF))
        s = jnp.concatenate(s_parts, axis=-1) + bias_ref[hd]
        m = jnp.max(s, axis=-1, keepdims=True)
        p = jnp.exp(s - m)
        l = jnp.sum(p, axis=-1, keepdims=True)
        p = p.astype(BF16)
        tq = q.shape[0]
        acc = sum(_dot(p[:, n * tq:(n + 1) * tq], v_refs[n][0, hd]) for n in range(nb))
        outs.append(acc / l)
    o_ref[0] = jnp.concatenate(outs, axis=-1).astype(o_ref.dtype)


def _chunk_attention(q, k, v, bias, tq):
    b, _, s, _ = q.shape
    left = LEFT_CHUNKS * CHUNK
    assert left % tq == 0
    nb = left // tq + 1
    kv_specs = [
        pl.BlockSpec((1, HEADS, tq, HEAD_DIM), functools.partial(
            lambda i, j, n: (i, 0, jnp.maximum(j - (nb - 1) + n, 0), 0), n=n))
        for n in range(nb)
    ]
    return pl.pallas_call(
        functools.partial(_chunk_body, nb),
        out_shape=jax.ShapeDtypeStruct((b, s, GROUP_WIDTH), BF16),
        grid=(b, s // tq),
        in_specs=[pl.BlockSpec((1, HEADS, tq, HEAD_DIM), lambda i, j: (i, 0, j, 0))] + kv_specs + kv_specs
        + [pl.BlockSpec((HEADS, tq, nb * tq), lambda i, j: (0, 0, 0))],
        out_specs=pl.BlockSpec((1, tq, GROUP_WIDTH), lambda i, j: (i, j, 0)),
        compiler_params=_cparams(("arbitrary", "arbitrary")),
        name="chunked_relbias_attention",
    )(q, *([k] * nb), *([v] * nb), bias)


def _dsa_body(n_sel, q_ref, kv_ref, vt_ref, qi_ref, kit_ref, wt_ref, tri_ref, o_ref,
              key_ref, bias_ref, m_ref, l_ref, acc_ref):
    i = pl.program_id(1)
    tq = q_ref.shape[2]
    nkb = i + 1
    key_pos, qry_pos = _key_query_iota(tq)
    vis_diag = key_pos < (qry_pos // CHUNK + 1) * CHUNK
    wt = wt_ref[0]

    def score_block(kb, masked):
        k0 = pl.multiple_of(kb * tq, tq)
        kit = kit_ref[0, pl.ds(k0, tq), :]
        dots = [_dot_nt(kit, qi_ref[0, hd]) for hd in range(IDX_HEADS)]
        score = sum(wt[hd:hd + 1, :] * jnp.maximum(dots[hd], 0.0) for hd in range(IDX_HEADS))
        score = jnp.where(score == 0.0, 0.0, score)
        bits = pltpu.bitcast(score, jnp.int32)
        okey = jnp.where(bits < 0, bits ^ 0x7FFFFFFF, bits)
        if masked:
            okey = jnp.where(vis_diag, okey, INT_MIN)
        key_ref[kb] = okey

    def score_loop(kb, carry):
        score_block(kb, False)
        return carry

    lax.fori_loop(0, i, score_loop, 0)
    score_block(i, True)

    def count(pred):
        def hits(kb):
            ind = jnp.where(pred(key_ref[kb]), 1.0, 0.0)
            return ind.reshape(tq // SUBLANES, SUBLANES, tq).sum(axis=0)

        def pair(j, acc):
            return acc + hits(2 * j) + hits(2 * j + 1)

        acc = lax.fori_loop(0, lax.shift_right_logical(nkb, 1), pair, jnp.zeros((SUBLANES, tq), F32))
        acc = acc + lax.cond((nkb & 1) == 1, lambda: hits(nkb - 1), lambda: jnp.zeros((SUBLANES, tq), F32))
        return jnp.sum(acc, axis=0, keepdims=True)

    def bit_step(it, ans_u):
        cand_u = ans_u | lax.shift_left(jnp.int32(1), 31 - it)
        cand = cand_u ^ INT_MIN
        return jnp.where(count(lambda k: k >= cand) >= n_sel, cand_u, ans_u)

    thr = lax.fori_loop(0, 32, bit_step, jnp.zeros((1, tq), jnp.int32)) ^ INT_MIN

    n_gt = count(lambda k: k > thr)
    need = jnp.where(thr != INT_MIN, n_sel - n_gt, 0.0)
    tri = tri_ref[...]

    def select_block(kb, seen):
        okey = key_ref[kb]
        eq = okey == thr
        eq_f = jnp.where(eq, 1.0, 0.0)
        earlier = seen + _dot(tri, eq_f.astype(BF16))
        tie_bias = jnp.where(eq, jnp.where(earlier < need, 0.0, NEG_INF), NEG_INF)
        bias_ref[kb] = jnp.where(okey > thr, 0.0, tie_bias)
        return seen + jnp.sum(eq_f, axis=0, keepdims=True)

    lax.fori_loop(0, nkb, select_block, jnp.zeros((1, tq), F32))

    m_ref[...] = jnp.full_like(m_ref, NEG_INF)
    l_ref[...] = jnp.zeros_like(l_ref)
    acc_ref[...] = jnp.zeros_like(acc_ref)

    def attend(kbs):
        items = [(kb, pl.multiple_of(kb * tq, tq), hd) for kb in kbs for hd in range(HEADS)]

        def scores(n, _):
            kb, k0, hd = items[n]
            return _dot_nt(kv_ref[0, pl.ds(k0, tq), :], q_ref[0, hd]) + bias_ref[kb]

        def update(n, s):
            _, k0, hd = items[n]
            m_ref[hd], l_ref[hd], acc_ref[hd] = _softmax2_step(
                s, vt_ref[0, :, pl.ds(k0, tq)], m_ref[hd], l_ref[hd], acc_ref[hd])

        _skewed(len(items), [scores, update])

    def attend_pair(j, carry):
        attend([2 * j, 2 * j + 1])
        return carry

    lax.fori_loop(0, lax.shift_right_logical(nkb, 1), attend_pair, 0)

    @pl.when((nkb & 1) == 1)
    def _():
        attend([nkb - 1])
    o_ref[0] = _heads_to_rows([acc_ref[hd] / l_ref[hd] for hd in range(HEADS)]).astype(o_ref.dtype)


def _dsa_attention(q, kv, vt, qi, kit, wt, tri, tq):
    b, _, s, _ = q.shape
    n_sel = min(TOPK_MAX, s // 4)
    return pl.pallas_call(
        functools.partial(_dsa_body, n_sel),
        out_shape=jax.ShapeDtypeStruct((b, s, GROUP_WIDTH), BF16),
        grid=(b, s // tq),
        in_specs=[
            pl.BlockSpec((1, HEADS, tq, LANES), lambda i, j: (i, 0, j, 0)),
            pl.BlockSpec((1, s, LANES), lambda i, j: (i, 0, 0)),
            pl.BlockSpec((1, HEAD_DIM, s), lambda i, j: (i, 0, 0)),
            pl.BlockSpec((1, IDX_HEADS, tq, LANES), lambda i, j: (i, 0, j, 0)),
            pl.BlockSpec((1, s, LANES), lambda i, j: (i, 0, 0)),
            pl.BlockSpec((1, SUBLANES, tq), lambda i, j: (i, 0, j)),
            pl.BlockSpec((tq, tq), lambda i, j: (0, 0)),
        ],
        out_specs=pl.BlockSpec((1, tq, GROUP_WIDTH), lambda i, j: (i, j, 0)),
        scratch_shapes=[
            pltpu.VMEM((s // tq, tq, tq), jnp.int32), pltpu.VMEM((s // tq, tq, tq), F32),
            pltpu.VMEM((HEADS, 1, tq), F32), pltpu.VMEM((HEADS, 1, tq), F32),
            pltpu.VMEM((HEADS, HEAD_DIM, tq), F32),
        ],
        compiler_params=_cparams(("arbitrary", "arbitrary")),
        name="sparse_indexer_attention",
    )(q, kv, vt, qi, kit, wt, tri)


ROW_CHUNKS = 8


def _store_rows(ref, val, r0=0):
    n = val.shape[0]
    for c in range(ROW_CHUNKS):
        ref[pl.ds(r0 * ROW_CHUNKS + c, n, stride=ROW_CHUNKS), :] = val[:, c * LANES:(c + 1) * LANES]


def _load_rows(ref, r0, n):
    return jnp.concatenate(
        [ref[pl.ds(r0 * ROW_CHUNKS + c, n, stride=ROW_CHUNKS), :] for c in range(ROW_CHUNKS)], axis=-1)


def _outstage_body(ya_ref, yb_ref, yc_ref, yd_ref, wo_ref, x_ref, g1_ref, sc_ref, sh_ref, ng_ref,
                   wr_ref, br_ref, tri_ref, xo_ref, h_ref, route_ref, gate_ref, cnt_ref, carry_ref):
    tm = x_ref.shape[0]
    gw = GROUP_WIDTH
    mix = sum(_dot(r[...], wo_ref[n * gw:(n + 1) * gw, :]) for n, r in enumerate((ya_ref, yb_ref, yc_ref, yd_ref)))
    x = x_ref[...] + g1_ref[0] * mix
    xo_ref[...] = x
    y = x * lax.rsqrt(jnp.mean(x * x, axis=-1, keepdims=True) + NORM_EPS) * ng_ref[...]
    h = y * (1.0 + sc_ref[0]) + sh_ref[0]
    _store_rows(h_ref, h)

    @pl.when(pl.program_id(0) == 0)
    def _():
        carry_ref[...] = jnp.zeros_like(carry_ref)

    logits = _dot_nt(wr_ref[...], h.astype(BF16)) + br_ref[...]
    e_iota = lax.broadcasted_iota(jnp.int32, (N_EXPERTS, tm), 0).astype(F32)
    vals, ids = [], []
    for _ in range(TOP_K):
        top = jnp.max(logits, axis=0, keepdims=True)
        idx = jnp.min(jnp.where(logits == top, e_iota, float(N_EXPERTS)), axis=0, keepdims=True)
        logits = jnp.where(e_iota == idx, -jnp.inf, logits)
        vals.append(top)
        ids.append(idx)
    exps = [jnp.exp(v - vals[0]) for v in vals]
    denom = sum(exps)
    onehot = sum(jnp.where(e_iota == idx, 1.0, 0.0) for idx in ids)
    before = carry_ref[...] + _dot(onehot.astype(BF16), tri_ref[...])
    ranks = [jnp.sum(jnp.where(e_iota == idx, before, 0.0), axis=0, keepdims=True) for idx in ids]
    carry_ref[...] += jnp.sum(onehot, axis=1, keepdims=True)
    route_ref[0] = jnp.concatenate(ids + ranks, axis=0).astype(jnp.int32)
    gate_ref[0] = jnp.concatenate([e / denom for e in exps] + [jnp.zeros_like(denom)] * TOP_K, axis=0)
    cnt_ref[...] = jnp.broadcast_to(carry_ref[...], cnt_ref.shape)


def _output_stage(ys, wo, x, g1, sc2, sh2, ng, wr_t, br, tri, tiles_per_batch):
    t, d = x.shape
    tm = TOK_TILE
    nt = t // tm
    row = lambda width: pl.BlockSpec((tm, width), lambda i: (i, 0))
    const = lambda shape: pl.BlockSpec(shape, lambda i: (0,) * len(shape))
    per_batch = pl.BlockSpec((1, 1, d), lambda i: (i // tiles_per_batch, 0, 0))
    return pl.pallas_call(
        _outstage_body,
        out_shape=(
            jax.ShapeDtypeStruct((t, d), F32), jax.ShapeDtypeStruct((t * ROW_CHUNKS, LANES), F32),
            jax.ShapeDtypeStruct((nt, ROUTE_ROWS, tm), jnp.int32), jax.ShapeDtypeStruct((nt, ROUTE_ROWS, tm), F32),
            jax.ShapeDtypeStruct((N_EXPERTS, LANES), F32),
        ),
        grid=(nt,),
        in_specs=[row(GROUP_WIDTH)] * 4 + [const((d, d)), row(d), per_batch, per_batch, per_batch, const((1, d)),
                                           const((N_EXPERTS, d)), const((N_EXPERTS, 1)), const((tm, tm))],
        out_specs=(row(d), pl.BlockSpec((tm * ROW_CHUNKS, LANES), lambda i: (i, 0)),
                   pl.BlockSpec((1, ROUTE_ROWS, tm), lambda i: (i, 0, 0)),
                   pl.BlockSpec((1, ROUTE_ROWS, tm), lambda i: (i, 0, 0)),
                   const((N_EXPERTS, LANES))),
        scratch_shapes=[pltpu.VMEM((N_EXPERTS, 1), F32)],
        compiler_params=_cparams(("arbitrary",)),
        name="mixer_output_stage_router",
    )(*ys, wo, x, g1, sc2, sh2, ng, wr_t, br, tri)


def _route_fetch(route_hbm, route_smem, rsem):
    i = pl.program_id(0)
    n = route_hbm.shape[1]
    slot = lax.rem(i, 2)

    def fetch(step, sl):
        dst = route_smem.at[pl.ds(pl.multiple_of(sl * n, n), n)]
        return pltpu.make_async_copy(route_hbm.at[step], dst, rsem.at[sl])

    @pl.when(i == 0)
    def _():
        fetch(0, 0).start()

    fetch(i, slot).wait()

    @pl.when(i + 1 < pl.num_programs(0))
    def _():
        fetch(i + 1, 1 - slot).start()

    return slot * n


H_BUFFERS = 3


def _dispatch_body(pstart_ref, pend_ref, route_hbm, h_hbm, xs_hbm, route_smem, hbuf, zero_ref, rsem, hsem, zsem, sem):
    tm = hbuf.shape[1]
    i = pl.program_id(0)
    n = pl.num_programs(0)

    def h_load(step, b):
        return pltpu.make_async_copy(h_hbm.at[pl.ds(pl.multiple_of(step * tm, tm), tm)], hbuf.at[b], hsem.at[b])

    def wait_rows(parity):
        for _ in range(TOP_K):
            pltpu.make_async_copy(hbuf.at[0], xs_hbm.at[pl.ds(0, tm)], sem.at[parity]).wait()

    @pl.when(i == 0)
    def _():
        h_load(0, 0).start()

        @pl.when(n > 1)
        def _():
            h_load(1, 1).start()

        zero_ref[...] = jnp.zeros_like(zero_ref)

        def fill(e):
            start = pl.multiple_of(pend_ref[e] - EXPERT_TILE, EXPERT_TILE)
            return pltpu.make_async_copy(zero_ref, xs_hbm.at[pl.ds(start, EXPERT_TILE)], zsem)

        def start(e, c):
            @pl.when(pend_ref[e] > pstart_ref[e])
            def _():
                fill(e).start()
            return c

        def wait(e, c):
            @pl.when(pend_ref[e] > pstart_ref[e])
            def _():
                fill(e).wait()
            return c

        lax.fori_loop(0, N_EXPERTS, start, 0)
        lax.fori_loop(0, N_EXPERTS, wait, 0)

    rec = _route_fetch(route_hbm, route_smem, rsem)
    b = lax.rem(i, H_BUFFERS)
    parity = lax.rem(i, 2)
    h_load(i, b).wait()
    h_ref = hbuf.at[b]
    for k in range(TOP_K):
        def start(j, c, base=rec + k * tm):
            for prio in range(2):
                t = 2 * j + prio
                pltpu.make_async_copy(h_ref.at[t], xs_hbm.at[route_smem[base + t]], sem.at[parity]).start(priority=prio)
            return c

        lax.fori_loop(0, tm // 2, start, 0, unroll=DMA_UNROLL // 2)

    @pl.when(i > 0)
    def _():
        wait_rows(1 - parity)

    @pl.when(i + 2 < n)
    def _():
        h_load(i + 2, lax.rem(i + 2, H_BUFFERS)).start()

    @pl.when(i == n - 1)
    def _():
        wait_rows(parity)


def _dispatch(pstart, pend, route, h, n_rows):
    t, dc, _ = h.shape
    tm = TOK_TILE
    return pl.pallas_call(
        _dispatch_body,
        out_shape=jax.ShapeDtypeStruct((n_rows, dc, LANES), F32),
        grid_spec=pltpu.PrefetchScalarGridSpec(
            num_scalar_prefetch=2,
            grid=(t // tm,),
            in_specs=[pl.BlockSpec(memory_space=pl.ANY), pl.BlockSpec(memory_space=pl.ANY)],
            out_specs=pl.BlockSpec(memory_space=pl.ANY),
            scratch_shapes=[
                pltpu.SMEM((2 * TOP_K * tm,), jnp.int32), pltpu.VMEM((H_BUFFERS, tm, dc, LANES), F32),
                pltpu.VMEM((EXPERT_TILE, dc, LANES), F32),
                pltpu.SemaphoreType.DMA((2,)), pltpu.SemaphoreType.DMA((H_BUFFERS,)), pltpu.SemaphoreType.DMA,
                pltpu.SemaphoreType.DMA((2,)),
            ],
        ),
        compiler_params=_cparams(("arbitrary",)),
        name="moe_dispatch",
    )(pstart, pend, route, h)


def _expert_body(tile_e_ref, n_used_ref, x_ref, wgu_ref, bgu_ref, wdn_ref, bdn_ref, y_ref):
    @pl.when(pl.program_id(0) < n_used_ref[0])
    def _():
        f = wdn_ref.shape[2]
        gu = _dot(_load_rows(x_ref, 0, EXPERT_TILE).astype(BF16), wgu_ref[0, 0]) + bgu_ref[0, 0]
        glu = jnp.minimum(gu[:, :f], SWIGLU_LIMIT)
        lin = jnp.clip(gu[:, f:], -SWIGLU_LIMIT, SWIGLU_LIMIT)
        act = glu * jax.nn.sigmoid(SWIGLU_ALPHA * glu) * (lin + 1.0)
        _store_rows(y_ref, _dot(act.astype(BF16), wdn_ref[0, 0]) + bdn_ref[0, 0])


def _experts(l, tile_e, n_used, xs, wgu, bgu, wdn, bdn):
    p, dc, _ = xs.shape
    assert dc == ROW_CHUNKS
    d = dc * LANES
    f = wdn.shape[2]
    tm = EXPERT_TILE
    used = lambda i, te, nu: jnp.minimum(i, nu[0] - 1)
    expert = lambda i, te, nu: (l, te[i], 0, 0)
    return pl.pallas_call(
        _expert_body,
        out_shape=jax.ShapeDtypeStruct((p * dc, LANES), F32),
        grid_spec=pltpu.PrefetchScalarGridSpec(
            num_scalar_prefetch=2,
            grid=(p // tm,),
            in_specs=[
                pl.BlockSpec((tm * dc, LANES), lambda i, te, nu: (used(i, te, nu), 0)),
                pl.BlockSpec((1, 1, d, 2 * f), expert),
                pl.BlockSpec((1, 1, 1, 2 * f), expert),
                pl.BlockSpec((1, 1, f, d), expert),
                pl.BlockSpec((1, 1, 1, d), expert),
            ],
            out_specs=pl.BlockSpec((tm * dc, LANES), lambda i, te, nu: (used(i, te, nu), 0)),
        ),
        compiler_params=_cparams(("arbitrary",)),
        name="moe_experts",
    )(tile_e, n_used, xs.reshape(p * dc, LANES), wgu, bgu, wdn, bdn).reshape(p, dc, LANES)


COMBINE_ROWS = 32
ROUTE_SLOTS = 3


def _combine_body(final, route_hbm, ys_hbm, x_ref, g2_ref, gate_ref, fg_ref, o_ref,
                  route_smem, buf_ref, rsem, sem):
    tm = x_ref.shape[0]
    i = pl.program_id(0)
    n = pl.num_programs(0)
    n_rec = TOP_K * tm

    def rec_fetch(step):
        sl = lax.rem(step, ROUTE_SLOTS)
        dst = route_smem.at[pl.ds(pl.multiple_of(sl * n_rec, n_rec), n_rec)]
        return pltpu.make_async_copy(route_hbm.at[step], dst, rsem.at[sl])

    def issue_gathers(step):
        rec = lax.rem(step, ROUTE_SLOTS) * n_rec
        half = lax.rem(step, 2)
        row0 = half * n_rec

        def start(j, c):
            for prio in range(2):
                r = 2 * j + prio
                dst = buf_ref.at[pl.ds(pl.multiple_of((row0 + r) * ROW_CHUNKS, ROW_CHUNKS), ROW_CHUNKS)]
                pltpu.make_async_copy(ys_hbm.at[route_smem[rec + r]], dst, sem.at[half]).start(priority=prio)
            return c

        lax.fori_loop(0, n_rec // 2, start, 0, unroll=DMA_UNROLL // 2)

    @pl.when(i == 0)
    def _():
        rec_fetch(0).start()

        @pl.when(n > 1)
        def _():
            rec_fetch(1).start()

        rec_fetch(0).wait()
        issue_gathers(0)

    @pl.when(i + 1 < n)
    def _():
        rec_fetch(i + 1).wait()
        issue_gathers(i + 1)

    @pl.when(i + 2 < n)
    def _():
        rec_fetch(i + 2).start()

    half = lax.rem(i, 2)
    half_view = buf_ref.at[pl.ds(0, n_rec * ROW_CHUNKS)]
    pltpu.make_async_copy(half_view, half_view, sem.at[half]).wait()
    g2 = g2_ref[0]
    fg = fg_ref[...]

    def rows(chunk, c):
        r0 = pl.multiple_of(chunk * COMBINE_ROWS, COMBINE_ROWS)
        gates = gate_ref[pl.ds(r0, COMBINE_ROWS), :]
        moe = sum(gates[:, k:k + 1] * _load_rows(buf_ref, half * n_rec + k * tm + r0, COMBINE_ROWS)
                  for k in range(TOP_K))
        x = x_ref[pl.ds(r0, COMBINE_ROWS), :] + g2 * moe
        if final:
            x = x * lax.rsqrt(jnp.mean(x * x, axis=-1, keepdims=True) + NORM_EPS) * fg
        o_ref[pl.ds(r0, COMBINE_ROWS), :] = x
        return c

    lax.fori_loop(0, tm // COMBINE_ROWS, rows, 0)


def _combine(route, ys, x, g2, gates_col, final_g, tiles_per_batch, final):
    t, d = x.shape
    tm = TOK_TILE
    return pl.pallas_call(
        functools.partial(_combine_body, final),
        out_shape=jax.ShapeDtypeStruct((t, d), F32),
        grid=(t // tm,),
        in_specs=[
            pl.BlockSpec(memory_space=pl.ANY), pl.BlockSpec(memory_space=pl.ANY),
            pl.BlockSpec((tm, d), lambda i: (i, 0)),
            pl.BlockSpec((1, 1, d), lambda i: (i // tiles_per_batch, 0, 0)),
            pl.BlockSpec((tm, ROUTE_ROWS), lambda i: (i, 0)),
            pl.BlockSpec((1, d), lambda i: (0, 0)),
        ],
        out_specs=pl.BlockSpec((tm, d), lambda i: (i, 0)),
        scratch_shapes=[
            pltpu.SMEM((ROUTE_SLOTS * TOP_K * tm,), jnp.int32), pltpu.VMEM((2 * TOP_K * tm * ROW_CHUNKS, LANES), F32),
            pltpu.SemaphoreType.DMA((ROUTE_SLOTS,)), pltpu.SemaphoreType.DMA((2,)),
        ],
        compiler_params=_cparams(("arbitrary",)),
        name="moe_combine",
    )(route, ys, x, g2, gates_col, final_g)


def _moe_plan(counts, n_tiles):
    counts = counts.astype(jnp.int32)
    tiles = (counts + EXPERT_TILE - 1) // EXPERT_TILE
    tile_end = jnp.cumsum(tiles)
    pend = tile_end * EXPERT_TILE
    pstart = pend - tiles * EXPERT_TILE
    n_used = tile_end[-1:]
    tile = jnp.minimum(jnp.arange(n_tiles), n_used - 1)
    tile_e = jnp.sum((tile_end[None, :] <= tile[:, None]).astype(jnp.int32), axis=1)
    return pstart, pend, tile_e, n_used


def _tri(n, rel):
    r = jnp.arange(n)[:, None]
    c = jnp.arange(n)[None, :]
    return rel(r, c).astype(BF16)


def _constants(s):
    return dict(
        tab=_rope_tables(s),
        tri_le=_tri(ATT_TILE, lambda r, c: c <= r),
        tri_gt=_tri(ATT_TILE, lambda r, c: r > c),
        tri_lt=_tri(ATT_TILE, lambda r, c: r < c),
        tri_lt_tok=_tri(TOK_TILE, lambda r, c: r < c),
    )


def _prep_layer(p, l):
    bf = jnp.zeros((1, LANES), F32).at[0, MISC_FA:MISC_FA + HEADS].set(p["b_forget"][l].astype(F32))
    w, wt = _reorder_w_in(p["w_in"][l])
    return dict(
        w_in=w, w_in_t=wt,
        wkv=jnp.concatenate([p["w_uk"][l], p["w_uv"][l]], axis=-1).astype(BF16),
        wuvt=p["w_uv"][l].T.astype(BF16),
        kvg=p["kv_norm_g"][l].reshape(1, KV_RANK).astype(F32),
        bf=bf,
        g1=p["norm1_g"][l].reshape(1, -1).astype(F32),
        chunk_bias=_chunk_bias_table(p["rel_bias"][l], ATT_TILE),
    )


def _mixers(x, sc1, sh1, lp, cst):
    tq = ATT_TILE
    (fq, fk, fvt, dq, dkv, dvt, dqi, dkit, dwt, cq, ck, cvt, eq, ek, ev) = _input_stage(
        x, sc1, sh1, lp["g1"], lp["w_in"], lp["w_in_t"], lp["wkv"], lp["wuvt"], lp["kvg"], lp["bf"],
        cst["tri_le"], cst["tab"], tq)
    ya = _fox_attention(fq, fk, fvt, tq)
    yb = _dsa_attention(dq, dkv, dvt, dqi, dkit, dwt, cst["tri_gt"], tq)
    yc = _sb_attention(cq, ck, cvt, cst["tri_lt"], tq)
    yd = _chunk_attention(eq, ek, ev, lp["chunk_bias"], tq)
    return ya, yb, yc, yd


def _moe(p, l, x, h, route, gate, counts, g2, tiles_per_batch, final):
    t, d = x.shape
    nt = t // TOK_TILE
    n_tiles = t * TOP_K // EXPERT_TILE + N_EXPERTS
    pstart, pend, tile_e, n_used = _moe_plan(counts, n_tiles)
    ids, ranks = route[:, :TOP_K, :], route[:, TOP_K:, :]
    first_row = sum(jnp.where(ids == e, pstart[e], 0) for e in range(N_EXPERTS))
    route = (first_row + ranks).reshape(nt, TOP_K * TOK_TILE)
    xs = _dispatch(pstart, pend, route, h.reshape(t, ROW_CHUNKS, LANES), n_tiles * EXPERT_TILE)
    ys = _experts(l, tile_e, n_used, xs, p["w_gu_bf16"], p["b_gu"][:, :, None, :].astype(F32),
                  p["w_dn_bf16"], p["b_dn"][:, :, None, :].astype(F32))
    gates_col = gate.transpose(0, 2, 1).reshape(t, ROUTE_ROWS)
    return _combine(route, ys, x, g2, gates_col, p["final_g"].reshape(1, d).astype(F32), tiles_per_batch, final)


def kernel(x, c, w_ada, b_ada, norm1_g, norm2_g, w_in, b_forget, kv_norm_g, w_uk, w_uv, rel_bias, w_o,
           w_router, b_router, w_gu, b_gu, w_dn, b_dn, final_g):
    p = dict(w_in=w_in, b_forget=b_forget, kv_norm_g=kv_norm_g, w_uk=w_uk, w_uv=w_uv, rel_bias=rel_bias,
             norm1_g=norm1_g, w_gu_bf16=w_gu.astype(BF16), b_gu=b_gu, w_dn_bf16=w_dn.astype(BF16), b_dn=b_dn,
             final_g=final_g)
    b, s, d = x.shape
    t = b * s
    depth = w_ada.shape[0]
    tiles_per_batch = s // TOK_TILE
    mod = _modulation(c, w_ada, b_ada)
    cst = _constants(s)
    for l in range(depth):
        sh1, sc1, g1, sh2, sc2, g2 = [m[:, None, :] for m in jnp.split(mod[l], 6, axis=-1)]
        ys = _mixers(x, sc1, sh1, _prep_layer(p, l), cst)
        x2, h, route, gate, cnt = _output_stage(
            [y.reshape(t, GROUP_WIDTH) for y in ys], w_o[l].astype(BF16), x.reshape(t, d), g1, sc2, sh2,
            norm2_g[l].reshape(1, d).astype(F32), w_router[l].T.astype(BF16),
            b_router[l].reshape(N_EXPERTS, 1).astype(F32), cst["tri_lt_tok"], tiles_per_batch)
        x = _moe(p, l, x2, h, route, gate, cnt[:, 0], g2, tiles_per_batch, l == depth - 1).reshape(b, s, d)
    return x
```

```python
import functools
import math

import jax
import jax.numpy as jnp
from jax import lax
from jax.experimental import pallas as pl
from jax.experimental.pallas import tpu as pltpu

F32 = jnp.float32
BF16 = jnp.bfloat16

HEAD_DIM = 64
HEADS = 4
GROUP_WIDTH = HEADS * HEAD_DIM
CHUNK = 64
ROPE_THETA = 10000.0
NORM_EPS = 1e-6
NEG_INF = -1e30
KV_RANK = 128
IDX_HEADS = 4
IDX_DIM = 32
TOPK_MAX = 256
LEFT_CHUNKS = 8
MAX_REL = 128
N_EXPERTS = 32
TOP_K = 4
SWIGLU_ALPHA = 1.702
SWIGLU_LIMIT = 7.0
LOG2E = math.log2(math.e)
INT_MIN = -2 ** 31

LANES = 128
SUBLANES = 8
VMEM_LIMIT = 48 * 1024 * 1024

ATT_TILE = 256
TOK_TILE = 512
EXPERT_TILE = 512
ROUTE_ROWS = 2 * TOP_K
DMA_UNROLL = 8

COL_A = 0
COL_C = 3 * GROUP_WIDTH
COL_D = 6 * GROUP_WIDTH
COL_QB = 9 * GROUP_WIDTH
COL_LAT = COL_QB + GROUP_WIDTH
COL_QI = COL_LAT + KV_RANK
COL_MISC = COL_QI + IDX_HEADS * IDX_DIM
N_COLS = COL_MISC + LANES
MISC_WI = IDX_DIM
MISC_FA = IDX_DIM + IDX_HEADS
ROW_VA = 0
ROW_VC = GROUP_WIDTH
ROW_WI = 2 * GROUP_WIDTH
N_ROWS_T = ROW_WI + SUBLANES


def _cparams(sem):
    return pltpu.CompilerParams(dimension_semantics=sem, vmem_limit_bytes=VMEM_LIMIT)


def _split3(x):
    hi = x.astype(BF16)
    r = x - hi.astype(F32)
    mid = r.astype(BF16)
    lo = (r - mid.astype(F32)).astype(BF16)
    return hi, mid, lo


def _dot(a, b):
    return jnp.dot(a, b, preferred_element_type=F32)


def _dot_nt(a, b):
    return lax.dot_general(a, b, (((1,), (1,)), ((), ())), preferred_element_type=F32)


def _mod_body(c_ref, w_ref, b_ref, o_ref):
    c = c_ref[...]
    act = (c * jax.nn.sigmoid(c)).astype(BF16)
    o_ref[0] = _dot(act, w_ref[0].astype(BF16)) + b_ref[0]


def _modulation(c, w_ada, b_ada):
    depth, d, n = w_ada.shape
    b = c.shape[0]
    tn = 1024
    return pl.pallas_call(
        _mod_body,
        out_shape=jax.ShapeDtypeStruct((depth, b, n), F32),
        grid=(depth, n // tn),
        in_specs=[
            pl.BlockSpec((b, d), lambda l, j: (0, 0)),
            pl.BlockSpec((1, d, tn), lambda l, j: (l, 0, j)),
            pl.BlockSpec((1, 1, tn), lambda l, j: (l, 0, j)),
        ],
        out_specs=pl.BlockSpec((1, b, tn), lambda l, j: (l, 0, j)),
        compiler_params=_cparams(("arbitrary", "arbitrary")),
        name="adaln_mod",
    )(c, w_ada, b_ada.reshape(depth, 1, n))


def _rope_tables(s):
    pos = jnp.arange(s, dtype=F32)[:, None]
    lane = jnp.arange(LANES)[None, :]

    def tables(dim, active):
        half = dim // 2
        j = lane % dim
        inv = ROPE_THETA ** (-(j % half).astype(F32) / half)
        ang = pos * inv
        cos = jnp.where(active, jnp.cos(ang), 1.0)
        sin = jnp.where(active, jnp.sin(ang), 0.0)
        first = j < half
        return [cos, jnp.where(first, -sin, 0.0), jnp.where(first, 0.0, sin)]

    everywhere = lane >= 0
    t = (tables(HEAD_DIM, everywhere) + tables(HEAD_DIM, lane < HEAD_DIM)
         + tables(IDX_DIM, everywhere) + tables(IDX_DIM, lane < IDX_DIM))
    return jnp.stack(t).astype(F32)


def _rope(x, tab, k, half):
    cos, sin_a, sin_b = tab[3 * k], tab[3 * k + 1], tab[3 * k + 2]
    return x * cos + pltpu.roll(x, LANES - half, 1) * sin_a + pltpu.roll(x, half, 1) * sin_b


def _instage_body(x_ref, sc_ref, sh_ref, g_ref, w_ref, wt_ref, wkv_ref, wuvt_ref, kvg_ref, bf_ref, tri_ref, tab_ref,
                  fq_o, fk_o, fvt_o, dq_o, dkv_o, dvt_o, dqi_o, dkit_o, dwt_o,
                  cq_o, ck_o, cvt_o, eq_o, ek_o, ev_o, carry_ref):
    tm = x_ref.shape[1]
    x = x_ref[0]
    y = x * lax.rsqrt(jnp.mean(x * x, axis=-1, keepdims=True) + NORM_EPS) * g_ref[...]
    h = (y * (1.0 + sc_ref[0]) + sh_ref[0]).astype(BF16)

    def proj(c0, n):
        return _dot(h, w_ref[:, c0:c0 + n])

    def proj_t(r0, n):
        return _dot_nt(wt_ref[r0:r0 + n, :], h)

    def heads_out(val, refs, scale):
        for i, ref in enumerate(refs):
            for hd in range(HEADS):
                c0 = i * GROUP_WIDTH + hd * HEAD_DIM
                piece = val[:, c0:c0 + HEAD_DIM]
                if i == 0:
                    piece = piece * scale
                ref[0, hd] = piece.astype(ref.dtype)

    def heads_out_t(val, ref):
        for hd in range(HEADS):
            ref[0, hd] = val[hd * HEAD_DIM:(hd + 1) * HEAD_DIM, :].astype(ref.dtype)

    scale = HEAD_DIM ** -0.5
    heads_out(proj(COL_C, 2 * GROUP_WIDTH), (cq_o, ck_o), scale)
    heads_out_t(proj_t(ROW_VC, GROUP_WIDTH), cvt_o)
    heads_out(proj(COL_D, 3 * GROUP_WIDTH), (eq_o, ek_o, ev_o), scale)

    misc = _rope(proj(COL_MISC, LANES), tab_ref, 3, IDX_DIM // 2)
    lane = lax.broadcasted_iota(jnp.int32, (tm, LANES), 1)
    ki = jnp.where(lane < IDX_DIM, misc, 0.0)
    kit = ki + pltpu.roll(ki, IDX_DIM, 1) + pltpu.roll(ki, 2 * IDX_DIM, 1) + pltpu.roll(ki, 3 * IDX_DIM, 1)
    dkit_o[0] = kit.astype(BF16)
    dwt_o[0] = proj_t(ROW_WI, SUBLANES) * ((IDX_HEADS ** -0.5) * (IDX_DIM ** -0.5))
    qi = _rope(proj(COL_QI, LANES), tab_ref, 2, IDX_DIM // 2)
    for hd in range(IDX_HEADS):
        in_head = (lane >= hd * IDX_DIM) & (lane < (hd + 1) * IDX_DIM)
        dqi_o[0, hd] = jnp.where(in_head, qi, 0.0).astype(BF16)
    lat = proj(COL_LAT, KV_RANK)
    lat = (lat * lax.rsqrt(jnp.mean(lat * lat, axis=-1, keepdims=True) + NORM_EPS) * kvg_ref[...]).astype(BF16)
    dkv_o[0] = _rope(_dot(lat, wkv_ref[...]), tab_ref, 1, HEAD_DIM // 2).astype(BF16)
    dvt_o[0] = _dot_nt(wuvt_ref[...], lat).astype(BF16)
    qb = proj(COL_QB, GROUP_WIDTH)
    zeros64 = jnp.zeros((tm, HEAD_DIM), F32)
    for half_i in range(2):
        r = _rope(qb[:, half_i * LANES:(half_i + 1) * LANES], tab_ref, 0, HEAD_DIM // 2) * (scale * LOG2E)
        for j in range(2):
            piece = jnp.concatenate([r[:, j * HEAD_DIM:(j + 1) * HEAD_DIM], zeros64], axis=-1)
            dq_o[0, 2 * half_i + j] = piece.astype(BF16)

    @pl.when(pl.program_id(1) == 0)
    def _():
        carry_ref[...] = jnp.zeros_like(carry_ref)

    z = misc + bf_ref[...]
    log_f = jnp.minimum(z, 0.0) - jnp.log1p(jnp.exp(-jnp.abs(z)))
    tri = tri_ref[...]
    f_cum = carry_ref[...] + sum(_dot(tri, p) for p in _split3(log_f))
    carry_ref[...] = f_cum[tm - 1:tm, :]
    f_cum = f_cum * LOG2E
    pa = proj(COL_A, 2 * GROUP_WIDTH)
    heads_out_t(proj_t(ROW_VA, GROUP_WIDTH), fvt_o)
    lane64 = lax.broadcasted_iota(jnp.int32, (tm, HEAD_DIM), 1)
    for hd in range(HEADS):
        fh = f_cum[:, MISC_FA + hd:MISC_FA + hd + 1]
        f3 = [jnp.broadcast_to(p.astype(F32), (tm, HEAD_DIM)) for p in _split3(fh)]
        one = jnp.where(lane64 < 6, 1.0, 0.0)
        q_ext = jnp.where(lane64 == 0, f3[0], jnp.where(lane64 == 1, f3[1], jnp.where(lane64 == 2, f3[2], one)))
        k_ext = jnp.where(lane64 == 3, -f3[0], jnp.where(lane64 == 4, -f3[1], jnp.where(lane64 == 5, -f3[2], one)))
        q = pa[:, hd * HEAD_DIM:(hd + 1) * HEAD_DIM] * (scale * LOG2E)
        k = pa[:, GROUP_WIDTH + hd * HEAD_DIM:GROUP_WIDTH + (hd + 1) * HEAD_DIM]
        fq_o[0, hd] = jnp.concatenate([q, q_ext], axis=-1).astype(BF16)
        fk_o[0, hd] = jnp.concatenate([k, k_ext], axis=-1).astype(BF16)


def _reorder_w_in(w_in):
    gw = GROUP_WIDTH
    a0 = 0
    b0 = 3 * gw + HEADS
    c0 = b0 + gw + KV_RANK + IDX_HEADS * IDX_DIM + IDX_DIM + IDX_HEADS
    d0 = c0 + 3 * gw
    lat0 = b0 + gw
    qi0 = lat0 + KV_RANK
    ki0 = qi0 + IDX_HEADS * IDX_DIM
    wi0 = ki0 + IDX_DIM
    pad = jnp.zeros(w_in.shape[:-1] + (LANES - IDX_DIM - IDX_HEADS - HEADS,), w_in.dtype)
    w = jnp.concatenate([
        w_in[:, a0:a0 + 3 * gw], w_in[:, c0:c0 + 3 * gw], w_in[:, d0:d0 + 3 * gw],
        w_in[:, b0:b0 + gw], w_in[:, lat0:lat0 + KV_RANK], w_in[:, qi0:qi0 + IDX_HEADS * IDX_DIM],
        w_in[:, ki0:ki0 + IDX_DIM], w_in[:, wi0:wi0 + IDX_HEADS], w_in[:, 3 * gw:3 * gw + HEADS], pad,
    ], axis=-1)
    wt = jnp.concatenate([
        w_in[:, a0 + 2 * gw:a0 + 3 * gw], w_in[:, c0 + 2 * gw:c0 + 3 * gw], w_in[:, wi0:wi0 + IDX_HEADS],
        jnp.zeros((w_in.shape[0], SUBLANES - IDX_HEADS), w_in.dtype),
    ], axis=-1).T
    return w.astype(BF16), wt.astype(BF16)


def _input_stage(x, sc, sh, g, w, wt, wkv, wuvt, kvg, bf, tri, tab, tm):
    b, s, d = x.shape
    hq = lambda width: jax.ShapeDtypeStruct((b, HEADS, s, width), BF16)
    hqt = jax.ShapeDtypeStruct((b, HEADS, HEAD_DIM, s), BF16)
    flat = jax.ShapeDtypeStruct((b, s, LANES), BF16)
    out_shape = (
        hq(LANES), hq(LANES), hqt,
        hq(LANES), flat, jax.ShapeDtypeStruct((b, HEAD_DIM, s), BF16),
        hq(LANES), flat, jax.ShapeDtypeStruct((b, SUBLANES, s), F32),
        hq(HEAD_DIM), hq(HEAD_DIM), hqt,
        hq(HEAD_DIM), hq(HEAD_DIM), hq(HEAD_DIM),
    )
    hspec = lambda width: pl.BlockSpec((1, HEADS, tm, width), lambda i, j: (i, 0, j, 0))
    htspec = pl.BlockSpec((1, HEADS, HEAD_DIM, tm), lambda i, j: (i, 0, 0, j))
    fspec = pl.BlockSpec((1, tm, LANES), lambda i, j: (i, j, 0))
    tspec = lambda rows: pl.BlockSpec((1, rows, tm), lambda i, j: (i, 0, j))
    const2 = lambda shape: pl.BlockSpec(shape, lambda i, j: (0, 0))
    out_specs = (
        hspec(LANES), hspec(LANES), htspec,
        hspec(LANES), fspec, tspec(HEAD_DIM),
        hspec(LANES), fspec, tspec(SUBLANES),
        hspec(HEAD_DIM), hspec(HEAD_DIM), htspec,
        hspec(HEAD_DIM), hspec(HEAD_DIM), hspec(HEAD_DIM),
    )
    return pl.pallas_call(
        _instage_body,
        out_shape=out_shape,
        grid=(b, s // tm),
        in_specs=[
            pl.BlockSpec((1, tm, d), lambda i, j: (i, j, 0)),
            pl.BlockSpec((1, 1, d), lambda i, j: (i, 0, 0)),
            pl.BlockSpec((1, 1, d), lambda i, j: (i, 0, 0)),
            const2((1, d)),
            const2((d, N_COLS)),
            const2((N_ROWS_T, d)),
            const2((KV_RANK, LANES)),
            const2((HEAD_DIM, KV_RANK)),
            const2((1, KV_RANK)),
            const2((1, LANES)),
            const2((tm, tm)),
            pl.BlockSpec((12, tm, LANES), lambda i, j: (0, j, 0)),
        ],
        out_specs=out_specs,
        scratch_shapes=[pltpu.VMEM((1, LANES), F32)],
        compiler_params=_cparams(("arbitrary", "arbitrary")),
        name="mixer_input_stage",
    )(x, sc, sh, g, w, wt, wkv, wuvt, kvg, bf, tri, tab)


def _key_query_iota(t):
    return lax.broadcasted_iota(jnp.int32, (t, t), 0), lax.broadcasted_iota(jnp.int32, (t, t), 1)


def _softmax2_step(s, vt, m, l, acc):
    m_new = jnp.maximum(m, jnp.max(s, axis=0, keepdims=True))
    alpha = jnp.exp2(m - m_new)
    p = jnp.exp2(s - m_new)
    l_new = alpha * l + jnp.sum(p, axis=0, keepdims=True)
    acc_new = alpha * acc + _dot(vt, p.astype(BF16))
    return m_new, l_new, acc_new


def _skewed(n, stages):
    vals = [None] * n
    for step in range(n + len(stages) - 1):
        for j, stage in enumerate(stages):
            i = step - j
            if 0 <= i < n:
                vals[i] = stage(i, vals[i])


def _heads_to_rows(accs):
    return jnp.concatenate([a.T for a in accs], axis=-1)


def _fox_body(q_ref, k_ref, vt_ref, o_ref, m_ref, l_ref, acc_ref):
    qi = pl.program_id(1)
    tq = q_ref.shape[2]
    key, qry = _key_query_iota(tq)
    m_ref[...] = jnp.full_like(m_ref, NEG_INF)
    l_ref[...] = jnp.zeros_like(l_ref)
    acc_ref[...] = jnp.zeros_like(acc_ref)

    def blocks(kbs):
        items = [(pl.multiple_of(kb * tq, tq), masked, hd) for kb, masked in kbs for hd in range(HEADS)]

        def scores(i, _):
            k0, masked, hd = items[i]
            s = _dot_nt(k_ref[0, hd, pl.ds(k0, tq), :], q_ref[0, hd])
            return jnp.where(key <= qry, s, NEG_INF) if masked else s

        def update(i, s):
            k0, _, hd = items[i]
            m_ref[hd], l_ref[hd], acc_ref[hd] = _softmax2_step(
                s, vt_ref[0, hd, :, pl.ds(k0, tq)], m_ref[hd], l_ref[hd], acc_ref[hd])

        _skewed(len(items), [scores, update])

    def pair_body(j, carry):
        blocks([(2 * j, False), (2 * j + 1, False)])
        return carry

    lax.fori_loop(0, lax.shift_right_logical(qi, 1), pair_body, 0)

    @pl.when((qi & 1) == 1)
    def _():
        blocks([(qi - 1, False), (qi, True)])

    @pl.when((qi & 1) == 0)
    def _():
        blocks([(qi, True)])

    o_ref[0] = _heads_to_rows([acc_ref[hd] / l_ref[hd] for hd in range(HEADS)]).astype(o_ref.dtype)


def _fox_attention(q, k, vt, tq):
    b, _, s, _ = q.shape
    return pl.pallas_call(
        _fox_body,
        out_shape=jax.ShapeDtypeStruct((b, s, GROUP_WIDTH), BF16),
        grid=(b, s // tq),
        in_specs=[
            pl.BlockSpec((1, HEADS, tq, LANES), lambda i, j: (i, 0, j, 0)),
            pl.BlockSpec((1, HEADS, s, LANES), lambda i, j: (i, 0, 0, 0)),
            pl.BlockSpec((1, HEADS, HEAD_DIM, s), lambda i, j: (i, 0, 0, 0)),
        ],
        out_specs=pl.BlockSpec((1, tq, GROUP_WIDTH), lambda i, j: (i, j, 0)),
        scratch_shapes=[pltpu.VMEM((HEADS, 1, tq), F32), pltpu.VMEM((HEADS, 1, tq), F32),
                        pltpu.VMEM((HEADS, HEAD_DIM, tq), F32)],
        compiler_params=_cparams(("arbitrary", "arbitrary")),
        name="forgetting_attention",
    )(q, k, vt)


def _sb_body(q_ref, k_ref, vt_ref, tri_ref, o_ref, r_ref, acc_ref):
    qi = pl.program_id(1)
    tq = q_ref.shape[2]
    key, qry = _key_query_iota(tq)
    tri = tri_ref[...]
    r_ref[...] = jnp.zeros_like(r_ref)
    acc_ref[...] = jnp.zeros_like(acc_ref)

    def blocks(kbs):
        items = [(pl.multiple_of(kb * tq, tq), masked, hd) for kb, masked in kbs for hd in range(HEADS)]

        def logits(i, _):
            k0, _, hd = items[i]
            return _dot_nt(k_ref[0, hd, pl.ds(k0, tq), :], q_ref[0, hd])

        def later_sums(i, z):
            _, masked, hd = items[i]
            log_1m = -(jnp.maximum(z, 0.0) + jnp.log(1.0 + jnp.exp2(jnp.abs(z) * (-LOG2E))))
            if masked:
                log_1m = jnp.where(key < qry, log_1m, 0.0)
            after = r_ref[hd] + _dot(tri, log_1m.astype(BF16))
            return z, log_1m, after

        def accumulate(i, val):
            k0, masked, hd = items[i]
            z, log_1m, after = val
            w = jnp.exp(log_1m + z + after)
            if masked:
                w = jnp.where(key < qry, w, 0.0)
            acc_ref[hd] += _dot(vt_ref[0, hd, :, pl.ds(k0, tq)], w.astype(BF16))
            r_ref[hd] += jnp.sum(log_1m, axis=0, keepdims=True)

        _skewed(len(items), [logits, later_sums, accumulate])

    @pl.when((qi & 1) == 1)
    def _():
        blocks([(qi, True), (qi - 1, False)])

    @pl.when((qi & 1) == 0)
    def _():
        blocks([(qi, True)])

    top = qi - 1 - (qi & 1)

    def pair_body(j, carry):
        blocks([(top - 2 * j, False), (top - 2 * j - 1, False)])
        return carry

    lax.fori_loop(0, lax.shift_right_logical(qi, 1), pair_body, 0)
    o_ref[0] = _heads_to_rows([acc_ref[hd] for hd in range(HEADS)]).astype(o_ref.dtype)


def _sb_attention(q, k, vt, tri, tq):
    b, _, s, _ = q.shape
    return pl.pallas_call(
        _sb_body,
        out_shape=jax.ShapeDtypeStruct((b, s, GROUP_WIDTH), BF16),
        grid=(b, s // tq),
        in_specs=[
            pl.BlockSpec((1, HEADS, tq, HEAD_DIM), lambda i, j: (i, 0, j, 0)),
            pl.BlockSpec((1, HEADS, s, HEAD_DIM), lambda i, j: (i, 0, 0, 0)),
            pl.BlockSpec((1, HEADS, HEAD_DIM, s), lambda i, j: (i, 0, 0, 0)),
            pl.BlockSpec((tq, tq), lambda i, j: (0, 0)),
        ],
        out_specs=pl.BlockSpec((1, tq, GROUP_WIDTH), lambda i, j: (i, j, 0)),
        scratch_shapes=[pltpu.VMEM((HEADS, 1, tq), F32), pltpu.VMEM((HEADS, HEAD_DIM, tq), F32)],
        compiler_params=_cparams(("arbitrary", "arbitrary")),
        name="stick_breaking_attention",
    )(q, k, vt, tri)


def _chunk_bias_table(rel_table, tq):
    left = LEFT_CHUNKS * CHUNK
    width = left + tq
    d = jnp.arange(width + tq - 1) - (tq - 1) - left
    diag = rel_table[:, jnp.clip(d, -MAX_REL, MAX_REL) + MAX_REL].astype(F32)
    bias = jnp.stack([diag[:, tq - 1 - t:tq - 1 - t + width] for t in range(tq)], axis=1)
    t = jnp.arange(tq)[:, None]
    j = jnp.arange(width)[None, :]
    c0 = (t // CHUNK) * CHUNK
    visible = (j >= c0) & (j - left < c0 + CHUNK)
    return jnp.where(visible[None], bias, NEG_INF)


def _chunk_body(nb, q_ref, *refs):
    k_refs, v_refs = refs[:nb], refs[nb:2 * nb]
    bias_ref, o_ref = refs[2 * nb], refs[2 * nb + 1]
    qi = pl.program_id(1)
    outs = []
    for hd in range(HEADS):
        q = q_ref[0, hd]
        s_parts = []
        for n in range(nb):
            s = _dot_nt(q, k_refs[n][0, hd])
            s_parts.append(jnp.where(qi - (nb - 1) + n >= 0, s, NEG_INF))
        s = jnp.concatenate(s_parts, axis=-1) + bias_ref[hd]
        m = jnp.max(s, axis=-1, keepdims=True)
        p = jnp.exp(s - m)
        l = jnp.sum(p, axis=-1, keepdims=True)
        p = p.astype(BF16)
        tq = q.shape[0]
        acc = sum(_dot(p[:, n * tq:(n + 1) * tq], v_refs[n][0, hd]) for n in range(nb))
        outs.append(acc / l)
    o_ref[0] = jnp.concatenate(outs, axis=-1).astype(o_ref.dtype)


def _chunk_attention(q, k, v, bias, tq):
    b, _, s, _ = q.shape
    left = LEFT_CHUNKS * CHUNK
    assert left % tq == 0
    nb = left // tq + 1
    kv_specs = [
        pl.BlockSpec((1, HEADS, tq, HEAD_DIM), functools.partial(
            lambda i, j, n: (i, 0, jnp.maximum(j - (nb - 1) + n, 0), 0), n=n))
        for n in range(nb)
    ]
    return pl.pallas_call(
        functools.partial(_chunk_body, nb),
        out_shape=jax.ShapeDtypeStruct((b, s, GROUP_WIDTH), BF16),
        grid=(b, s // tq),
        in_specs=[pl.BlockSpec((1, HEADS, tq, HEAD_DIM), lambda i, j: (i, 0, j, 0))] + kv_specs + kv_specs
        + [pl.BlockSpec((HEADS, tq, nb * tq), lambda i, j: (0, 0, 0))],
        out_specs=pl.BlockSpec((1, tq, GROUP_WIDTH), lambda i, j: (i, j, 0)),
        compiler_params=_cparams(("arbitrary", "arbitrary")),
        name="chunked_relbias_attention",
    )(q, *([k] * nb), *([v] * nb), bias)


def _dsa_body(n_sel, q_ref, kv_ref, vt_ref, qi_ref, kit_ref, wt_ref, tri_ref, o_ref,
              key_ref, bias_ref, m_ref, l_ref, acc_ref):
    i = pl.program_id(1)
    tq = q_ref.shape[2]
    nkb = i + 1
    key_pos, qry_pos = _key_query_iota(tq)
    vis_diag = key_pos < (qry_pos // CHUNK + 1) * CHUNK
    wt = wt_ref[0]

    def score_block(kb, masked):
        k0 = pl.multiple_of(kb * tq, tq)
        kit = kit_ref[0, pl.ds(k0, tq), :]
        dots = [_dot_nt(kit, qi_ref[0, hd]) for hd in range(IDX_HEADS)]
        score = sum(wt[hd:hd + 1, :] * jnp.maximum(dots[hd], 0.0) for hd in range(IDX_HEADS))
        score = jnp.where(score == 0.0, 0.0, score)
        bits = pltpu.bitcast(score, jnp.int32)
        okey = jnp.where(bits < 0, bits ^ 0x7FFFFFFF, bits)
        if masked:
            okey = jnp.where(vis_diag, okey, INT_MIN)
        key_ref[kb] = okey

    def score_loop(kb, carry):
        score_block(kb, False)
        return carry

    lax.fori_loop(0, i, score_loop, 0)
    score_block(i, True)

    def count(pred):
        def hits(kb):
            ind = jnp.where(pred(key_ref[kb]), 1.0, 0.0)
            return ind.reshape(tq // SUBLANES, SUBLANES, tq).sum(axis=0)

        def pair(j, acc):
            return acc + hits(2 * j) + hits(2 * j + 1)

        acc = lax.fori_loop(0, lax.shift_right_logical(nkb, 1), pair, jnp.zeros((SUBLANES, tq), F32))
        acc = acc + lax.cond((nkb & 1) == 1, lambda: hits(nkb - 1), lambda: jnp.zeros((SUBLANES, tq), F32))
        return jnp.sum(acc, axis=0, keepdims=True)

    def bit_step(it, ans_u):
        cand_u = ans_u | lax.shift_left(jnp.int32(1), 31 - it)
        cand = cand_u ^ INT_MIN
        return jnp.where(count(lambda k: k >= cand) >= n_sel, cand_u, ans_u)

    thr = lax.fori_loop(0, 32, bit_step, jnp.zeros((1, tq), jnp.int32)) ^ INT_MIN

    n_gt = count(lambda k: k > thr)
    need = jnp.where(thr != INT_MIN, n_sel - n_gt, 0.0)
    tri = tri_ref[...]

    def select_block(kb, seen):
        okey = key_ref[kb]
        eq = okey == thr
        eq_f = jnp.where(eq, 1.0, 0.0)
        earlier = seen + _dot(tri, eq_f.astype(BF16))
        tie_bias = jnp.where(eq, jnp.where(earlier < need, 0.0, NEG_INF), NEG_INF)
        bias_ref[kb] = jnp.where(okey > thr, 0.0, tie_bias)
        return seen + jnp.sum(eq_f, axis=0, keepdims=True)

    lax.fori_loop(0, nkb, select_block, jnp.zeros((1, tq), F32))

    m_ref[...] = jnp.full_like(m_ref, NEG_INF)
    l_ref[...] = jnp.zeros_like(l_ref)
    acc_ref[...] = jnp.zeros_like(acc_ref)

    def attend(kbs):
        items = [(kb, pl.multiple_of(kb * tq, tq), hd) for kb in kbs for hd in range(HEADS)]

        def scores(n, _):
            kb, k0, hd = items[n]
            return _dot_nt(kv_ref[0, pl.ds(k0, tq), :], q_ref[0, hd]) + bias_ref[kb]

        def update(n, s):
            _, k0, hd = items[n]
            m_ref[hd], l_ref[hd], acc_ref[hd] = _softmax2_step(
                s, vt_ref[0, :, pl.ds(k0, tq)], m_ref[hd], l_ref[hd], acc_ref[hd])

        _skewed(len(items), [scores, update])

    def attend_pair(j, carry):
        attend([2 * j, 2 * j + 1])
        return carry

    lax.fori_loop(0, lax.shift_right_logical(nkb, 1), attend_pair, 0)

    @pl.when((nkb & 1) == 1)
    def _():
        attend([nkb - 1])
    o_ref[0] = _heads_to_rows([acc_ref[hd] / l_ref[hd] for hd in range(HEADS)]).astype(o_ref.dtype)


def _dsa_attention(q, kv, vt, qi, kit, wt, tri, tq):
    b, _, s, _ = q.shape
    n_sel = min(TOPK_MAX, s // 4)
    return pl.pallas_call(
        functools.partial(_dsa_body, n_sel),
        out_shape=jax.ShapeDtypeStruct((b, s, GROUP_WIDTH), BF16),
        grid=(b, s // tq),
        in_specs=[
            pl.BlockSpec((1, HEADS, tq, LANES), lambda i, j: (i, 0, j, 0)),
            pl.BlockSpec((1, s, LANES), lambda i, j: (i, 0, 0)),
            pl.BlockSpec((1, HEAD_DIM, s), lambda i, j: (i, 0, 0)),
            pl.BlockSpec((1, IDX_HEADS, tq, LANES), lambda i, j: (i, 0, j, 0)),
            pl.BlockSpec((1, s, LANES), lambda i, j: (i, 0, 0)),
            pl.BlockSpec((1, SUBLANES, tq), lambda i, j: (i, 0, j)),
            pl.BlockSpec((tq, tq), lambda i, j: (0, 0)),
        ],
        out_specs=pl.BlockSpec((1, tq, GROUP_WIDTH), lambda i, j: (i, j, 0)),
        scratch_shapes=[
            pltpu.VMEM((s // tq, tq, tq), jnp.int32), pltpu.VMEM((s // tq, tq, tq), F32),
            pltpu.VMEM((HEADS, 1, tq), F32), pltpu.VMEM((HEADS, 1, tq), F32),
            pltpu.VMEM((HEADS, HEAD_DIM, tq), F32),
        ],
        compiler_params=_cparams(("arbitrary", "arbitrary")),
        name="sparse_indexer_attention",
    )(q, kv, vt, qi, kit, wt, tri)


ROW_CHUNKS = 8


def _store_rows(ref, val, r0=0):
    n = val.shape[0]
    for c in range(ROW_CHUNKS):
        ref[pl.ds(r0 * ROW_CHUNKS + c, n, stride=ROW_CHUNKS), :] = val[:, c * LANES:(c + 1) * LANES]


def _load_rows(ref, r0, n):
    return jnp.concatenate(
        [ref[pl.ds(r0 * ROW_CHUNKS + c, n, stride=ROW_CHUNKS), :] for c in range(ROW_CHUNKS)], axis=-1)


def _outstage_body(ya_ref, yb_ref, yc_ref, yd_ref, wo_ref, x_ref, g1_ref, sc_ref, sh_ref, ng_ref,
                   wr_ref, br_ref, tri_ref, xo_ref, h_ref, route_ref, gate_ref, cnt_ref, carry_ref):
    tm = x_ref.shape[0]
    gw = GROUP_WIDTH
    mix = sum(_dot(r[...], wo_ref[n * gw:(n + 1) * gw, :]) for n, r in enumerate((ya_ref, yb_ref, yc_ref, yd_ref)))
    x = x_ref[...] + g1_ref[0] * mix
    xo_ref[...] = x
    y = x * lax.rsqrt(jnp.mean(x * x, axis=-1, keepdims=True) + NORM_EPS) * ng_ref[...]
    h = y * (1.0 + sc_ref[0]) + sh_ref[0]
    _store_rows(h_ref, h)

    @pl.when(pl.program_id(0) == 0)
    def _():
        carry_ref[...] = jnp.zeros_like(carry_ref)

    logits = _dot_nt(wr_ref[...], h.astype(BF16)) + br_ref[...]
    e_iota = lax.broadcasted_iota(jnp.int32, (N_EXPERTS, tm), 0).astype(F32)
    vals, ids = [], []
    for _ in range(TOP_K):
        top = jnp.max(logits, axis=0, keepdims=True)
        idx = jnp.min(jnp.where(logits == top, e_iota, float(N_EXPERTS)), axis=0, keepdims=True)
        logits = jnp.where(e_iota == idx, -jnp.inf, logits)
        vals.append(top)
        ids.append(idx)
    exps = [jnp.exp(v - vals[0]) for v in vals]
    denom = sum(exps)
    onehot = sum(jnp.where(e_iota == idx, 1.0, 0.0) for idx in ids)
    before = carry_ref[...] + _dot(onehot.astype(BF16), tri_ref[...])
    ranks = [jnp.sum(jnp.where(e_iota == idx, before, 0.0), axis=0, keepdims=True) for idx in ids]
    carry_ref[...] += jnp.sum(onehot, axis=1, keepdims=True)
    route_ref[0] = jnp.concatenate(ids + ranks, axis=0).astype(jnp.int32)
    gate_ref[0] = jnp.concatenate([e / denom for e in exps] + [jnp.zeros_like(denom)] * TOP_K, axis=0)
    cnt_ref[...] = jnp.broadcast_to(carry_ref[...], cnt_ref.shape)


def _output_stage(ys, wo, x, g1, sc2, sh2, ng, wr_t, br, tri, tiles_per_batch):
    t, d = x.shape
    tm = TOK_TILE
    nt = t // tm
    row = lambda width: pl.BlockSpec((tm, width), lambda i: (i, 0))
    const = lambda shape: pl.BlockSpec(shape, lambda i: (0,) * len(shape))
    per_batch = pl.BlockSpec((1, 1, d), lambda i: (i // tiles_per_batch, 0, 0))
    return pl.pallas_call(
        _outstage_body,
        out_shape=(
            jax.ShapeDtypeStruct((t, d), F32), jax.ShapeDtypeStruct((t * ROW_CHUNKS, LANES), F32),
            jax.ShapeDtypeStruct((nt, ROUTE_ROWS, tm), jnp.int32), jax.ShapeDtypeStruct((nt, ROUTE_ROWS, tm), F32),
            jax.ShapeDtypeStruct((N_EXPERTS, LANES), F32),
        ),
        grid=(nt,),
        in_specs=[row(GROUP_WIDTH)] * 4 + [const((d, d)), row(d), per_batch, per_batch, per_batch, const((1, d)),
                                           const((N_EXPERTS, d)), const((N_EXPERTS, 1)), const((tm, tm))],
        out_specs=(row(d), pl.BlockSpec((tm * ROW_CHUNKS, LANES), lambda i: (i, 0)),
                   pl.BlockSpec((1, ROUTE_ROWS, tm), lambda i: (i, 0, 0)),
                   pl.BlockSpec((1, ROUTE_ROWS, tm), lambda i: (i, 0, 0)),
                   const((N_EXPERTS, LANES))),
        scratch_shapes=[pltpu.VMEM((N_EXPERTS, 1), F32)],
        compiler_params=_cparams(("arbitrary",)),
        name="mixer_output_stage_router",
    )(*ys, wo, x, g1, sc2, sh2, ng, wr_t, br, tri)


def _route_fetch(route_hbm, route_smem, rsem):
    i = pl.program_id(0)
    n = route_hbm.shape[1]
    slot = lax.rem(i, 2)

    def fetch(step, sl):
        dst = route_smem.at[pl.ds(pl.multiple_of(sl * n, n), n)]
        return pltpu.make_async_copy(route_hbm.at[step], dst, rsem.at[sl])

    @pl.when(i == 0)
    def _():
        fetch(0, 0).start()

    fetch(i, slot).wait()

    @pl.when(i + 1 < pl.num_programs(0))
    def _():
        fetch(i + 1, 1 - slot).start()

    return slot * n


H_BUFFERS = 3


def _dispatch_body(pstart_ref, pend_ref, route_hbm, h_hbm, xs_hbm, route_smem, hbuf, zero_ref, rsem, hsem, zsem, sem):
    tm = hbuf.shape[1]
    i = pl.program_id(0)
    n = pl.num_programs(0)

    def h_load(step, b):
        return pltpu.make_async_copy(h_hbm.at[pl.ds(pl.multiple_of(step * tm, tm), tm)], hbuf.at[b], hsem.at[b])

    def wait_rows(parity):
        for _ in range(TOP_K):
            pltpu.make_async_copy(hbuf.at[0], xs_hbm.at[pl.ds(0, tm)], sem.at[parity]).wait()

    @pl.when(i == 0)
    def _():
        h_load(0, 0).start()

        @pl.when(n > 1)
        def _():
            h_load(1, 1).start()

        zero_ref[...] = jnp.zeros_like(zero_ref)

        def fill(e):
            start = pl.multiple_of(pend_ref[e] - EXPERT_TILE, EXPERT_TILE)
            return pltpu.make_async_copy(zero_ref, xs_hbm.at[pl.ds(start, EXPERT_TILE)], zsem)

        def start(e, c):
            @pl.when(pend_ref[e] > pstart_ref[e])
            def _():
                fill(e).start()
            return c

        def wait(e, c):
            @pl.when(pend_ref[e] > pstart_ref[e])
            def _():
                fill(e).wait()
            return c

        lax.fori_loop(0, N_EXPERTS, start, 0)
        lax.fori_loop(0, N_EXPERTS, wait, 0)

    rec = _route_fetch(route_hbm, route_smem, rsem)
    b = lax.rem(i, H_BUFFERS)
    parity = lax.rem(i, 2)
    h_load(i, b).wait()
    h_ref = hbuf.at[b]
    for k in range(TOP_K):
        def start(j, c, base=rec + k * tm):
            for prio in range(2):
                t = 2 * j + prio
                pltpu.make_async_copy(h_ref.at[t], xs_hbm.at[route_smem[base + t]], sem.at[parity]).start(priority=prio)
            return c

        lax.fori_loop(0, tm // 2, start, 0, unroll=DMA_UNROLL // 2)

    @pl.when(i > 0)
    def _():
        wait_rows(1 - parity)

    @pl.when(i + 2 < n)
    def _():
        h_load(i + 2, lax.rem(i + 2, H_BUFFERS)).start()

    @pl.when(i == n - 1)
    def _():
        wait_rows(parity)


def _dispatch(pstart, pend, route, h, n_rows):
    t, dc, _ = h.shape
    tm = TOK_TILE
    return pl.pallas_call(
        _dispatch_body,
        out_shape=jax.ShapeDtypeStruct((n_rows, dc, LANES), F32),
        grid_spec=pltpu.PrefetchScalarGridSpec(
            num_scalar_prefetch=2,
            grid=(t // tm,),
            in_specs=[pl.BlockSpec(memory_space=pl.ANY), pl.BlockSpec(memory_space=pl.ANY)],
            out_specs=pl.BlockSpec(memory_space=pl.ANY),
            scratch_shapes=[
                pltpu.SMEM((2 * TOP_K * tm,), jnp.int32), pltpu.VMEM((H_BUFFERS, tm, dc, LANES), F32),
                pltpu.VMEM((EXPERT_TILE, dc, LANES), F32),
                pltpu.SemaphoreType.DMA((2,)), pltpu.SemaphoreType.DMA((H_BUFFERS,)), pltpu.SemaphoreType.DMA,
                pltpu.SemaphoreType.DMA((2,)),
            ],
        ),
        compiler_params=_cparams(("arbitrary",)),
        name="moe_dispatch",
    )(pstart, pend, route, h)


def _expert_body(tile_e_ref, n_used_ref, x_ref, wgu_ref, bgu_ref, wdn_ref, bdn_ref, y_ref):
    @pl.when(pl.program_id(0) < n_used_ref[0])
    def _():
        f = wdn_ref.shape[2]
        gu = _dot(_load_rows(x_ref, 0, EXPERT_TILE).astype(BF16), wgu_ref[0, 0]) + bgu_ref[0, 0]
        glu = jnp.minimum(gu[:, :f], SWIGLU_LIMIT)
        lin = jnp.clip(gu[:, f:], -SWIGLU_LIMIT, SWIGLU_LIMIT)
        act = glu * jax.nn.sigmoid(SWIGLU_ALPHA * glu) * (lin + 1.0)
        _store_rows(y_ref, _dot(act.astype(BF16), wdn_ref[0, 0]) + bdn_ref[0, 0])


def _experts(l, tile_e, n_used, xs, wgu, bgu, wdn, bdn):
    p, dc, _ = xs.shape
    assert dc == ROW_CHUNKS
    d = dc * LANES
    f = wdn.shape[2]
    tm = EXPERT_TILE
    used = lambda i, te, nu: jnp.minimum(i, nu[0] - 1)
    expert = lambda i, te, nu: (l, te[i], 0, 0)
    return pl.pallas_call(
        _expert_body,
        out_shape=jax.ShapeDtypeStruct((p * dc, LANES), F32),
        grid_spec=pltpu.PrefetchScalarGridSpec(
            num_scalar_prefetch=2,
            grid=(p // tm,),
            in_specs=[
                pl.BlockSpec((tm * dc, LANES), lambda i, te, nu: (used(i, te, nu), 0)),
                pl.BlockSpec((1, 1, d, 2 * f), expert),
                pl.BlockSpec((1, 1, 1, 2 * f), expert),
                pl.BlockSpec((1, 1, f, d), expert),
                pl.BlockSpec((1, 1, 1, d), expert),
            ],
            out_specs=pl.BlockSpec((tm * dc, LANES), lambda i, te, nu: (used(i, te, nu), 0)),
        ),
        compiler_params=_cparams(("arbitrary",)),
        name="moe_experts",
    )(tile_e, n_used, xs.reshape(p * dc, LANES), wgu, bgu, wdn, bdn).reshape(p, dc, LANES)


COMBINE_ROWS = 32
ROUTE_SLOTS = 3


def _combine_body(final, route_hbm, ys_hbm, x_ref, g2_ref, gate_ref, fg_ref, o_ref,
                  route_smem, buf_ref, rsem, sem):
    tm = x_ref.shape[0]
    i = pl.program_id(0)
    n = pl.num_programs(0)
    n_rec = TOP_K * tm

    def rec_fetch(step):
        sl = lax.rem(step, ROUTE_SLOTS)
        dst = route_smem.at[pl.ds(pl.multiple_of(sl * n_rec, n_rec), n_rec)]
        return pltpu.make_async_copy(route_hbm.at[step], dst, rsem.at[sl])

    def issue_gathers(step):
        rec = lax.rem(step, ROUTE_SLOTS) * n_rec
        half = lax.rem(step, 2)
        row0 = half * n_rec

        def start(j, c):
            for prio in range(2):
                r = 2 * j + prio
                dst = buf_ref.at[pl.ds(pl.multiple_of((row0 + r) * ROW_CHUNKS, ROW_CHUNKS), ROW_CHUNKS)]
                pltpu.make_async_copy(ys_hbm.at[route_smem[rec + r]], dst, sem.at[half]).start(priority=prio)
            return c

        lax.fori_loop(0, n_rec // 2, start, 0, unroll=DMA_UNROLL // 2)

    @pl.when(i == 0)
    def _():
        rec_fetch(0).start()

        @pl.when(n > 1)
        def _():
            rec_fetch(1).start()

        rec_fetch(0).wait()
        issue_gathers(0)

    @pl.when(i + 1 < n)
    def _():
        rec_fetch(i + 1).wait()
        issue_gathers(i + 1)

    @pl.when(i + 2 < n)
    def _():
        rec_fetch(i + 2).start()

    half = lax.rem(i, 2)
    half_view = buf_ref.at[pl.ds(0, n_rec * ROW_CHUNKS)]
    pltpu.make_async_copy(half_view, half_view, sem.at[half]).wait()
    g2 = g2_ref[0]
    fg = fg_ref[...]

    def rows(chunk, c):
        r0 = pl.multiple_of(chunk * COMBINE_ROWS, COMBINE_ROWS)
        gates = gate_ref[pl.ds(r0, COMBINE_ROWS), :]
        moe = sum(gates[:, k:k + 1] * _load_rows(buf_ref, half * n_rec + k * tm + r0, COMBINE_ROWS)
                  for k in range(TOP_K))
        x = x_ref[pl.ds(r0, COMBINE_ROWS), :] + g2 * moe
        if final:
            x = x * lax.rsqrt(jnp.mean(x * x, axis=-1, keepdims=True) + NORM_EPS) * fg
        o_ref[pl.ds(r0, COMBINE_ROWS), :] = x
        return c

    lax.fori_loop(0, tm // COMBINE_ROWS, rows, 0)


def _combine(route, ys, x, g2, gates_col, final_g, tiles_per_batch, final):
    t, d = x.shape
    tm = TOK_TILE
    return pl.pallas_call(
        functools.partial(_combine_body, final),
        out_shape=jax.ShapeDtypeStruct((t, d), F32),
        grid=(t // tm,),
        in_specs=[
            pl.BlockSpec(memory_space=pl.ANY), pl.BlockSpec(memory_space=pl.ANY),
            pl.BlockSpec((tm, d), lambda i: (i, 0)),
            pl.BlockSpec((1, 1, d), lambda i: (i // tiles_per_batch, 0, 0)),
            pl.BlockSpec((tm, ROUTE_ROWS), lambda i: (i, 0)),
            pl.BlockSpec((1, d), lambda i: (0, 0)),
        ],
        out_specs=pl.BlockSpec((tm, d), lambda i: (i, 0)),
        scratch_shapes=[
            pltpu.SMEM((ROUTE_SLOTS * TOP_K * tm,), jnp.int32), pltpu.VMEM((2 * TOP_K * tm * ROW_CHUNKS, LANES), F32),
            pltpu.SemaphoreType.DMA((ROUTE_SLOTS,)), pltpu.SemaphoreType.DMA((2,)),
        ],
        compiler_params=_cparams(("arbitrary",)),
        name="moe_combine",
    )(route, ys, x, g2, gates_col, final_g)


def _moe_plan(counts, n_tiles):
    counts = counts.astype(jnp.int32)
    tiles = (counts + EXPERT_TILE - 1) // EXPERT_TILE
    tile_end = jnp.cumsum(tiles)
    pend = tile_end * EXPERT_TILE
    pstart = pend - tiles * EXPERT_TILE
    n_used = tile_end[-1:]
    tile = jnp.minimum(jnp.arange(n_tiles), n_used - 1)
    tile_e = jnp.sum((tile_end[None, :] <= tile[:, None]).astype(jnp.int32), axis=1)
    return pstart, pend, tile_e, n_used


def _tri(n, rel):
    r = jnp.arange(n)[:, None]
    c = jnp.arange(n)[None, :]
    return rel(r, c).astype(BF16)


def _constants(s):
    return dict(
        tab=_rope_tables(s),
        tri_le=_tri(ATT_TILE, lambda r, c: c <= r),
        tri_gt=_tri(ATT_TILE, lambda r, c: r > c),
        tri_lt=_tri(ATT_TILE, lambda r, c: r < c),
        tri_lt_tok=_tri(TOK_TILE, lambda r, c: r < c),
    )


def _prep_layer(p, l):
    bf = jnp.zeros((1, LANES), F32).at[0, MISC_FA:MISC_FA + HEADS].set(p["b_forget"][l].astype(F32))
    w, wt = _reorder_w_in(p["w_in"][l])
    return dict(
        w_in=w, w_in_t=wt,
        wkv=jnp.concatenate([p["w_uk"][l], p["w_uv"][l]], axis=-1).astype(BF16),
        wuvt=p["w_uv"][l].T.astype(BF16),
        kvg=p["kv_norm_g"][l].reshape(1, KV_RANK).astype(F32),
        bf=bf,
        g1=p["norm1_g"][l].reshape(1, -1).astype(F32),
        chunk_bias=_chunk_bias_table(p["rel_bias"][l], ATT_TILE),
    )


def _mixers(x, sc1, sh1, lp, cst):
    tq = ATT_TILE
    (fq, fk, fvt, dq, dkv, dvt, dqi, dkit, dwt, cq, ck, cvt, eq, ek, ev) = _input_stage(
        x, sc1, sh1, lp["g1"], lp["w_in"], lp["w_in_t"], lp["wkv"], lp["wuvt"], lp["kvg"], lp["bf"],
        cst["tri_le"], cst["tab"], tq)
    ya = _fox_attention(fq, fk, fvt, tq)
    yb = _dsa_attention(dq, dkv, dvt, dqi, dkit, dwt, cst["tri_gt"], tq)
    yc = _sb_attention(cq, ck, cvt, cst["tri_lt"], tq)
    yd = _chunk_attention(eq, ek, ev, lp["chunk_bias"], tq)
    return ya, yb, yc, yd


def _moe(p, l, x, h, route, gate, counts, g2, tiles_per_batch, final):
    t, d = x.shape
    nt = t // TOK_TILE
    n_tiles = t * TOP_K // EXPERT_TILE + N_EXPERTS
    pstart, pend, tile_e, n_used = _moe_plan(counts, n_tiles)
    ids, ranks = route[:, :TOP_K, :], route[:, TOP_K:, :]
    first_row = sum(jnp.where(ids == e, pstart[e], 0) for e in range(N_EXPERTS))
    route = (first_row + ranks).reshape(nt, TOP_K * TOK_TILE)
    xs = _dispatch(pstart, pend, route, h.reshape(t, ROW_CHUNKS, LANES), n_tiles * EXPERT_TILE)
    ys = _experts(l, tile_e, n_used, xs, p["w_gu_bf16"], p["b_gu"][:, :, None, :].astype(F32),
                  p["w_dn_bf16"], p["b_dn"][:, :, None, :].astype(F32))
    gates_col = gate.transpose(0, 2, 1).reshape(t, ROUTE_ROWS)
    return _combine(route, ys, x, g2, gates_col, p["final_g"].reshape(1, d).astype(F32), tiles_per_batch, final)


def kernel(x, c, w_ada, b_ada, norm1_g, norm2_g, w_in, b_forget, kv_norm_g, w_uk, w_uv, rel_bias, w_o,
           w_router, b_router, w_gu, b_gu, w_dn, b_dn, final_g):
    p = dict(w_in=w_in, b_forget=b_forget, kv_norm_g=kv_norm_g, w_uk=w_uk, w_uv=w_uv, rel_bias=rel_bias,
             norm1_g=norm1_g, w_gu_bf16=w_gu.astype(BF16), b_gu=b_gu, w_dn_bf16=w_dn.astype(BF16), b_dn=b_dn,
             final_g=final_g)
    b, s, d = x.shape
    t = b * s
    depth = w_ada.shape[0]
    tiles_per_batch = s // TOK_TILE
    mod = _modulation(c, w_ada, b_ada)
    cst = _constants(s)
    for l in range(depth):
        sh1, sc1, g1, sh2, sc2, g2 = [m[:, None, :] for m in jnp.split(mod[l], 6, axis=-1)]
        ys = _mixers(x, sc1, sh1, _prep_layer(p, l), cst)
        x2, h, route, gate, cnt = _output_stage(
            [y.reshape(t, GROUP_WIDTH) for y in ys], w_o[l].astype(BF16), x.reshape(t, d), g1, sc2, sh2,
            norm2_g[l].reshape(1, d).astype(F32), w_router[l].T.astype(BF16),
            b_router[l].reshape(N_EXPERTS, 1).astype(F32), cst["tri_lt_tok"], tiles_per_batch)
        x = _moe(p, l, x2, h, route, gate, cnt[:, 0], g2, tiles_per_batch, l == depth - 1).reshape(b, s, d)
    return x
```

```python
import functools
import math

import jax
import jax.numpy as jnp
from jax import lax
from jax.experimental import pallas as pl
from jax.experimental.pallas import tpu as pltpu

F32 = jnp.float32
BF16 = jnp.bfloat16

HEAD_DIM = 64
HEADS = 4
GROUP_WIDTH = HEADS * HEAD_DIM
CHUNK = 64
ROPE_THETA = 10000.0
NORM_EPS = 1e-6
NEG_INF = -1e30
KV_RANK = 128
IDX_HEADS = 4
IDX_DIM = 32
TOPK_MAX = 256
LEFT_CHUNKS = 8
MAX_REL = 128
N_EXPERTS = 32
TOP_K = 4
SWIGLU_ALPHA = 1.702
SWIGLU_LIMIT = 7.0
LOG2E = math.log2(math.e)
INT_MIN = -2 ** 31

LANES = 128
SUBLANES = 8
VMEM_LIMIT = 48 * 1024 * 1024

ATT_TILE = 256
TOK_TILE = 512
EXPERT_TILE = 512
ROUTE_ROWS = 2 * TOP_K
DMA_UNROLL = 8

COL_A = 0
COL_C = 3 * GROUP_WIDTH
COL_D = 6 * GROUP_WIDTH
COL_QB = 9 * GROUP_WIDTH
COL_LAT = COL_QB + GROUP_WIDTH
COL_QI = COL_LAT + KV_RANK
COL_MISC = COL_QI + IDX_HEADS * IDX_DIM
N_COLS = COL_MISC + LANES
MISC_WI = IDX_DIM
MISC_FA = IDX_DIM + IDX_HEADS
ROW_VA = 0
ROW_VC = GROUP_WIDTH
ROW_VD = 2 * GROUP_WIDTH
ROW_WI = 3 * GROUP_WIDTH
N_ROWS_T = ROW_WI + SUBLANES


def _cparams(sem):
    return pltpu.CompilerParams(dimension_semantics=sem, vmem_limit_bytes=VMEM_LIMIT)


def _split3(x):
    hi = x.astype(BF16)
    r = x - hi.astype(F32)
    mid = r.astype(BF16)
    lo = (r - mid.astype(F32)).astype(BF16)
    return hi, mid, lo


def _dot(a, b):
    return jnp.dot(a, b, preferred_element_type=F32)


def _dot_nt(a, b):
    return lax.dot_general(a, b, (((1,), (1,)), ((), ())), preferred_element_type=F32)


def _mod_body(c_ref, w_ref, b_ref, o_ref):
    c = c_ref[...]
    act = (c * jax.nn.sigmoid(c)).astype(BF16)
    o_ref[0] = _dot(act, w_ref[0].astype(BF16)) + b_ref[0]


def _modulation(c, w_ada, b_ada):
    depth, d, n = w_ada.shape
    b = c.shape[0]
    tn = 1024
    return pl.pallas_call(
        _mod_body,
        out_shape=jax.ShapeDtypeStruct((depth, b, n), F32),
        grid=(depth, n // tn),
        in_specs=[
            pl.BlockSpec((b, d), lambda l, j: (0, 0)),
            pl.BlockSpec((1, d, tn), lambda l, j: (l, 0, j)),
            pl.BlockSpec((1, 1, tn), lambda l, j: (l, 0, j)),
        ],
        out_specs=pl.BlockSpec((1, b, tn), lambda l, j: (l, 0, j)),
        compiler_params=_cparams(("arbitrary", "arbitrary")),
        name="adaln_mod",
    )(c, w_ada, b_ada.reshape(depth, 1, n))


def _rope_tables(s):
    pos = jnp.arange(s, dtype=F32)[:, None]
    lane = jnp.arange(LANES)[None, :]

    def tables(dim, active):
        half = dim // 2
        j = lane % dim
        inv = ROPE_THETA ** (-(j % half).astype(F32) / half)
        ang = pos * inv
        cos = jnp.where(active, jnp.cos(ang), 1.0)
        sin = jnp.where(active, jnp.sin(ang), 0.0)
        first = j < half
        return [cos, jnp.where(first, -sin, 0.0), jnp.where(first, 0.0, sin)]

    everywhere = lane >= 0
    t = (tables(HEAD_DIM, everywhere) + tables(HEAD_DIM, lane < HEAD_DIM)
         + tables(IDX_DIM, everywhere) + tables(IDX_DIM, lane < IDX_DIM))
    return jnp.stack(t).astype(F32)


def _rope(x, tab, k, half):
    cos, sin_a, sin_b = tab[3 * k], tab[3 * k + 1], tab[3 * k + 2]
    return x * cos + pltpu.roll(x, LANES - half, 1) * sin_a + pltpu.roll(x, half, 1) * sin_b


def _instage_body(x_ref, sc_ref, sh_ref, g_ref, w_ref, wt_ref, wkv_ref, wuvt_ref, kvg_ref, bf_ref, tri_ref, tab_ref,
                  fq_o, fk_o, fvt_o, dq_o, dkv_o, dvt_o, dqi_o, dkit_o, dwt_o,
                  cq_o, ck_o, cvt_o, eq_o, ek_o, ev_o, carry_ref):
    tm = x_ref.shape[1]
    x = x_ref[0]
    y = x * lax.rsqrt(jnp.mean(x * x, axis=-1, keepdims=True) + NORM_EPS) * g_ref[...]
    h = (y * (1.0 + sc_ref[0]) + sh_ref[0]).astype(BF16)

    def proj(c0, n):
        return _dot(h, w_ref[:, c0:c0 + n])

    def proj_t(r0, n):
        return _dot_nt(wt_ref[r0:r0 + n, :], h)

    def heads_out(val, refs, scale):
        for i, ref in enumerate(refs):
            for hd in range(HEADS):
                c0 = i * GROUP_WIDTH + hd * HEAD_DIM
                piece = val[:, c0:c0 + HEAD_DIM]
                if i == 0:
                    piece = piece * scale
                ref[0, hd] = piece.astype(ref.dtype)

    def heads_out_t(val, ref):
        for hd in range(HEADS):
            ref[0, hd] = val[hd * HEAD_DIM:(hd + 1) * HEAD_DIM, :].astype(ref.dtype)

    scale = HEAD_DIM ** -0.5
    heads_out(proj(COL_C, 2 * GROUP_WIDTH), (cq_o, ck_o), scale)
    heads_out_t(proj_t(ROW_VC, GROUP_WIDTH), cvt_o)
    heads_out(proj(COL_D, 2 * GROUP_WIDTH), (eq_o, ek_o), scale)
    heads_out_t(proj_t(ROW_VD, GROUP_WIDTH), ev_o)

    misc = _rope(proj(COL_MISC, LANES), tab_ref, 3, IDX_DIM // 2)
    lane = lax.broadcasted_iota(jnp.int32, (tm, LANES), 1)
    ki = jnp.where(lane < IDX_DIM, misc, 0.0)
    kit = ki + pltpu.roll(ki, IDX_DIM, 1) + pltpu.roll(ki, 2 * IDX_DIM, 1) + pltpu.roll(ki, 3 * IDX_DIM, 1)
    dkit_o[0] = kit.astype(BF16)
    dwt_o[0] = proj_t(ROW_WI, SUBLANES) * ((IDX_HEADS ** -0.5) * (IDX_DIM ** -0.5))
    qi = _rope(proj(COL_QI, LANES), tab_ref, 2, IDX_DIM // 2)
    for hd in range(IDX_HEADS):
        in_head = (lane >= hd * IDX_DIM) & (lane < (hd + 1) * IDX_DIM)
        dqi_o[0, hd] = jnp.where(in_head, qi, 0.0).astype(BF16)
    lat = proj(COL_LAT, KV_RANK)
    lat = (lat * lax.rsqrt(jnp.mean(lat * lat, axis=-1, keepdims=True) + NORM_EPS) * kvg_ref[...]).astype(BF16)
    dkv_o[0] = _rope(_dot(lat, wkv_ref[...]), tab_ref, 1, HEAD_DIM // 2).astype(BF16)
    dvt_o[0] = _dot_nt(wuvt_ref[...], lat).astype(BF16)
    qb = proj(COL_QB, GROUP_WIDTH)
    zeros64 = jnp.zeros((tm, HEAD_DIM), F32)
    for half_i in range(2):
        r = _rope(qb[:, half_i * LANES:(half_i + 1) * LANES], tab_ref, 0, HEAD_DIM // 2) * (scale * LOG2E)
        for j in range(2):
            piece = jnp.concatenate([r[:, j * HEAD_DIM:(j + 1) * HEAD_DIM], zeros64], axis=-1)
            dq_o[0, 2 * half_i + j] = piece.astype(BF16)

    @pl.when(pl.program_id(1) == 0)
    def _():
        carry_ref[...] = jnp.zeros_like(carry_ref)

    z = misc + bf_ref[...]
    log_f = jnp.minimum(z, 0.0) - jnp.log1p(jnp.exp(-jnp.abs(z)))
    tri = tri_ref[...]
    f_cum = carry_ref[...] + sum(_dot(tri, p) for p in _split3(log_f))
    carry_ref[...] = f_cum[tm - 1:tm, :]
    f_cum = f_cum * LOG2E
    pa = proj(COL_A, 2 * GROUP_WIDTH)
    heads_out_t(proj_t(ROW_VA, GROUP_WIDTH), fvt_o)
    lane64 = lax.broadcasted_iota(jnp.int32, (tm, HEAD_DIM), 1)
    for hd in range(HEADS):
        fh = f_cum[:, MISC_FA + hd:MISC_FA + hd + 1]
        f3 = [jnp.broadcast_to(p.astype(F32), (tm, HEAD_DIM)) for p in _split3(fh)]
        one = jnp.where(lane64 < 6, 1.0, 0.0)
        q_ext = jnp.where(lane64 == 0, f3[0], jnp.where(lane64 == 1, f3[1], jnp.where(lane64 == 2, f3[2], one)))
        k_ext = jnp.where(lane64 == 3, -f3[0], jnp.where(lane64 == 4, -f3[1], jnp.where(lane64 == 5, -f3[2], one)))
        q = pa[:, hd * HEAD_DIM:(hd + 1) * HEAD_DIM] * (scale * LOG2E)
        k = pa[:, GROUP_WIDTH + hd * HEAD_DIM:GROUP_WIDTH + (hd + 1) * HEAD_DIM]
        fq_o[0, hd] = jnp.concatenate([q, q_ext], axis=-1).astype(BF16)
        fk_o[0, hd] = jnp.concatenate([k, k_ext], axis=-1).astype(BF16)


def _reorder_w_in(w_in):
    gw = GROUP_WIDTH
    a0 = 0
    b0 = 3 * gw + HEADS
    c0 = b0 + gw + KV_RANK + IDX_HEADS * IDX_DIM + IDX_DIM + IDX_HEADS
    d0 = c0 + 3 * gw
    lat0 = b0 + gw
    qi0 = lat0 + KV_RANK
    ki0 = qi0 + IDX_HEADS * IDX_DIM
    wi0 = ki0 + IDX_DIM
    pad = jnp.zeros(w_in.shape[:-1] + (LANES - IDX_DIM - IDX_HEADS - HEADS,), w_in.dtype)
    w = jnp.concatenate([
        w_in[:, a0:a0 + 3 * gw], w_in[:, c0:c0 + 3 * gw], w_in[:, d0:d0 + 3 * gw],
        w_in[:, b0:b0 + gw], w_in[:, lat0:lat0 + KV_RANK], w_in[:, qi0:qi0 + IDX_HEADS * IDX_DIM],
        w_in[:, ki0:ki0 + IDX_DIM], w_in[:, wi0:wi0 + IDX_HEADS], w_in[:, 3 * gw:3 * gw + HEADS], pad,
    ], axis=-1)
    wt = jnp.concatenate([
        w_in[:, a0 + 2 * gw:a0 + 3 * gw], w_in[:, c0 + 2 * gw:c0 + 3 * gw], w_in[:, d0 + 2 * gw:d0 + 3 * gw],
        w_in[:, wi0:wi0 + IDX_HEADS],
        jnp.zeros((w_in.shape[0], SUBLANES - IDX_HEADS), w_in.dtype),
    ], axis=-1).T
    return w.astype(BF16), wt.astype(BF16)


def _input_stage(x, sc, sh, g, w, wt, wkv, wuvt, kvg, bf, tri, tab, tm):
    b, s, d = x.shape
    hq = lambda width: jax.ShapeDtypeStruct((b, HEADS, s, width), BF16)
    hqt = jax.ShapeDtypeStruct((b, HEADS, HEAD_DIM, s), BF16)
    flat = jax.ShapeDtypeStruct((b, s, LANES), BF16)
    out_shape = (
        hq(LANES), hq(LANES), hqt,
        hq(LANES), flat, jax.ShapeDtypeStruct((b, HEAD_DIM, s), BF16),
        hq(LANES), flat, jax.ShapeDtypeStruct((b, SUBLANES, s), F32),
        hq(HEAD_DIM), hq(HEAD_DIM), hqt,
        hq(HEAD_DIM), hq(HEAD_DIM), hqt,
    )
    hspec = lambda width: pl.BlockSpec((1, HEADS, tm, width), lambda i, j: (i, 0, j, 0))
    htspec = pl.BlockSpec((1, HEADS, HEAD_DIM, tm), lambda i, j: (i, 0, 0, j))
    fspec = pl.BlockSpec((1, tm, LANES), lambda i, j: (i, j, 0))
    tspec = lambda rows: pl.BlockSpec((1, rows, tm), lambda i, j: (i, 0, j))
    const2 = lambda shape: pl.BlockSpec(shape, lambda i, j: (0, 0))
    out_specs = (
        hspec(LANES), hspec(LANES), htspec,
        hspec(LANES), fspec, tspec(HEAD_DIM),
        hspec(LANES), fspec, tspec(SUBLANES),
        hspec(HEAD_DIM), hspec(HEAD_DIM), htspec,
        hspec(HEAD_DIM), hspec(HEAD_DIM), htspec,
    )
    return pl.pallas_call(
        _instage_body,
        out_shape=out_shape,
        grid=(b, s // tm),
        in_specs=[
            pl.BlockSpec((1, tm, d), lambda i, j: (i, j, 0)),
            pl.BlockSpec((1, 1, d), lambda i, j: (i, 0, 0)),
            pl.BlockSpec((1, 1, d), lambda i, j: (i, 0, 0)),
            const2((1, d)),
            const2((d, N_COLS)),
            const2((N_ROWS_T, d)),
            const2((KV_RANK, LANES)),
            const2((HEAD_DIM, KV_RANK)),
            const2((1, KV_RANK)),
            const2((1, LANES)),
            const2((tm, tm)),
            pl.BlockSpec((12, tm, LANES), lambda i, j: (0, j, 0)),
        ],
        out_specs=out_specs,
        scratch_shapes=[pltpu.VMEM((1, LANES), F32)],
        compiler_params=_cparams(("arbitrary", "arbitrary")),
        name="mixer_input_stage",
    )(x, sc, sh, g, w, wt, wkv, wuvt, kvg, bf, tri, tab)


def _key_query_iota(t):
    return lax.broadcasted_iota(jnp.int32, (t, t), 0), lax.broadcasted_iota(jnp.int32, (t, t), 1)


def _softmax2_step(s, vt, m, l, acc):
    m_new = jnp.maximum(m, jnp.max(s, axis=0, keepdims=True))
    alpha = jnp.exp2(m - m_new)
    p = jnp.exp2(s - m_new)
    l_new = alpha * l + jnp.sum(p, axis=0, keepdims=True)
    acc_new = alpha * acc + _dot(vt, p.astype(BF16))
    return m_new, l_new, acc_new


def _skewed(n, stages):
    vals = [None] * n
    for step in range(n + len(stages) - 1):
        for j, stage in enumerate(stages):
            i = step - j
            if 0 <= i < n:
                vals[i] = stage(i, vals[i])


def _heads_to_rows(accs):
    return jnp.concatenate([a.T for a in accs], axis=-1)


def _fox_body(q_ref, k_ref, vt_ref, o_ref, m_ref, l_ref, acc_ref):
    qi = pl.program_id(1)
    tq = q_ref.shape[2]
    key, qry = _key_query_iota(tq)
    m_ref[...] = jnp.full_like(m_ref, NEG_INF)
    l_ref[...] = jnp.zeros_like(l_ref)
    acc_ref[...] = jnp.zeros_like(acc_ref)

    def blocks(kbs):
        items = [(pl.multiple_of(kb * tq, tq), masked, hd) for kb, masked in kbs for hd in range(HEADS)]

        def scores(i, _):
            k0, masked, hd = items[i]
            s = _dot_nt(k_ref[0, hd, pl.ds(k0, tq), :], q_ref[0, hd])
            return jnp.where(key <= qry, s, NEG_INF) if masked else s

        def update(i, s):
            k0, _, hd = items[i]
            m_ref[hd], l_ref[hd], acc_ref[hd] = _softmax2_step(
                s, vt_ref[0, hd, :, pl.ds(k0, tq)], m_ref[hd], l_ref[hd], acc_ref[hd])

        _skewed(len(items), [scores, update])

    def pair_body(j, carry):
        blocks([(2 * j, False), (2 * j + 1, False)])
        return carry

    lax.fori_loop(0, lax.shift_right_logical(qi, 1), pair_body, 0)

    @pl.when((qi & 1) == 1)
    def _():
        blocks([(qi - 1, False), (qi, True)])

    @pl.when((qi & 1) == 0)
    def _():
        blocks([(qi, True)])

    o_ref[0] = _heads_to_rows([acc_ref[hd] / l_ref[hd] for hd in range(HEADS)]).astype(o_ref.dtype)


def _fox_attention(q, k, vt, tq):
    b, _, s, _ = q.shape
    return pl.pallas_call(
        _fox_body,
        out_shape=jax.ShapeDtypeStruct((b, s, GROUP_WIDTH), BF16),
        grid=(b, s // tq),
        in_specs=[
            pl.BlockSpec((1, HEADS, tq, LANES), lambda i, j: (i, 0, j, 0)),
            pl.BlockSpec((1, HEADS, s, LANES), lambda i, j: (i, 0, 0, 0)),
            pl.BlockSpec((1, HEADS, HEAD_DIM, s), lambda i, j: (i, 0, 0, 0)),
        ],
        out_specs=pl.BlockSpec((1, tq, GROUP_WIDTH), lambda i, j: (i, j, 0)),
        scratch_shapes=[pltpu.VMEM((HEADS, 1, tq), F32), pltpu.VMEM((HEADS, 1, tq), F32),
                        pltpu.VMEM((HEADS, HEAD_DIM, tq), F32)],
        compiler_params=_cparams(("arbitrary", "arbitrary")),
        name="forgetting_attention",
    )(q, k, vt)


def _sb_body(q_ref, k_ref, vt_ref, tri_ref, o_ref, r_ref, acc_ref):
    qi = pl.program_id(1)
    tq = q_ref.shape[2]
    key, qry = _key_query_iota(tq)
    tri = tri_ref[...]
    r_ref[...] = jnp.zeros_like(r_ref)
    acc_ref[...] = jnp.zeros_like(acc_ref)

    def blocks(kbs):
        items = [(pl.multiple_of(kb * tq, tq), masked, hd) for kb, masked in kbs for hd in range(HEADS)]

        def logits(i, _):
            k0, _, hd = items[i]
            return _dot_nt(k_ref[0, hd, pl.ds(k0, tq), :], q_ref[0, hd])

        def later_sums(i, z):
            _, masked, hd = items[i]
            log_1m = -(jnp.maximum(z, 0.0) + jnp.log(1.0 + jnp.exp2(jnp.abs(z) * (-LOG2E))))
            if masked:
                log_1m = jnp.where(key < qry, log_1m, 0.0)
            after = r_ref[hd] + _dot(tri, log_1m.astype(BF16))
            return z, log_1m, after

        def accumulate(i, val):
            k0, masked, hd = items[i]
            z, log_1m, after = val
            w = jnp.exp(log_1m + z + after)
            if masked:
                w = jnp.where(key < qry, w, 0.0)
            acc_ref[hd] += _dot(vt_ref[0, hd, :, pl.ds(k0, tq)], w.astype(BF16))
            r_ref[hd] += jnp.sum(log_1m, axis=0, keepdims=True)

        _skewed(len(items), [logits, later_sums, accumulate])

    @pl.when((qi & 1) == 1)
    def _():
        blocks([(qi, True), (qi - 1, False)])

    @pl.when((qi & 1) == 0)
    def _():
        blocks([(qi, True)])

    top = qi - 1 - (qi & 1)

    def pair_body(j, carry):
        blocks([(top - 2 * j, False), (top - 2 * j - 1, False)])
        return carry

    lax.fori_loop(0, lax.shift_right_logical(qi, 1), pair_body, 0)
    o_ref[0] = _heads_to_rows([acc_ref[hd] for hd in range(HEADS)]).astype(o_ref.dtype)


def _sb_attention(q, k, vt, tri, tq):
    b, _, s, _ = q.shape
    return pl.pallas_call(
        _sb_body,
        out_shape=jax.ShapeDtypeStruct((b, s, GROUP_WIDTH), BF16),
        grid=(b, s // tq),
        in_specs=[
            pl.BlockSpec((1, HEADS, tq, HEAD_DIM), lambda i, j: (i, 0, j, 0)),
            pl.BlockSpec((1, HEADS, s, HEAD_DIM), lambda i, j: (i, 0, 0, 0)),
            pl.BlockSpec((1, HEADS, HEAD_DIM, s), lambda i, j: (i, 0, 0, 0)),
            pl.BlockSpec((tq, tq), lambda i, j: (0, 0)),
        ],
        out_specs=pl.BlockSpec((1, tq, GROUP_WIDTH), lambda i, j: (i, j, 0)),
        scratch_shapes=[pltpu.VMEM((HEADS, 1, tq), F32), pltpu.VMEM((HEADS, HEAD_DIM, tq), F32)],
        compiler_params=_cparams(("arbitrary", "arbitrary")),
        name="stick_breaking_attention",
    )(q, k, vt, tri)


def _chunk_bias_table(rel_table, tq):
    left = LEFT_CHUNKS * CHUNK
    width = left + tq
    d = jnp.arange(width + tq - 1) - (tq - 1) - left
    diag = rel_table[:, jnp.clip(d, -MAX_REL, MAX_REL) + MAX_REL].astype(F32)
    bias = jnp.stack([diag[:, tq - 1 - t:tq - 1 - t + width] for t in range(tq)], axis=1)
    t = jnp.arange(tq)[:, None]
    j = jnp.arange(width)[None, :]
    c0 = (t // CHUNK) * CHUNK
    visible = (j >= c0) & (j - left < c0 + CHUNK)
    return jnp.where(visible[None], bias, NEG_INF)


def _chunk_body(nb, q_ref, *refs):
    k_refs, v_refs = refs[:nb], refs[nb:2 * nb]
    bias_ref, o_ref = refs[2 * nb], refs[2 * nb + 1]
    qi = pl.program_id(1)
    tq = q_ref.shape[2]
    outs = [None] * HEADS

    def scores(hd, _):
        parts = []
        for n in range(nb):
            s = _dot_nt(k_refs[n][0, hd], q_ref[0, hd])
            parts.append(jnp.where(qi - (nb - 1) + n >= 0, s, NEG_INF))
        return jnp.concatenate(parts, axis=0) + bias_ref[hd]

    def finish(hd, s):
        m = jnp.max(s, axis=0, keepdims=True)
        p = jnp.exp(s - m)
        l = jnp.sum(p, axis=0, keepdims=True)
        p = p.astype(BF16)
        acc = sum(_dot(v_refs[n][0, hd], p[n * tq:(n + 1) * tq, :]) for n in range(nb))
        outs[hd] = acc / l

    _skewed(HEADS, [scores, finish])
    o_ref[0] = _heads_to_rows(outs).astype(o_ref.dtype)


def _chunk_attention(q, k, v, bias, tq):
    b, _, s, _ = q.shape
    left = LEFT_CHUNKS * CHUNK
    assert left % tq == 0
    nb = left // tq + 1
    k_specs = [
        pl.BlockSpec((1, HEADS, tq, HEAD_DIM), functools.partial(
            lambda i, j, n: (i, 0, jnp.maximum(j - (nb - 1) + n, 0), 0), n=n))
        for n in range(nb)
    ]
    vt_specs = [
        pl.BlockSpec((1, HEADS, HEAD_DIM, tq), functools.partial(
            lambda i, j, n: (i, 0, 0, jnp.maximum(j - (nb - 1) + n, 0)), n=n))
        for n in range(nb)
    ]
    return pl.pallas_call(
        functools.partial(_chunk_body, nb),
        out_shape=jax.ShapeDtypeStruct((b, s, GROUP_WIDTH), BF16),
        grid=(b, s // tq),
        in_specs=[pl.BlockSpec((1, HEADS, tq, HEAD_DIM), lambda i, j: (i, 0, j, 0))] + k_specs + vt_specs
        + [pl.BlockSpec((HEADS, nb * tq, tq), lambda i, j: (0, 0, 0))],
        out_specs=pl.BlockSpec((1, tq, GROUP_WIDTH), lambda i, j: (i, j, 0)),
        compiler_params=_cparams(("arbitrary", "arbitrary")),
        name="chunked_relbias_attention",
    )(q, *([k] * nb), *([v] * nb), bias)


def _dsa_body(n_sel, q_ref, kv_ref, vt_ref, qi_ref, kit_ref, wt_ref, tri_ref, o_ref,
              key_ref, bias_ref, m_ref, l_ref, acc_ref):
    i = pl.program_id(1)
    tq = q_ref.shape[2]
    nkb = i + 1
    key_pos, qry_pos = _key_query_iota(tq)
    vis_diag = key_pos < (qry_pos // CHUNK + 1) * CHUNK
    wt = wt_ref[0]

    def score_block(kb, masked):
        k0 = pl.multiple_of(kb * tq, tq)
        kit = kit_ref[0, pl.ds(k0, tq), :]
        dots = [_dot_nt(kit, qi_ref[0, hd]) for hd in range(IDX_HEADS)]
        score = sum(wt[hd:hd + 1, :] * jnp.maximum(dots[hd], 0.0) for hd in range(IDX_HEADS))
        score = jnp.where(score == 0.0, 0.0, score)
        bits = pltpu.bitcast(score, jnp.int32)
        okey = jnp.where(bits < 0, bits ^ 0x7FFFFFFF, bits)
        if masked:
            okey = jnp.where(vis_diag, okey, INT_MIN)
        key_ref[kb] = okey

    def score_loop(kb, carry):
        score_block(kb, False)
        return carry

    lax.fori_loop(0, i, score_loop, 0)
    score_block(i, True)

    def count(pred):
        def hits(kb):
            ind = jnp.where(pred(key_ref[kb]), 1.0, 0.0)
            return ind.reshape(tq // SUBLANES, SUBLANES, tq).sum(axis=0)

        def pair(j, acc):
            return acc + hits(2 * j) + hits(2 * j + 1)

        acc = lax.fori_loop(0, lax.shift_right_logical(nkb, 1), pair, jnp.zeros((SUBLANES, tq), F32))
        acc = acc + lax.cond((nkb & 1) == 1, lambda: hits(nkb - 1), lambda: jnp.zeros((SUBLANES, tq), F32))
        return jnp.sum(acc, axis=0, keepdims=True)

    def bit_step(it, ans_u):
        cand_u = ans_u | lax.shift_left(jnp.int32(1), 31 - it)
        cand = cand_u ^ INT_MIN
        return jnp.where(count(lambda k: k >= cand) >= n_sel, cand_u, ans_u)

    thr = lax.fori_loop(0, 32, bit_step, jnp.zeros((1, tq), jnp.int32)) ^ INT_MIN

    n_gt = count(lambda k: k > thr)
    need = jnp.where(thr != INT_MIN, n_sel - n_gt, 0.0)
    tri = tri_ref[...]

    def select_block(kb, seen):
        okey = key_ref[kb]
        eq = okey == thr
        eq_f = jnp.where(eq, 1.0, 0.0)
        earlier = seen + _dot(tri, eq_f.astype(BF16))
        tie_bias = jnp.where(eq, jnp.where(earlier < need, 0.0, NEG_INF), NEG_INF)
        bias_ref[kb] = jnp.where(okey > thr, 0.0, tie_bias)
        return seen + jnp.sum(eq_f, axis=0, keepdims=True)

    lax.fori_loop(0, nkb, select_block, jnp.zeros((1, tq), F32))

    m_ref[...] = jnp.full_like(m_ref, NEG_INF)
    l_ref[...] = jnp.zeros_like(l_ref)
    acc_ref[...] = jnp.zeros_like(acc_ref)

    def attend(kbs):
        items = [(kb, pl.multiple_of(kb * tq, tq), hd) for kb in kbs for hd in range(HEADS)]

        def scores(n, _):
            kb, k0, hd = items[n]
            return _dot_nt(kv_ref[0, pl.ds(k0, tq), :], q_ref[0, hd]) + bias_ref[kb]

        def update(n, s):
            _, k0, hd = items[n]
            m_ref[hd], l_ref[hd], acc_ref[hd] = _softmax2_step(
                s, vt_ref[0, :, pl.ds(k0, tq)], m_ref[hd], l_ref[hd], acc_ref[hd])

        _skewed(len(items), [scores, update])

    def attend_pair(j, carry):
        attend([2 * j, 2 * j + 1])
        return carry

    lax.fori_loop(0, lax.shift_right_logical(nkb, 1), attend_pair, 0)

    @pl.when((nkb & 1) == 1)
    def _():
        attend([nkb - 1])
    o_ref[0] = _heads_to_rows([acc_ref[hd] / l_ref[hd] for hd in range(HEADS)]).astype(o_ref.dtype)


def _dsa_attention(q, kv, vt, qi, kit, wt, tri, tq):
    b, _, s, _ = q.shape
    n_sel = min(TOPK_MAX, s // 4)
    return pl.pallas_call(
        functools.partial(_dsa_body, n_sel),
        out_shape=jax.ShapeDtypeStruct((b, s, GROUP_WIDTH), BF16),
        grid=(b, s // tq),
        in_specs=[
            pl.BlockSpec((1, HEADS, tq, LANES), lambda i, j: (i, 0, j, 0)),
            pl.BlockSpec((1, s, LANES), lambda i, j: (i, 0, 0)),
            pl.BlockSpec((1, HEAD_DIM, s), lambda i, j: (i, 0, 0)),
            pl.BlockSpec((1, IDX_HEADS, tq, LANES), lambda i, j: (i, 0, j, 0)),
            pl.BlockSpec((1, s, LANES), lambda i, j: (i, 0, 0)),
            pl.BlockSpec((1, SUBLANES, tq), lambda i, j: (i, 0, j)),
            pl.BlockSpec((tq, tq), lambda i, j: (0, 0)),
        ],
        out_specs=pl.BlockSpec((1, tq, GROUP_WIDTH), lambda i, j: (i, j, 0)),
        scratch_shapes=[
            pltpu.VMEM((s // tq, tq, tq), jnp.int32), pltpu.VMEM((s // tq, tq, tq), F32),
            pltpu.VMEM((HEADS, 1, tq), F32), pltpu.VMEM((HEADS, 1, tq), F32),
            pltpu.VMEM((HEADS, HEAD_DIM, tq), F32),
        ],
        compiler_params=_cparams(("arbitrary", "arbitrary")),
        name="sparse_indexer_attention",
    )(q, kv, vt, qi, kit, wt, tri)


ROW_CHUNKS = 8


def _store_rows(ref, val, r0=0):
    n = val.shape[0]
    for c in range(ROW_CHUNKS):
        ref[pl.ds(r0 * ROW_CHUNKS + c, n, stride=ROW_CHUNKS), :] = val[:, c * LANES:(c + 1) * LANES]


def _load_rows(ref, r0, n):
    return jnp.concatenate(
        [ref[pl.ds(r0 * ROW_CHUNKS + c, n, stride=ROW_CHUNKS), :] for c in range(ROW_CHUNKS)], axis=-1)


def _outstage_body(ya_ref, yb_ref, yc_ref, yd_ref, wo_ref, x_ref, g1_ref, sc_ref, sh_ref, ng_ref,
                   wr_ref, br_ref, tri_ref, xo_ref, h_ref, route_ref, gate_ref, cnt_ref, carry_ref):
    tm = x_ref.shape[0]
    gw = GROUP_WIDTH
    mix = sum(_dot(r[...], wo_ref[n * gw:(n + 1) * gw, :]) for n, r in enumerate((ya_ref, yb_ref, yc_ref, yd_ref)))
    x = x_ref[...] + g1_ref[0] * mix
    xo_ref[...] = x
    y = x * lax.rsqrt(jnp.mean(x * x, axis=-1, keepdims=True) + NORM_EPS) * ng_ref[...]
    h = y * (1.0 + sc_ref[0]) + sh_ref[0]
    _store_rows(h_ref, h)

    @pl.when(pl.program_id(0) == 0)
    def _():
        carry_ref[...] = jnp.zeros_like(carry_ref)

    logits = _dot_nt(wr_ref[...], h.astype(BF16)) + br_ref[...]
    e_iota = lax.broadcasted_iota(jnp.int32, (N_EXPERTS, tm), 0).astype(F32)
    vals, ids = [], []
    for _ in range(TOP_K):
        top = jnp.max(logits, axis=0, keepdims=True)
        idx = jnp.min(jnp.where(logits == top, e_iota, float(N_EXPERTS)), axis=0, keepdims=True)
        logits = jnp.where(e_iota == idx, -jnp.inf, logits)
        vals.append(top)
        ids.append(idx)
    exps = [jnp.exp(v - vals[0]) for v in vals]
    denom = sum(exps)
    onehot = sum(jnp.where(e_iota == idx, 1.0, 0.0) for idx in ids)
    before = carry_ref[...] + _dot(onehot.astype(BF16), tri_ref[...])
    ranks = [jnp.sum(jnp.where(e_iota == idx, before, 0.0), axis=0, keepdims=True) for idx in ids]
    carry_ref[...] += jnp.sum(onehot, axis=1, keepdims=True)
    route_ref[0] = jnp.concatenate(ids + ranks, axis=0).astype(jnp.int32)
    gate_ref[0] = jnp.concatenate([e / denom for e in exps] + [jnp.zeros_like(denom)] * TOP_K, axis=0)
    cnt_ref[...] = jnp.broadcast_to(carry_ref[...], cnt_ref.shape)


def _output_stage(ys, wo, x, g1, sc2, sh2, ng, wr_t, br, tri, tiles_per_batch):
    t, d = x.shape
    tm = TOK_TILE
    nt = t // tm
    row = lambda width: pl.BlockSpec((tm, width), lambda i: (i, 0))
    const = lambda shape: pl.BlockSpec(shape, lambda i: (0,) * len(shape))
    per_batch = pl.BlockSpec((1, 1, d), lambda i: (i // tiles_per_batch, 0, 0))
    return pl.pallas_call(
        _outstage_body,
        out_shape=(
            jax.ShapeDtypeStruct((t, d), F32), jax.ShapeDtypeStruct((t * ROW_CHUNKS, LANES), F32),
            jax.ShapeDtypeStruct((nt, ROUTE_ROWS, tm), jnp.int32), jax.ShapeDtypeStruct((nt, ROUTE_ROWS, tm), F32),
            jax.ShapeDtypeStruct((N_EXPERTS, LANES), F32),
        ),
        grid=(nt,),
        in_specs=[row(GROUP_WIDTH)] * 4 + [const((d, d)), row(d), per_batch, per_batch, per_batch, const((1, d)),
                                           const((N_EXPERTS, d)), const((N_EXPERTS, 1)), const((tm, tm))],
        out_specs=(row(d), pl.BlockSpec((tm * ROW_CHUNKS, LANES), lambda i: (i, 0)),
                   pl.BlockSpec((1, ROUTE_ROWS, tm), lambda i: (i, 0, 0)),
                   pl.BlockSpec((1, ROUTE_ROWS, tm), lambda i: (i, 0, 0)),
                   const((N_EXPERTS, LANES))),
        scratch_shapes=[pltpu.VMEM((N_EXPERTS, 1), F32)],
        compiler_params=_cparams(("arbitrary",)),
        name="mixer_output_stage_router",
    )(*ys, wo, x, g1, sc2, sh2, ng, wr_t, br, tri)


def _route_fetch(route_hbm, route_smem, rsem):
    i = pl.program_id(0)
    n = route_hbm.shape[1]
    slot = lax.rem(i, 2)

    def fetch(step, sl):
        dst = route_smem.at[pl.ds(pl.multiple_of(sl * n, n), n)]
        return pltpu.make_async_copy(route_hbm.at[step], dst, rsem.at[sl])

    @pl.when(i == 0)
    def _():
        fetch(0, 0).start()

    fetch(i, slot).wait()

    @pl.when(i + 1 < pl.num_programs(0))
    def _():
        fetch(i + 1, 1 - slot).start()

    return slot * n


H_BUFFERS = 3


def _dispatch_body(pstart_ref, pend_ref, route_hbm, h_hbm, xs_hbm, route_smem, hbuf, zero_ref, rsem, hsem, zsem, sem):
    tm = hbuf.shape[1]
    i = pl.program_id(0)
    n = pl.num_programs(0)

    def h_load(step, b):
        return pltpu.make_async_copy(h_hbm.at[pl.ds(pl.multiple_of(step * tm, tm), tm)], hbuf.at[b], hsem.at[b])

    def wait_rows(parity):
        for _ in range(TOP_K):
            pltpu.make_async_copy(hbuf.at[0], xs_hbm.at[pl.ds(0, tm)], sem.at[parity]).wait()

    @pl.when(i == 0)
    def _():
        h_load(0, 0).start()

        @pl.when(n > 1)
        def _():
            h_load(1, 1).start()

        zero_ref[...] = jnp.zeros_like(zero_ref)

        def fill(e):
            start = pl.multiple_of(pend_ref[e] - EXPERT_TILE, EXPERT_TILE)
            return pltpu.make_async_copy(zero_ref, xs_hbm.at[pl.ds(start, EXPERT_TILE)], zsem)

        def start(e, c):
            @pl.when(pend_ref[e] > pstart_ref[e])
            def _():
                fill(e).start()
            return c

        def wait(e, c):
            @pl.when(pend_ref[e] > pstart_ref[e])
            def _():
                fill(e).wait()
            return c

        lax.fori_loop(0, N_EXPERTS, start, 0)
        lax.fori_loop(0, N_EXPERTS, wait, 0)

    rec = _route_fetch(route_hbm, route_smem, rsem)
    b = lax.rem(i, H_BUFFERS)
    parity = lax.rem(i, 2)
    h_load(i, b).wait()
    h_ref = hbuf.at[b]
    for k in range(TOP_K):
        def start(j, c, base=rec + k * tm):
            for prio in range(2):
                t = 2 * j + prio
                pltpu.make_async_copy(h_ref.at[t], xs_hbm.at[route_smem[base + t]], sem.at[parity]).start(priority=prio)
            return c

        lax.fori_loop(0, tm // 2, start, 0, unroll=DMA_UNROLL // 2)

    @pl.when(i > 0)
    def _():
        wait_rows(1 - parity)

    @pl.when(i + 2 < n)
    def _():
        h_load(i + 2, lax.rem(i + 2, H_BUFFERS)).start()

    @pl.when(i == n - 1)
    def _():
        wait_rows(parity)


def _dispatch(pstart, pend, route, h, n_rows):
    t, dc, _ = h.shape
    tm = TOK_TILE
    return pl.pallas_call(
        _dispatch_body,
        out_shape=jax.ShapeDtypeStruct((n_rows, dc, LANES), F32),
        grid_spec=pltpu.PrefetchScalarGridSpec(
            num_scalar_prefetch=2,
            grid=(t // tm,),
            in_specs=[pl.BlockSpec(memory_space=pl.ANY), pl.BlockSpec(memory_space=pl.ANY)],
            out_specs=pl.BlockSpec(memory_space=pl.ANY),
            scratch_shapes=[
                pltpu.SMEM((2 * TOP_K * tm,), jnp.int32), pltpu.VMEM((H_BUFFERS, tm, dc, LANES), F32),
                pltpu.VMEM((EXPERT_TILE, dc, LANES), F32),
                pltpu.SemaphoreType.DMA((2,)), pltpu.SemaphoreType.DMA((H_BUFFERS,)), pltpu.SemaphoreType.DMA,
                pltpu.SemaphoreType.DMA((2,)),
            ],
        ),
        compiler_params=_cparams(("arbitrary",)),
        name="moe_dispatch",
    )(pstart, pend, route, h)


def _expert_body(tile_e_ref, n_used_ref, x_ref, wgu_ref, bgu_ref, wdn_ref, bdn_ref, y_ref):
    @pl.when(pl.program_id(0) < n_used_ref[0])
    def _():
        f = wdn_ref.shape[2]
        gu = _dot(_load_rows(x_ref, 0, EXPERT_TILE).astype(BF16), wgu_ref[0, 0]) + bgu_ref[0, 0]
        glu = jnp.minimum(gu[:, :f], SWIGLU_LIMIT)
        lin = jnp.clip(gu[:, f:], -SWIGLU_LIMIT, SWIGLU_LIMIT)
        act = glu * jax.nn.sigmoid(SWIGLU_ALPHA * glu) * (lin + 1.0)
        _store_rows(y_ref, _dot(act.astype(BF16), wdn_ref[0, 0]) + bdn_ref[0, 0])


def _experts(l, tile_e, n_used, xs, wgu, bgu, wdn, bdn):
    p, dc, _ = xs.shape
    assert dc == ROW_CHUNKS
    d = dc * LANES
    f = wdn.shape[2]
    tm = EXPERT_TILE
    used = lambda i, te, nu: jnp.minimum(i, nu[0] - 1)
    expert = lambda i, te, nu: (l, te[i], 0, 0)
    return pl.pallas_call(
        _expert_body,
        out_shape=jax.ShapeDtypeStruct((p * dc, LANES), F32),
        grid_spec=pltpu.PrefetchScalarGridSpec(
            num_scalar_prefetch=2,
            grid=(p // tm,),
            in_specs=[
                pl.BlockSpec((tm * dc, LANES), lambda i, te, nu: (used(i, te, nu), 0)),
                pl.BlockSpec((1, 1, d, 2 * f), expert),
                pl.BlockSpec((1, 1, 1, 2 * f), expert),
                pl.BlockSpec((1, 1, f, d), expert),
                pl.BlockSpec((1, 1, 1, d), expert),
            ],
            out_specs=pl.BlockSpec((tm * dc, LANES), lambda i, te, nu: (used(i, te, nu), 0)),
        ),
        compiler_params=_cparams(("arbitrary",)),
        name="moe_experts",
    )(tile_e, n_used, xs.reshape(p * dc, LANES), wgu, bgu, wdn, bdn).reshape(p, dc, LANES)


COMBINE_ROWS = 32
ROUTE_SLOTS = 3


def _combine_body(final, route_hbm, ys_hbm, x_ref, g2_ref, gate_ref, fg_ref, o_ref,
                  route_smem, buf_ref, rsem, sem):
    tm = x_ref.shape[0]
    i = pl.program_id(0)
    n = pl.num_programs(0)
    n_rec = TOP_K * tm

    def rec_fetch(step):
        sl = lax.rem(step, ROUTE_SLOTS)
        dst = route_smem.at[pl.ds(pl.multiple_of(sl * n_rec, n_rec), n_rec)]
        return pltpu.make_async_copy(route_hbm.at[step], dst, rsem.at[sl])

    def issue_gathers(step):
        rec = lax.rem(step, ROUTE_SLOTS) * n_rec
        half = lax.rem(step, 2)
        row0 = half * n_rec

        def start(j, c):
            for prio in range(2):
                r = 2 * j + prio
                dst = buf_ref.at[pl.ds(pl.multiple_of((row0 + r) * ROW_CHUNKS, ROW_CHUNKS), ROW_CHUNKS)]
                pltpu.make_async_copy(ys_hbm.at[route_smem[rec + r]], dst, sem.at[half]).start(priority=prio)
            return c

        lax.fori_loop(0, n_rec // 2, start, 0, unroll=DMA_UNROLL // 2)

    @pl.when(i == 0)
    def _():
        rec_fetch(0).start()

        @pl.when(n > 1)
        def _():
            rec_fetch(1).start()

        rec_fetch(0).wait()
        issue_gathers(0)

    @pl.when(i + 1 < n)
    def _():
        rec_fetch(i + 1).wait()
        issue_gathers(i + 1)

    @pl.when(i + 2 < n)
    def _():
        rec_fetch(i + 2).start()

    half = lax.rem(i, 2)
    half_view = buf_ref.at[pl.ds(0, n_rec * ROW_CHUNKS)]
    pltpu.make_async_copy(half_view, half_view, sem.at[half]).wait()
    g2 = g2_ref[0]
    fg = fg_ref[...]

    def rows(chunk, c):
        r0 = pl.multiple_of(chunk * COMBINE_ROWS, COMBINE_ROWS)
        gates = gate_ref[pl.ds(r0, COMBINE_ROWS), :]
        moe = sum(gates[:, k:k + 1] * _load_rows(buf_ref, half * n_rec + k * tm + r0, COMBINE_ROWS)
                  for k in range(TOP_K))
        x = x_ref[pl.ds(r0, COMBINE_ROWS), :] + g2 * moe
        if final:
            x = x * lax.rsqrt(jnp.mean(x * x, axis=-1, keepdims=True) + NORM_EPS) * fg
        o_ref[pl.ds(r0, COMBINE_ROWS), :] = x
        return c

    lax.fori_loop(0, tm // COMBINE_ROWS, rows, 0)


def _combine(route, ys, x, g2, gates_col, final_g, tiles_per_batch, final):
    t, d = x.shape
    tm = TOK_TILE
    return pl.pallas_call(
        functools.partial(_combine_body, final),
        out_shape=jax.ShapeDtypeStruct((t, d), F32),
        grid=(t // tm,),
        in_specs=[
            pl.BlockSpec(memory_space=pl.ANY), pl.BlockSpec(memory_space=pl.ANY),
            pl.BlockSpec((tm, d), lambda i: (i, 0)),
            pl.BlockSpec((1, 1, d), lambda i: (i // tiles_per_batch, 0, 0)),
            pl.BlockSpec((tm, ROUTE_ROWS), lambda i: (i, 0)),
            pl.BlockSpec((1, d), lambda i: (0, 0)),
        ],
        out_specs=pl.BlockSpec((tm, d), lambda i: (i, 0)),
        scratch_shapes=[
            pltpu.SMEM((ROUTE_SLOTS * TOP_K * tm,), jnp.int32), pltpu.VMEM((2 * TOP_K * tm * ROW_CHUNKS, LANES), F32),
            pltpu.SemaphoreType.DMA((ROUTE_SLOTS,)), pltpu.SemaphoreType.DMA((2,)),
        ],
        compiler_params=_cparams(("arbitrary",)),
        name="moe_combine",
    )(route, ys, x, g2, gates_col, final_g)


def _moe_plan(counts, n_tiles):
    counts = counts.astype(jnp.int32)
    tiles = (counts + EXPERT_TILE - 1) // EXPERT_TILE
    tile_end = jnp.cumsum(tiles)
    pend = tile_end * EXPERT_TILE
    pstart = pend - tiles * EXPERT_TILE
    n_used = tile_end[-1:]
    tile = jnp.minimum(jnp.arange(n_tiles), n_used - 1)
    tile_e = jnp.sum((tile_end[None, :] <= tile[:, None]).astype(jnp.int32), axis=1)
    return pstart, pend, tile_e, n_used


def _tri(n, rel):
    r = jnp.arange(n)[:, None]
    c = jnp.arange(n)[None, :]
    return rel(r, c).astype(BF16)


def _constants(s):
    return dict(
        tab=_rope_tables(s),
        tri_le=_tri(ATT_TILE, lambda r, c: c <= r),
        tri_gt=_tri(ATT_TILE, lambda r, c: r > c),
        tri_lt=_tri(ATT_TILE, lambda r, c: r < c),
        tri_lt_tok=_tri(TOK_TILE, lambda r, c: r < c),
    )


def _prep_layer(p, l):
    bf = jnp.zeros((1, LANES), F32).at[0, MISC_FA:MISC_FA + HEADS].set(p["b_forget"][l].astype(F32))
    w, wt = _reorder_w_in(p["w_in"][l])
    return dict(
        w_in=w, w_in_t=wt,
        wkv=jnp.concatenate([p["w_uk"][l], p["w_uv"][l]], axis=-1).astype(BF16),
        wuvt=p["w_uv"][l].T.astype(BF16),
        kvg=p["kv_norm_g"][l].reshape(1, KV_RANK).astype(F32),
        bf=bf,
        g1=p["norm1_g"][l].reshape(1, -1).astype(F32),
        chunk_bias=_chunk_bias_table(p["rel_bias"][l], ATT_TILE).transpose(0, 2, 1),
    )


def _mixers(x, sc1, sh1, lp, cst):
    tq = ATT_TILE
    (fq, fk, fvt, dq, dkv, dvt, dqi, dkit, dwt, cq, ck, cvt, eq, ek, ev) = _input_stage(
        x, sc1, sh1, lp["g1"], lp["w_in"], lp["w_in_t"], lp["wkv"], lp["wuvt"], lp["kvg"], lp["bf"],
        cst["tri_le"], cst["tab"], tq)
    ya = _fox_attention(fq, fk, fvt, tq)
    yb = _dsa_attention(dq, dkv, dvt, dqi, dkit, dwt, cst["tri_gt"], tq)
    yc = _sb_attention(cq, ck, cvt, cst["tri_lt"], tq)
    yd = _chunk_attention(eq, ek, ev, lp["chunk_bias"], tq)
    return ya, yb, yc, yd


def _moe(p, l, x, h, route, gate, counts, g2, tiles_per_batch, final):
    t, d = x.shape
    nt = t // TOK_TILE
    n_tiles = t * TOP_K // EXPERT_TILE + N_EXPERTS
    pstart, pend, tile_e, n_used = _moe_plan(counts, n_tiles)
    ids, ranks = route[:, :TOP_K, :], route[:, TOP_K:, :]
    first_row = sum(jnp.where(ids == e, pstart[e], 0) for e in range(N_EXPERTS))
    route = (first_row + ranks).reshape(nt, TOP_K * TOK_TILE)
    xs = _dispatch(pstart, pend, route, h.reshape(t, ROW_CHUNKS, LANES), n_tiles * EXPERT_TILE)
    ys = _experts(l, tile_e, n_used, xs, p["w_gu_bf16"], p["b_gu"][:, :, None, :].astype(F32),
                  p["w_dn_bf16"], p["b_dn"][:, :, None, :].astype(F32))
    gates_col = gate.transpose(0, 2, 1).reshape(t, ROUTE_ROWS)
    return _combine(route, ys, x, g2, gates_col, p["final_g"].reshape(1, d).astype(F32), tiles_per_batch, final)


def kernel(x, c, w_ada, b_ada, norm1_g, norm2_g, w_in, b_forget, kv_norm_g, w_uk, w_uv, rel_bias, w_o,
           w_router, b_router, w_gu, b_gu, w_dn, b_dn, final_g):
    p = dict(w_in=w_in, b_forget=b_forget, kv_norm_g=kv_norm_g, w_uk=w_uk, w_uv=w_uv, rel_bias=rel_bias,
             norm1_g=norm1_g, w_gu_bf16=w_gu.astype(BF16), b_gu=b_gu, w_dn_bf16=w_dn.astype(BF16), b_dn=b_dn,
             final_g=final_g)
    b, s, d = x.shape
    t = b * s
    depth = w_ada.shape[0]
    tiles_per_batch = s // TOK_TILE
    mod = _modulation(c, w_ada, b_ada)
    cst = _constants(s)
    for l in range(depth):
        sh1, sc1, g1, sh2, sc2, g2 = [m[:, None, :] for m in jnp.split(mod[l], 6, axis=-1)]
        ys = _mixers(x, sc1, sh1, _prep_layer(p, l), cst)
        x2, h, route, gate, cnt = _output_stage(
            [y.reshape(t, GROUP_WIDTH) for y in ys], w_o[l].astype(BF16), x.reshape(t, d), g1, sc2, sh2,
            norm2_g[l].reshape(1, d).astype(F32), w_router[l].T.astype(BF16),
            b_router[l].reshape(N_EXPERTS, 1).astype(F32), cst["tri_lt_tok"], tiles_per_batch)
        x = _moe(p, l, x2, h, route, gate, cnt[:, 0], g2, tiles_per_batch, l == depth - 1).reshape(b, s, d)
    return x
```
